```python
import math
import jax
import jax.numpy as jnp
from jax import lax
import numpy as np


D_MODEL = 1024
BATCH = 32
SEQ = 2048
DEPTH = 4

N_MIXERS = 3
NSA_HEADS = 16
NSA_GROUPS = 4
NSA_HPG = NSA_HEADS // NSA_GROUPS
HEAD_DIM = D_MODEL // NSA_HEADS
CMP_BLOCK = 32
CMP_STRIDE = 16
SEL_BLOCK = 64
SEL_TOP = 16
SEL_QCHUNK = 16
WINDOW = 512
WIN_QBLOCK = 128
ROPE_THETA = 10000.0
FORCE_SCORE = 1e6
NSA_Q_COLS = NSA_HEADS * HEAD_DIM
NSA_KV_COLS = NSA_GROUPS * HEAD_DIM
NSA_IN_COLS = NSA_Q_COLS + 6 * NSA_KV_COLS + 3 * NSA_HEADS
SC_WIDTH = 3
LRU_WIDTH = 1280
LRU_BLOCKS = 16
LRU_BW = LRU_WIDTH // LRU_BLOCKS
LRU_CONV = 4
LRU_C = 8.0
N_EXPERTS = 16
N_GROUPS = 4
EXPERTS_PER_GROUP = N_EXPERTS // N_GROUPS
TOP_GROUPS = 1
TOP_K = 2
D_FF = 1024
DN_ALPHA = (2.0 * DEPTH) ** 0.25
DN_BETA = (8.0 * DEPTH) ** -0.25
LN_EPS = 1e-5
NEG_INF = -1e30

N_NSA = (DEPTH + 2) // 3
N_SC = (DEPTH + 1) // 3
N_LRU = DEPTH // 3

kernel_name = 'hybrid_nsa_shortconv_rglru_moe'


def layer_norm(x, g, b):
    xf = x.astype(jnp.float32)
    mu = xf.mean(-1, keepdims=True)
    var = jnp.square(xf - mu).mean(-1, keepdims=True)
    y = (xf - mu) * lax.rsqrt(var + LN_EPS) * g.astype(jnp.float32) + b.astype(jnp.float32)
    return y.astype(x.dtype)


def rope(x, pos):
    half = x.shape[-1] // 2
    inv = ROPE_THETA ** (-jnp.arange(half, dtype=jnp.float32) / half)
    ang = pos.astype(jnp.float32)[..., None] * inv
    ang = ang.reshape(ang.shape[:2] + (1,) * (x.ndim - 3) + (half,))
    cos = jnp.cos(ang).astype(x.dtype)
    sin = jnp.sin(ang).astype(x.dtype)
    x1, x2 = x[..., :half], x[..., half:]
    return jnp.concatenate([x1 * cos - x2 * sin, x1 * sin + x2 * cos], axis=-1)


def masked_softmax(s, mask):
    s = jnp.where(mask, s, NEG_INF)
    p = jax.nn.softmax(s, axis=-1)
    return jnp.where(mask, p, 0.0)


def causal_dwconv(u, w, b):
    width = w.shape[0]
    S = u.shape[1]
    up = jnp.pad(u, ((0, 0), (width - 1, 0), (0, 0)))
    y = b
    for k in range(width):
        y = y + w[k] * up[:, k:k + S]
    return y


def nsa_mixer(x, pos, w_in, b_gate, pos_k, w1_k, w2_k, pos_v, w1_v, w2_v, w_out):
    B, S, _ = x.shape
    G, R, Dh = NSA_GROUPS, NSA_HPG, HEAD_DIM
    f32 = jnp.float32
    t = jnp.arange(S)
    split_at = [int(v) for v in np.cumsum([NSA_Q_COLS] + [NSA_KV_COLS] * 6)]
    q, kc, vc, ks, vs, kw, vw, gl = jnp.split(x @ w_in, split_at, axis=-1)
    q = rope(q.reshape(B, S, G, R, Dh), pos) * (Dh ** -0.5)
    kc = kc.reshape(B, S, G, Dh)
    vc = vc.reshape(B, S, G, Dh)
    ks = rope(ks.reshape(B, S, G, Dh), pos)
    vs = vs.reshape(B, S, G, Dh)
    kw = rope(kw.reshape(B, S, G, Dh), pos)
    vw = vw.reshape(B, S, G, Dh)
    gates = jax.nn.sigmoid(gl + b_gate).reshape(B, S, G, R, 3)

    n_cmp = (S - CMP_BLOCK) // CMP_STRIDE + 1
    cmp_start = np.arange(n_cmp) * CMP_STRIDE
    cmp_end = cmp_start + CMP_BLOCK - 1
    blk_idx = cmp_start[:, None] + np.arange(CMP_BLOCK)[None, :]

    def compress(tok, pe, w1, w2):
        blocks = tok[:, blk_idx] + pe[:, None, :]
        flat = blocks.transpose(0, 1, 3, 2, 4).reshape(B, n_cmp, G, CMP_BLOCK * Dh)
        return jax.nn.gelu(flat @ w1) @ w2

    k_cmp = rope(compress(kc, pos_k, w1_k, w2_k), pos[:, cmp_end])
    v_cmp = compress(vc, pos_v, w1_v, w2_v)
    s_cmp = jnp.einsum('bsgrd,bngd->bgrsn', q, k_cmp).astype(f32)
    p_cmp = masked_softmax(s_cmp, jnp.asarray(cmp_end)[None, :] <= t[:, None])
    o_cmp = jnp.einsum('bgrsn,bngd->bsgrd', p_cmp.astype(x.dtype), v_cmp)

    n_sel = S // SEL_BLOCK
    k_sel = min(SEL_TOP, n_sel)
    sel_start = np.arange(n_sel) * SEL_BLOCK
    overlap = jnp.asarray(((cmp_start[:, None] <= sel_start[None, :] + SEL_BLOCK - 1)
                           & (cmp_end[:, None] >= sel_start[None, :])).astype(np.float32))
    imp = jnp.einsum('bgrsn,nm->bgsm', p_cmp, overlap)
    blk_ar = jnp.arange(n_sel)
    rel = (t // SEL_BLOCK)[:, None] - blk_ar[None, :]
    imp = jnp.where(rel >= 0, imp, -1.0)
    forced = (blk_ar[None, :] == 0) | (rel == 0) | (rel == 1)
    imp = jnp.where(forced, FORCE_SCORE, imp)
    _, sel_idx = lax.top_k(imp, k_sel)

    ks_b = ks.reshape(B, n_sel, SEL_BLOCK, G, Dh).transpose(0, 3, 1, 2, 4)
    vs_b = vs.reshape(B, n_sel, SEL_BLOCK, G, Dh).transpose(0, 3, 1, 2, 4)
    n_qc = S // SEL_QCHUNK
    q_ch = q.reshape(B, n_qc, SEL_QCHUNK, G, R, Dh).swapaxes(0, 1)
    idx_ch = sel_idx.reshape(B, G, n_qc, SEL_QCHUNK, k_sel).transpose(2, 0, 1, 3, 4)
    t_ch = t.reshape(n_qc, SEL_QCHUNK)
    b_ar = jnp.arange(B)[:, None, None, None]
    g_ar = jnp.arange(G)[None, :, None, None]
    in_blk = jnp.arange(SEL_BLOCK)

    def sel_chunk(args):
        qc, ic, tc = args
        kg = ks_b[b_ar, g_ar, ic]
        vg = vs_b[b_ar, g_ar, ic]
        s = jnp.einsum('bqgrd,bgqkld->bgrqkl', qc, kg).astype(f32)
        key_pos = ic[..., None] * SEL_BLOCK + in_blk
        m = (key_pos <= tc[None, None, :, None, None])[:, :, None]
        p = masked_softmax(s.reshape(B, G, R, SEL_QCHUNK, -1), m.reshape(B, G, 1, SEL_QCHUNK, -1))
        return jnp.einsum('bgrqkl,bgqkld->bqgrd', p.reshape(s.shape).astype(x.dtype), vg)

    o_slc = lax.map(sel_chunk, (q_ch, idx_ch, t_ch)).swapaxes(0, 1).reshape(B, S, G, R, Dh)

    kw_p = jnp.pad(kw, ((0, 0), (WINDOW, 0), (0, 0), (0, 0)))
    vw_p = jnp.pad(vw, ((0, 0), (WINDOW, 0), (0, 0), (0, 0)))
    n_wb = S // WIN_QBLOCK
    q_wb = q.reshape(B, n_wb, WIN_QBLOCK, G, R, Dh).swapaxes(0, 1)
    span = WINDOW + WIN_QBLOCK

    def win_block(args):
        qb, blk = args
        start = blk * WIN_QBLOCK
        kb = lax.dynamic_slice_in_dim(kw_p, start, span, axis=1)
        vb = lax.dynamic_slice_in_dim(vw_p, start, span, axis=1)
        tq = start + jnp.arange(WIN_QBLOCK)
        tk = start - WINDOW + jnp.arange(span)
        d = tq[:, None] - tk[None, :]
        m = (d >= 0) & (d < WINDOW) & (tk[None, :] >= 0)
        s = jnp.einsum('bqgrd,bkgd->bgrqk', qb, kb).astype(f32)
        p = masked_softmax(s, m)
        return jnp.einsum('bgrqk,bkgd->bqgrd', p.astype(x.dtype), vb)

    o_win = lax.map(win_block, (q_wb, jnp.arange(n_wb))).swapaxes(0, 1).reshape(B, S, G, R, Dh)

    o = gates[..., 0:1] * o_cmp + gates[..., 1:2] * o_slc + gates[..., 2:3] * o_win
    return o.reshape(B, S, NSA_Q_COLS) @ w_out


def shortconv_mixer(x, w_in, conv_w, conv_b, w_out):
    u, bg, cg = jnp.split(x @ w_in, 3, axis=-1)
    z = causal_dwconv(cg * u, conv_w, conv_b)
    return (bg * z) @ w_out


def rglru_mixer(x, w_in, conv_w, conv_b, wa, ba, wx, bx, lam, w_out):
    B, S, _ = x.shape
    f32 = jnp.float32
    gate_br, rec = jnp.split(x @ w_in, 2, axis=-1)
    u = causal_dwconv(rec, conv_w, conv_b)
    ub = u.reshape(B, S, LRU_BLOCKS, LRU_BW)
    r = jax.nn.sigmoid(jnp.einsum('bshi,hij->bshj', ub, wa).reshape(B, S, LRU_WIDTH) + ba)
    ig = jax.nn.sigmoid(jnp.einsum('bshi,hij->bshj', ub, wx).reshape(B, S, LRU_WIDTH) + bx)
    log_a = LRU_C * r.astype(f32) * jax.nn.log_sigmoid(lam.astype(f32))
    a = jnp.exp(log_a)
    b_in = jnp.sqrt(-jnp.expm1(2.0 * log_a)) * (ig * u).astype(f32)

    def combine(c1, c2):
        a1, b1 = c1
        a2, b2 = c2
        return a1 * a2, a2 * b1 + b2

    _, h = lax.associative_scan(combine, (a, b_in), axis=1)
    y = jax.nn.gelu(gate_br) * h.astype(x.dtype)
    return y @ w_out


def moe_ffn(x, router_w, router_b, w_gate, w_up, w_down):
    B, S, D = x.shape
    f32 = jnp.float32
    xt = x.reshape(B * S, D)
    scores = jax.nn.sigmoid((xt @ router_w).astype(f32))
    biased = scores + router_b.astype(f32)
    grp = biased.reshape(-1, N_GROUPS, EXPERTS_PER_GROUP)
    grp_score = lax.top_k(grp, 2)[0].sum(-1)
    _, top_g = lax.top_k(grp_score, TOP_GROUPS)
    gmask = jax.nn.one_hot(top_g, N_GROUPS, dtype=f32).sum(-2) > 0
    emask = jnp.repeat(gmask, EXPERTS_PER_GROUP, axis=-1)
    _, top_e = lax.top_k(jnp.where(emask, biased, NEG_INF), TOP_K)
    w = jnp.take_along_axis(scores, top_e, axis=-1)
    w = w / w.sum(-1, keepdims=True)
    gate = (jax.nn.one_hot(top_e, N_EXPERTS, dtype=f32) * w[..., None]).sum(-2).astype(x.dtype)
    y = jnp.zeros_like(xt)
    for e in range(N_EXPERTS):
        hid = jax.nn.silu(xt @ w_gate[e]) * (xt @ w_up[e])
        y = y + gate[:, e:e + 1] * (hid @ w_down[e])
    return y.reshape(B, S, D)


def setup_inputs(seed: int = 0) -> dict:
    key = jax.random.key(seed)
    ks = jax.random.split(key, 40)
    f32 = jnp.float32

    def nrm(i, shape, scale):
        return jax.random.normal(ks[i], shape, f32) * scale

    D = D_MODEL
    x = nrm(0, (BATCH, SEQ, D), 1.0)
    offset = jax.random.randint(ks[1], (BATCH, 1), 0, 4096, dtype=jnp.int32)
    positions = offset + jnp.arange(SEQ, dtype=jnp.int32)[None, :]
    lru_u = jax.random.uniform(ks[24], (N_LRU, LRU_WIDTH), f32, minval=0.9, maxval=0.999)
    lru_s = lru_u ** (1.0 / LRU_C)
    return {
        'x': x,
        'positions': positions,
        'nsa_w_in': nrm(2, (N_NSA, D, NSA_IN_COLS), D ** -0.5),
        'nsa_b_gate': nrm(3, (N_NSA, 3 * NSA_HEADS), 0.1),
        'nsa_cmp_pos_k': nrm(4, (N_NSA, CMP_BLOCK, HEAD_DIM), 0.02),
        'nsa_cmp_w1_k': nrm(5, (N_NSA, CMP_BLOCK * HEAD_DIM, HEAD_DIM), (CMP_BLOCK * HEAD_DIM) ** -0.5),
        'nsa_cmp_w2_k': nrm(6, (N_NSA, HEAD_DIM, HEAD_DIM), HEAD_DIM ** -0.5),
        'nsa_cmp_pos_v': nrm(7, (N_NSA, CMP_BLOCK, HEAD_DIM), 0.02),
        'nsa_cmp_w1_v': nrm(8, (N_NSA, CMP_BLOCK * HEAD_DIM, HEAD_DIM), (CMP_BLOCK * HEAD_DIM) ** -0.5),
        'nsa_cmp_w2_v': nrm(9, (N_NSA, HEAD_DIM, HEAD_DIM), HEAD_DIM ** -0.5),
        'nsa_w_out': nrm(10, (N_NSA, NSA_Q_COLS, D), DN_BETA * NSA_Q_COLS ** -0.5),
        'sc_w_in': nrm(11, (N_SC, D, 3 * D), D ** -0.5),
        'sc_conv_w': nrm(12, (N_SC, SC_WIDTH, D), SC_WIDTH ** -0.5),
        'sc_conv_b': nrm(13, (N_SC, D), 0.01),
        'sc_w_out': nrm(14, (N_SC, D, D), DN_BETA * D ** -0.5),
        'lru_w_in': nrm(15, (N_LRU, D, 2 * LRU_WIDTH), D ** -0.5),
        'lru_conv_w': nrm(16, (N_LRU, LRU_CONV, LRU_WIDTH), LRU_CONV ** -0.5),
        'lru_conv_b': nrm(17, (N_LRU, LRU_WIDTH), 0.01),
        'lru_wa': nrm(18, (N_LRU, LRU_BLOCKS, LRU_BW, LRU_BW), LRU_BW ** -0.5),
        'lru_ba': nrm(19, (N_LRU, LRU_WIDTH), 0.1),
        'lru_wx': nrm(20, (N_LRU, LRU_BLOCKS, LRU_BW, LRU_BW), LRU_BW ** -0.5),
        'lru_bx': nrm(21, (N_LRU, LRU_WIDTH), 0.1),
        'lru_lambda': jnp.log(lru_s) - jnp.log1p(-lru_s),
        'lru_w_out': nrm(22, (N_LRU, LRU_WIDTH, D), DN_BETA * LRU_WIDTH ** -0.5),
        'router_w': nrm(25, (D, N_EXPERTS), D ** -0.5),
        'router_b': nrm(26, (N_EXPERTS,), 0.01),
        'moe_w_gate': nrm(27, (DEPTH, N_EXPERTS, D, D_FF), D ** -0.5),
        'moe_w_up': nrm(28, (DEPTH, N_EXPERTS, D, D_FF), D ** -0.5),
        'moe_w_down': nrm(29, (DEPTH, N_EXPERTS, D_FF, D), DN_BETA * D_FF ** -0.5),
        'ln_g': 1.0 + nrm(30, (DEPTH, 2, D), 0.02),
        'ln_b': nrm(31, (DEPTH, 2, D), 0.02),
    }


def reference(x, positions, nsa_w_in, nsa_b_gate, nsa_cmp_pos_k, nsa_cmp_w1_k, nsa_cmp_w2_k,
              nsa_cmp_pos_v, nsa_cmp_w1_v, nsa_cmp_w2_v, nsa_w_out, sc_w_in, sc_conv_w, sc_conv_b,
              sc_w_out, lru_w_in, lru_conv_w, lru_conv_b, lru_wa, lru_ba, lru_wx, lru_bx, lru_lambda,
              lru_w_out, router_w, router_b, moe_w_gate, moe_w_up, moe_w_down, ln_g, ln_b):
    for i in range(DEPTH):
        kind, j = i % N_MIXERS, i // N_MIXERS
        if kind == 0:
            m = nsa_mixer(x, positions, nsa_w_in[j], nsa_b_gate[j], nsa_cmp_pos_k[j], nsa_cmp_w1_k[j],
                          nsa_cmp_w2_k[j], nsa_cmp_pos_v[j], nsa_cmp_w1_v[j], nsa_cmp_w2_v[j], nsa_w_out[j])
        elif kind == 1:
            m = shortconv_mixer(x, sc_w_in[j], sc_conv_w[j], sc_conv_b[j], sc_w_out[j])
        else:
            m = rglru_mixer(x, lru_w_in[j], lru_conv_w[j], lru_conv_b[j], lru_wa[j], lru_ba[j],
                            lru_wx[j], lru_bx[j], lru_lambda[j], lru_w_out[j])
        x = layer_norm(DN_ALPHA * x + m, ln_g[i, 0], ln_b[i, 0])
        f = moe_ffn(x, router_w, router_b, moe_w_gate[i], moe_w_up[i], moe_w_down[i])
        x = layer_norm(DN_ALPHA * x + f, ln_g[i, 1], ln_b[i, 1])
    return x
```

```python
import functools

import numpy as np
import jax
import jax.numpy as jnp
from jax import lax
from jax.experimental import pallas as pl
from jax.experimental.pallas import tpu as pltpu

F32 = jnp.float32
BF16 = jnp.bfloat16
MXU_DTYPE = BF16

D_MODEL = 1024
DEPTH = 4
N_MIXERS = 3
NSA_HEADS = 16
NSA_GROUPS = 4
NSA_HPG = NSA_HEADS // NSA_GROUPS
HEAD_DIM = D_MODEL // NSA_HEADS
CMP_BLOCK = 32
CMP_STRIDE = 16
SEL_BLOCK = 64
SEL_TOP = 16
WINDOW = 512
ROPE_THETA = 10000.0
FORCE_SCORE = 1e6
NSA_Q_COLS = NSA_HEADS * HEAD_DIM
NSA_KV_COLS = NSA_GROUPS * HEAD_DIM
SC_WIDTH = 3
LRU_WIDTH = 1280
LRU_BLOCKS = 16
LRU_BW = LRU_WIDTH // LRU_BLOCKS
LRU_CONV = 4
LRU_C = 8.0
N_EXPERTS = 16
N_GROUPS = 4
EXPERTS_PER_GROUP = N_EXPERTS // N_GROUPS
TOP_K = 2
D_FF = 1024
DN_ALPHA = (2.0 * DEPTH) ** 0.25
LN_EPS = 1e-5
NEG_INF = -1e30

LANES = 128
SUBLANES = 8
VMEM_LIMIT_BYTES = 56 * 1024 * 1024
ROW_TILE = 512
LRU_ROW_TILE = 256
ATT_TQ = 256
ATT_TK = 256
MOE_TOKENS = 1024
MOE_CHUNK = 128
MOE_ALIGN = SUBLANES


def _cparams(sem):
    return pltpu.CompilerParams(dimension_semantics=sem, vmem_limit_bytes=VMEM_LIMIT_BYTES)


def _dot(a, b):
    return jnp.dot(a, b, preferred_element_type=F32)


def _dot_nt(a, b, precision=None):
    return lax.dot_general(a, b, (((1,), (1,)), ((), ())), preferred_element_type=F32,
                           precision=precision)


def _layer_norm_rows(y, g, b):
    mu = jnp.mean(y, axis=-1, keepdims=True)
    d = y - mu
    var = jnp.mean(d * d, axis=-1, keepdims=True)
    return d * lax.rsqrt(var + LN_EPS) * g + b


def _const_spec(shape):
    nd = len(shape)
    return pl.BlockSpec(shape, lambda *_: (0,) * nd)


def _rope_tables_kernel(pos_ref, inv_ref, cos_ref, sina_ref, sinb_ref):
    ang = pos_ref[...] * inv_ref[...]
    c = jnp.cos(ang)
    s = jnp.sin(ang)
    lane = lax.broadcasted_iota(jnp.int32, ang.shape, 1)
    first = (lane % HEAD_DIM) < (HEAD_DIM // 2)
    cos_ref[...] = c
    sina_ref[...] = jnp.where(first, -s, 0.0)
    sinb_ref[...] = jnp.where(first, 0.0, s)


def _rope_tables(pos_flat):
    n = pos_flat.shape[0]
    tm = min(n, 1024)
    half = HEAD_DIM // 2
    inv = ROPE_THETA ** (-jnp.arange(half, dtype=F32) / half)
    inv_lane = jnp.tile(inv, LANES // half)[None, :]
    posb = jnp.broadcast_to(pos_flat.astype(F32)[:, None], (n, LANES))
    spec = pl.BlockSpec((tm, LANES), lambda i: (i, 0))
    out = jax.ShapeDtypeStruct((n, LANES), F32)
    return pl.pallas_call(
        _rope_tables_kernel,
        grid=(n // tm,),
        in_specs=[spec, _const_spec((1, LANES))],
        out_specs=[spec, spec, spec],
        out_shape=[out, out, out],
        compiler_params=_cparams(("arbitrary",)),
        name="rope_tables",
    )(posb, inv_lane)


def _rope128(v, cos, sina, sinb):
    half = HEAD_DIM // 2
    return (v * cos + pltpu.roll(v, LANES - half, 1) * sina + pltpu.roll(v, half, 1) * sinb)


_NSA_SEG = {
    "q": 0, "ks": 1024, "kw": 1280, "kc": 1536, "vc": 1792, "vs": 2048, "vw": 2304, "gate": 2560,
}
_NSA_PROJ_COLS = 2560 + LANES


def _nsa_proj_kernel(x_ref, w_ref, bg_ref, cos_ref, sina_ref, sinb_ref,
                     q_ref, ks_ref, kw_ref, kc_ref, vc_ref, vs_ref, vw_ref, g_ref):
    xb = x_ref[...].astype(MXU_DTYPE)
    cos = cos_ref[...]
    sina = sina_ref[...]
    sinb = sinb_ref[...]
    scale = HEAD_DIM ** -0.5

    def seg(name, width=256, c=0):
        o = _NSA_SEG[name] + c * 256
        return _dot(xb, w_ref[:, o:o + width])

    for c in range(NSA_Q_COLS // 256):
        acc = seg("q", c=c)
        for h in range(2):
            r = _rope128(acc[:, h * LANES:(h + 1) * LANES], cos, sina, sinb) * scale
            q_ref[:, c * 256 + h * LANES:c * 256 + (h + 1) * LANES] = r.astype(q_ref.dtype)
    for name, ref in (("ks", ks_ref), ("kw", kw_ref)):
        acc = seg(name)
        for h in range(2):
            r = _rope128(acc[:, h * LANES:(h + 1) * LANES], cos, sina, sinb)
            ref[:, h * LANES:(h + 1) * LANES] = r.astype(ref.dtype)
    kc_ref[...] = seg("kc")
    vc_ref[...] = seg("vc")
    vs_ref[...] = seg("vs").astype(vs_ref.dtype)
    vw_ref[...] = seg("vw").astype(vw_ref.dtype)
    g_ref[...] = jax.nn.sigmoid(seg("gate", width=LANES) + bg_ref[...])


def _nsa_proj(x2, w_perm, bg_perm, cos_t, sina_t, sinb_t):
    t = x2.shape[0]
    tm = ROW_TILE
    row = lambda w: pl.BlockSpec((tm, w), lambda i: (i, 0))
    outs = [
        jax.ShapeDtypeStruct((t, NSA_Q_COLS), MXU_DTYPE),
        jax.ShapeDtypeStruct((t, NSA_KV_COLS), MXU_DTYPE),
        jax.ShapeDtypeStruct((t, NSA_KV_COLS), MXU_DTYPE),
        jax.ShapeDtypeStruct((t, NSA_KV_COLS), F32),
        jax.ShapeDtypeStruct((t, NSA_KV_COLS), F32),
        jax.ShapeDtypeStruct((t, NSA_KV_COLS), MXU_DTYPE),
        jax.ShapeDtypeStruct((t, NSA_KV_COLS), MXU_DTYPE),
        jax.ShapeDtypeStruct((t, LANES), F32),
    ]
    return pl.pallas_call(
        _nsa_proj_kernel,
        grid=(t // tm,),
        in_specs=[row(D_MODEL), _const_spec((D_MODEL, _NSA_PROJ_COLS)), _const_spec((1, LANES)),
                  row(LANES), row(LANES), row(LANES)],
        out_specs=[row(NSA_Q_COLS)] + [row(NSA_KV_COLS)] * 6 + [row(LANES)],
        out_shape=outs,
        compiler_params=_cparams(("arbitrary",)),
        name="nsa_proj",
    )(x2, w_perm, bg_perm, cos_t, sina_t, sinb_t)


def _nsa_compress_kernel(kc_ref, vc_ref, pek_ref, pev_ref, w1k_ref, w1v_ref, w2k_ref, w2v_ref,
                         cos_ref, sina_ref, sinb_ref, ko_ref, vo_ref):
    nc = kc_ref.shape[1]
    half_feat = kc_ref.shape[2]
    rows = lax.broadcasted_iota(jnp.int32, (nc, 1), 0)
    valid = rows < nc - 1

    def compress(tok, pe_ref, w1_ref, w2_ref):
        a = (tok + pe_ref[:, :half_feat]).astype(MXU_DTYPE)
        b = (tok + pe_ref[:, half_feat:]).astype(MXU_DTYPE)
        pa = _dot(a, w1_ref[:half_feat, :])
        pb = _dot(b, w1_ref[half_feat:, :])
        pre = pa + pltpu.roll(pb, nc - 1, 0)
        h = jax.nn.gelu(pre)
        return _dot(h.astype(MXU_DTYPE), w2_ref[...])

    kcmp = compress(kc_ref[0], pek_ref, w1k_ref, w2k_ref)
    vcmp = compress(vc_ref[0], pev_ref, w1v_ref, w2v_ref)
    cos = cos_ref[0]
    sina = sina_ref[0]
    sinb = sinb_ref[0]
    for h in range(NSA_KV_COLS // LANES):
        r = _rope128(kcmp[:, h * LANES:(h + 1) * LANES], cos, sina, sinb)
        ko_ref[0, :, h * LANES:(h + 1) * LANES] = jnp.where(valid, r, 0.0).astype(ko_ref.dtype)
    vo_ref[0] = jnp.where(valid, vcmp, 0.0).astype(vo_ref.dtype)


def _nsa_compress(kc3, vc3, pek, pev, w1k, w1v, w2k, w2v, cos_c, sina_c, sinb_c):
    b, nc, feat = kc3.shape
    tok = pl.BlockSpec((1, nc, feat), lambda i: (i, 0, 0))
    tab = pl.BlockSpec((1, nc, LANES), lambda i: (i, 0, 0))
    out = pl.BlockSpec((1, nc, NSA_KV_COLS), lambda i: (i, 0, 0))
    oshape = jax.ShapeDtypeStruct((b, nc, NSA_KV_COLS), MXU_DTYPE)
    return pl.pallas_call(
        _nsa_compress_kernel,
        grid=(b,),
        in_specs=[tok, tok, _const_spec(pek.shape), _const_spec(pev.shape),
                  _const_spec(w1k.shape), _const_spec(w1v.shape),
                  _const_spec(w2k.shape), _const_spec(w2v.shape), tab, tab, tab],
        out_specs=[out, out],
        out_shape=[oshape, oshape],
        compiler_params=_cparams(("arbitrary",)),
        name="nsa_compress",
    )(kc3, vc3, pek, pev, w1k, w1v, w2k, w2v, cos_c, sina_c, sinb_c)


def _nsa_attn_kernel(q_ref, ks_ref, vs_ref, kw_ref, vw_ref, kc_ref, vc_ref, g_ref, ov_ref, et_ref,
                     o_ref, m_ref, l_ref, acc_ref, *, n_sel, k_sel):
    tq = q_ref.shape[1]
    tk = ATT_TK
    nc = kc_ref.shape[1]
    R = NSA_HPG
    qi = pl.program_id(1)
    t0 = qi * tq
    trow = t0 + lax.broadcasted_iota(jnp.int32, (tq, 1), 0)
    lane_grp = lax.broadcasted_iota(jnp.int32, (1, NSA_KV_COLS), 1) // HEAD_DIM
    gates = g_ref[0]
    out_r = [jnp.zeros((tq, NSA_KV_COLS), F32) for _ in range(R)]

    def flash(qs, k_ref, v_ref, j_lo, j_hi, mask_fn):
        m_ref[...] = jnp.full(m_ref.shape, NEG_INF, F32)
        l_ref[...] = jnp.zeros(l_ref.shape, F32)
        acc_ref[...] = jnp.zeros(acc_ref.shape, F32)

        def body(j, carry):
            k0 = pl.multiple_of(j * tk, tk)
            kt = k_ref[0, pl.ds(k0, tk), :]
            vt = v_ref[0, pl.ds(k0, tk), :]
            kpos = k0 + lax.broadcasted_iota(jnp.int32, (1, tk), 1)
            mask = mask_fn(k0, kpos)[None]
            s3 = _dot_nt(qs, kt).reshape(R, tq, tk)
            s3 = jnp.where(mask, s3, NEG_INF)
            m_old = m_ref[...].reshape(R, tq, 1)
            m_new = jnp.maximum(m_old, jnp.max(s3, axis=-1, keepdims=True))
            alpha = jnp.exp(m_old - m_new)
            p = jnp.where(mask, jnp.exp(s3 - m_new), 0.0)
            l_ref[...] = (alpha * l_ref[...].reshape(R, tq, 1)
                          + jnp.sum(p, axis=-1, keepdims=True)).reshape(R * tq, 1)
            m_ref[...] = m_new.reshape(R * tq, 1)
            pv = _dot(p.reshape(R * tq, tk).astype(MXU_DTYPE), vt)
            acc_ref[...] = alpha.reshape(R * tq, 1) * acc_ref[...] + pv
            return carry

        lax.fori_loop(j_lo, j_hi, body, 0)
        return acc_ref[...] / l_ref[...]

    for g in range(NSA_GROUPS):
        gmask = lane_grp == g
        qs = jnp.concatenate(
            [jnp.where(gmask, q_ref[0, :, r * NSA_KV_COLS:(r + 1) * NSA_KV_COLS], 0)
             for r in range(R)], axis=0)

        n_idx = lax.broadcasted_iota(jnp.int32, (1, nc), 1)
        cmp_ok = ((n_idx * CMP_STRIDE + CMP_BLOCK - 1) <= trow)[None]
        s3 = _dot_nt(qs, kc_ref[0]).reshape(R, tq, nc)
        s3 = jnp.where(cmp_ok, s3, NEG_INF)
        mx = jnp.max(s3, axis=-1, keepdims=True)
        e = jnp.where(cmp_ok, jnp.exp(s3 - mx), 0.0)
        den = jnp.sum(e, axis=-1, keepdims=True)
        p3 = e / jnp.maximum(den, 1e-30)
        o_cmp = _dot(p3.reshape(R * tq, nc).astype(MXU_DTYPE), vc_ref[0])

        psum = jnp.sum(p3, axis=0)
        imp = jnp.dot(psum, ov_ref[...], preferred_element_type=F32,
                      precision=lax.Precision.HIGHEST)
        blk = lax.broadcasted_iota(jnp.int32, (1, LANES), 1)
        rel = trow // SEL_BLOCK - blk
        imp = jnp.where(rel >= 0, imp, -1.0)
        forced = (blk == 0) | (rel == 0) | (rel == 1)
        imp = jnp.where(forced, FORCE_SCORE, imp)
        imp_t = jnp.transpose(imp)[:n_sel, :]
        midx = lax.broadcasted_iota(jnp.int32, (n_sel, 1), 0)
        cnt = jnp.zeros((n_sel, tq), F32)
        for mp in range(n_sel):
            row = imp_t[mp:mp + 1, :]
            beats = (row > imp_t) | ((row == imp_t) & (midx > mp))
            cnt = cnt + jnp.where(beats, 1.0, 0.0)
        sel_t = jnp.where(cnt < k_sel, 1.0, 0.0)
        if n_sel < LANES:
            sel_t = jnp.concatenate([sel_t, jnp.zeros((LANES - n_sel, tq), F32)], axis=0)
        sel = jnp.transpose(sel_t).astype(MXU_DTYPE)

        def sel_mask(k0, kpos):
            selm = _dot_nt(sel, et_ref[pl.ds(k0, tk), :])
            return (selm > 0.5) & (kpos <= trow)

        o_slc = flash(qs, ks_ref, vs_ref, 0, (t0 + tq) // tk, sel_mask)

        def win_mask(k0, kpos):
            d = trow - kpos
            return (d >= 0) & (d < WINDOW)

        j_lo = jnp.maximum(t0 - WINDOW, 0) // tk
        o_win = flash(qs, kw_ref, vw_ref, j_lo, (t0 + tq) // tk, win_mask)

        def gate_col(branch):
            return jnp.concatenate(
                [gates[:, branch * NSA_HEADS + r * NSA_GROUPS + g:
                       branch * NSA_HEADS + r * NSA_GROUPS + g + 1] for r in range(R)], axis=0)

        o_g = gate_col(0) * o_cmp + gate_col(1) * o_slc + gate_col(2) * o_win
        for r in range(R):
            out_r[r] = jnp.where(gmask, o_g[r * tq:(r + 1) * tq, :], out_r[r])

    for r in range(R):
        o_ref[0, :, r * NSA_KV_COLS:(r + 1) * NSA_KV_COLS] = out_r[r].astype(o_ref.dtype)


def _nsa_attn(q, ks, vs, kw, vw, kcmp, vcmp, gates, ov, et):
    b, s, _ = q.shape
    nc = kcmp.shape[1]
    tq = ATT_TQ
    n_sel = s // SEL_BLOCK
    k_sel = min(SEL_TOP, n_sel)
    qspec = pl.BlockSpec((1, tq, NSA_Q_COLS), lambda i, j: (i, j, 0))
    kv = pl.BlockSpec((1, s, NSA_KV_COLS), lambda i, j: (i, 0, 0))
    cmp_spec = pl.BlockSpec((1, nc, NSA_KV_COLS), lambda i, j: (i, 0, 0))
    gspec = pl.BlockSpec((1, tq, LANES), lambda i, j: (i, j, 0))
    return pl.pallas_call(
        functools.partial(_nsa_attn_kernel, n_sel=n_sel, k_sel=k_sel),
        grid=(b, s // tq),
        in_specs=[qspec, kv, kv, kv, kv, cmp_spec, cmp_spec, gspec,
                  _const_spec(ov.shape), _const_spec(et.shape)],
        out_specs=qspec,
        out_shape=jax.ShapeDtypeStruct((b, s, NSA_Q_COLS), MXU_DTYPE),
        scratch_shapes=[pltpu.VMEM((NSA_HPG * tq, 1), F32), pltpu.VMEM((NSA_HPG * tq, 1), F32),
                        pltpu.VMEM((NSA_HPG * tq, NSA_KV_COLS), F32)],
        compiler_params=_cparams(("arbitrary", "arbitrary")),
        name="nsa_attn",
    )(q, ks, vs, kw, vw, kcmp, vcmp, gates, ov, et)


def _mm_res_ln_kernel(a_ref, w_ref, x_ref, g_ref, b_ref, o_ref):
    m = _dot(a_ref[...].astype(MXU_DTYPE), w_ref[...])
    o_ref[...] = _layer_norm_rows(DN_ALPHA * x_ref[...] + m, g_ref[...], b_ref[...])


def _mm_res_ln(a, w, x2, g, b):
    t, k = a.shape
    tm = ROW_TILE
    return pl.pallas_call(
        _mm_res_ln_kernel,
        grid=(t // tm,),
        in_specs=[pl.BlockSpec((tm, k), lambda i: (i, 0)), _const_spec(w.shape),
                  pl.BlockSpec((tm, D_MODEL), lambda i: (i, 0)),
                  _const_spec((1, D_MODEL)), _const_spec((1, D_MODEL))],
        out_specs=pl.BlockSpec((tm, D_MODEL), lambda i: (i, 0)),
        out_shape=jax.ShapeDtypeStruct((t, D_MODEL), F32),
        compiler_params=_cparams(("arbitrary",)),
        name="mm_res_ln",
    )(a, w, x2, g, b)


def _shift_rows(v, prev, k):
    out = pltpu.roll(v, k, 0)
    rows = lax.broadcasted_iota(jnp.int32, (v.shape[0], 1), 0)
    np_rows = prev.shape[0]
    for i in range(k):
        out = jnp.where(rows == i, prev[np_rows - k + i:np_rows - k + i + 1, :], out)
    return out


def _shortconv_kernel(x_ref, win_ref, cw_ref, cb_ref, wout_ref, g_ref, b_ref, o_ref, carry_ref):
    @pl.when(pl.program_id(1) == 0)
    def _():
        carry_ref[...] = jnp.zeros(carry_ref.shape, F32)

    x = x_ref[0]
    xb = x.astype(MXU_DTYPE)
    d = D_MODEL
    u = _dot(xb, win_ref[:, 0:d])
    bg = _dot(xb, win_ref[:, d:2 * d])
    cg = _dot(xb, win_ref[:, 2 * d:3 * d])
    v = cg * u
    prev = carry_ref[...]
    z = cb_ref[...] + cw_ref[SC_WIDTH - 1:SC_WIDTH, :] * v
    for k in range(1, SC_WIDTH):
        z = z + cw_ref[SC_WIDTH - 1 - k:SC_WIDTH - k, :] * _shift_rows(v, prev, k)
    carry_ref[...] = v[v.shape[0] - SUBLANES:, :]
    m = _dot((bg * z).astype(MXU_DTYPE), wout_ref[...])
    o_ref[0] = _layer_norm_rows(DN_ALPHA * x + m, g_ref[...], b_ref[...])


def _shortconv_layer(x3, win, cw, cb, wout, g, b):
    bsz, s, d = x3.shape
    tm = ROW_TILE
    xspec = pl.BlockSpec((1, tm, d), lambda i, j: (i, j, 0))
    return pl.pallas_call(
        _shortconv_kernel,
        grid=(bsz, s // tm),
        in_specs=[xspec, _const_spec(win.shape), _const_spec(cw.shape), _const_spec(cb.shape),
                  _const_spec(wout.shape), _const_spec(g.shape), _const_spec(b.shape)],
        out_specs=xspec,
        out_shape=jax.ShapeDtypeStruct(x3.shape, F32),
        scratch_shapes=[pltpu.VMEM((SUBLANES, d), F32)],
        compiler_params=_cparams(("arbitrary", "arbitrary")),
        name="shortconv_layer",
    )(x3, win, cw, cb, wout, g, b)


def _rglru_kernel(x_ref, win_ref, cw_ref, cb_ref, wa_ref, ba_ref, wx_ref, bx_ref, lam_ref,
                  wout_ref, g_ref, b_ref, o_ref, carry_ref, h_ref, a_s, b_s):
    @pl.when(pl.program_id(1) == 0)
    def _():
        carry_ref[...] = jnp.zeros(carry_ref.shape, F32)
        h_ref[...] = jnp.zeros(h_ref.shape, F32)

    x = x_ref[0]
    tm = x.shape[0]
    xb = x.astype(MXU_DTYPE)
    w = LRU_WIDTH
    gate_br = _dot(xb, win_ref[:, 0:w])
    rec = _dot(xb, win_ref[:, w:2 * w])
    prev = carry_ref[...]
    u = cb_ref[...] + cw_ref[LRU_CONV - 1:LRU_CONV, :] * rec
    for k in range(1, LRU_CONV):
        u = u + cw_ref[LRU_CONV - 1 - k:LRU_CONV - k, :] * _shift_rows(rec, prev, k)
    carry_ref[...] = rec[tm - SUBLANES:, :]
    ub = u.astype(MXU_DTYPE)
    r = jax.nn.sigmoid(_dot(ub, wa_ref[...]) + ba_ref[...])
    ig = jax.nn.sigmoid(_dot(ub, wx_ref[...]) + bx_ref[...])
    log_a = LRU_C * r * jax.nn.log_sigmoid(lam_ref[...])
    a_s[...] = jnp.exp(log_a)
    th = jnp.tanh(log_a)
    b_s[...] = jnp.sqrt(-2.0 * th / (1.0 - th)) * (ig * u)

    rows = lax.broadcasted_iota(jnp.int32, (SUBLANES, 1), 0)

    def group(i, hc):
        r0 = pl.multiple_of(i * SUBLANES, SUBLANES)
        a = a_s[pl.ds(r0, SUBLANES), :]
        bv = b_s[pl.ds(r0, SUBLANES), :]
        for dlt in (1, 2, 4):
            a_sh = jnp.where(rows >= dlt, pltpu.roll(a, dlt, 0), 1.0)
            b_sh = jnp.where(rows >= dlt, pltpu.roll(bv, dlt, 0), 0.0)
            bv = a * b_sh + bv
            a = a * a_sh
        h = a * hc + bv
        b_s[pl.ds(r0, SUBLANES), :] = h
        return h[SUBLANES - 1:SUBLANES, :]

    h_ref[...] = lax.fori_loop(0, tm // SUBLANES, group, h_ref[...])
    y = jax.nn.gelu(gate_br) * b_s[...]
    m = _dot(y.astype(MXU_DTYPE), wout_ref[...])
    o_ref[0] = _layer_norm_rows(DN_ALPHA * x + m, g_ref[...], b_ref[...])


def _rglru_layer(x3, win, cw, cb, wa, ba, wx, bx, lam, wout, g, b):
    bsz, s, d = x3.shape
    tm = LRU_ROW_TILE
    xspec = pl.BlockSpec((1, tm, d), lambda i, j: (i, j, 0))
    consts = [win, cw, cb, wa, ba, wx, bx, lam, wout, g, b]
    return pl.pallas_call(
        _rglru_kernel,
        grid=(bsz, s // tm),
        in_specs=[xspec] + [_const_spec(c.shape) for c in consts],
        out_specs=xspec,
        out_shape=jax.ShapeDtypeStruct(x3.shape, F32),
        scratch_shapes=[pltpu.VMEM((SUBLANES, LRU_WIDTH), F32), pltpu.VMEM((1, LRU_WIDTH), F32),
                        pltpu.VMEM((tm, LRU_WIDTH), F32), pltpu.VMEM((tm, LRU_WIDTH), F32)],
        compiler_params=_cparams(("arbitrary", "arbitrary")),
        name="rglru_layer",
    )(x3, win, cw, cb, wa, ba, wx, bx, lam, wout, g, b)


def _moe_router_kernel(x_ref, rwt_ref, rb_ref, tri_ref, pos_ref, wts_ref, meta_ref):
    tt = x_ref.shape[0]
    logits = _dot_nt(rwt_ref[...], x_ref[...], precision=lax.Precision.HIGHEST)
    scores = jax.nn.sigmoid(logits)
    biased = scores + rb_ref[...]
    rows = [biased[e:e + 1, :] for e in range(N_EXPERTS)]

    gscore = []
    for gi in range(N_GROUPS):
        a, b, c, d = rows[gi * EXPERTS_PER_GROUP:(gi + 1) * EXPERTS_PER_GROUP]
        hi1, lo1 = jnp.maximum(a, b), jnp.minimum(a, b)
        hi2, lo2 = jnp.maximum(c, d), jnp.minimum(c, d)
        gscore.append(jnp.maximum(hi1, hi2) + jnp.maximum(jnp.minimum(hi1, hi2),
                                                          jnp.maximum(lo1, lo2)))
    best = gscore[0]
    top_g = jnp.zeros((1, tt), jnp.int32)
    for gi in range(1, N_GROUPS):
        better = gscore[gi] > best
        best = jnp.where(better, gscore[gi], best)
        top_g = jnp.where(better, gi, top_g)

    def argmax_in_group(exclude):
        bv = jnp.full((1, tt), -jnp.inf, F32)
        bi = jnp.full((1, tt), -1, jnp.int32)
        for e in range(N_EXPERTS):
            ok = top_g == (e // EXPERTS_PER_GROUP)
            if exclude is not None:
                ok = ok & (exclude != e)
            better = ok & (rows[e] > bv)
            bv = jnp.where(better, rows[e], bv)
            bi = jnp.where(better, e, bi)
        return bi

    e1 = argmax_in_group(None)
    e2 = argmax_in_group(e1)
    eidx = lax.broadcasted_iota(jnp.int32, (N_EXPERTS, 1), 0)
    oh1 = eidx == e1
    oh2 = eidx == e2
    s1 = jnp.sum(jnp.where(oh1, scores, 0.0), axis=0, keepdims=True)
    s2 = jnp.sum(jnp.where(oh2, scores, 0.0), axis=0, keepdims=True)
    wsum = s1 + s2
    wts_ref[0, 0:1, :] = s1 / wsum
    wts_ref[0, 1:2, :] = s2 / wsum

    oh = jnp.where(oh1 | oh2, 1.0, 0.0)
    csum = _dot(oh.astype(MXU_DTYPE), tri_ref[...])
    excl = csum - oh
    cnt = csum[:, tt - 1:tt]
    padded = jnp.ceil(cnt * (1.0 / MOE_ALIGN)) * MOE_ALIGN
    run = jnp.zeros((1, 1), F32)
    off_rows = []
    for e in range(N_EXPERTS):
        off_rows.append(run)
        run = run + padded[e:e + 1, :]
    off = jnp.concatenate(off_rows, axis=0)
    slot = off + excl
    pos_ref[0, 0:1, :] = jnp.sum(jnp.where(oh1, slot, 0.0), axis=0, keepdims=True).astype(jnp.int32)
    pos_ref[0, 1:2, :] = jnp.sum(jnp.where(oh2, slot, 0.0), axis=0, keepdims=True).astype(jnp.int32)
    lane = lax.broadcasted_iota(jnp.int32, (1, LANES), 1)
    meta = jnp.zeros((1, LANES), F32)
    for e in range(N_EXPERTS):
        meta = jnp.where(lane == e, off[e:e + 1, :], meta)
        meta = jnp.where(lane == N_EXPERTS + e, cnt[e:e + 1, :], meta)
    meta_ref[0] = meta.astype(jnp.int32)


def _moe_router(x2, rwt, rb, tri):
    t = x2.shape[0]
    tt = tri.shape[0]
    nt = t // tt
    return pl.pallas_call(
        _moe_router_kernel,
        grid=(nt,),
        in_specs=[pl.BlockSpec((tt, D_MODEL), lambda i: (i, 0)), _const_spec(rwt.shape),
                  _const_spec(rb.shape), _const_spec(tri.shape)],
        out_specs=[pl.BlockSpec((1, TOP_K, tt), lambda i: (i, 0, 0)),
                   pl.BlockSpec((1, TOP_K, tt), lambda i: (i, 0, 0)),
                   pl.BlockSpec((1, 1, LANES), lambda i: (i, 0, 0))],
        out_shape=[jax.ShapeDtypeStruct((nt, TOP_K, tt), jnp.int32),
                   jax.ShapeDtypeStruct((nt, TOP_K, tt), F32),
                   jax.ShapeDtypeStruct((nt, 1, LANES), jnp.int32)],
        compiler_params=_cparams(("arbitrary",)),
        name="moe_router",
    )(x2, rwt, rb, tri)


def _moe_ffn_kernel(pos_ref, wts_ref, meta_ref, x_ref, wg_ref, wu_ref, wd_ref, g_ref, b_ref,
                    o_ref, xs_ref):
    tt = x_ref.shape[0]
    ch = MOE_CHUNK
    e = pl.program_id(1)

    @pl.when(e == 0)
    def _dispatch():
        def zero_gap(ei, carry):
            start = meta_ref[0, 0, ei] + meta_ref[0, 0, N_EXPERTS + ei]
            stop = jnp.where(ei == N_EXPERTS - 1, start + ch + MOE_ALIGN,
                             meta_ref[0, 0, jnp.minimum(ei + 1, N_EXPERTS - 1)])

            def zrow(r, c):
                xs_ref[pl.ds(r, 1), :] = jnp.zeros((1, D_MODEL), F32)
                return c

            return lax.fori_loop(start, stop, zrow, carry)

        lax.fori_loop(0, N_EXPERTS, zero_gap, 0)

        def scatter(t, carry):
            row = x_ref[pl.ds(t, 1), :]
            xs_ref[pl.ds(pos_ref[0, 0, t], 1), :] = row
            xs_ref[pl.ds(pos_ref[0, 1, t], 1), :] = row
            return carry

        lax.fori_loop(0, tt, scatter, 0)

    off = meta_ref[0, 0, e]
    cnt = meta_ref[0, 0, N_EXPERTS + e]
    rid = lax.broadcasted_iota(jnp.int32, (ch, 1), 0)

    def chunk(c, carry):
        r0 = pl.multiple_of(off + c * ch, MOE_ALIGN)
        xs = xs_ref[pl.ds(r0, ch), :]
        xb = xs.astype(MXU_DTYPE)
        hid = jax.nn.silu(_dot(xb, wg_ref[0])) * _dot(xb, wu_ref[0])
        out = _dot(hid.astype(MXU_DTYPE), wd_ref[0])
        xs_ref[pl.ds(r0, ch), :] = jnp.where(rid + c * ch < cnt, out, xs)
        return carry

    lax.fori_loop(0, (cnt + ch - 1) // ch, chunk, 0)

    @pl.when(e == N_EXPERTS - 1)
    def _combine():
        def gather(t, carry):
            f = (wts_ref[0, 0, t] * xs_ref[pl.ds(pos_ref[0, 0, t], 1), :]
                 + wts_ref[0, 1, t] * xs_ref[pl.ds(pos_ref[0, 1, t], 1), :])
            o_ref[pl.ds(t, 1), :] = f
            return carry

        lax.fori_loop(0, tt, gather, 0)

        def norm(i, carry):
            r0 = pl.multiple_of(i * ch, ch)
            y = DN_ALPHA * x_ref[pl.ds(r0, ch), :] + o_ref[pl.ds(r0, ch), :]
            o_ref[pl.ds(r0, ch), :] = _layer_norm_rows(y, g_ref[...], b_ref[...])
            return carry

        lax.fori_loop(0, tt // ch, norm, 0)


def _moe_ffn(pos, wts, meta, x2, wg, wu, wd, g, b):
    t = x2.shape[0]
    tt = pos.shape[2]
    nt = t // tt
    smem = lambda shape: pl.BlockSpec(shape, lambda i, e: (i, 0, 0), memory_space=pltpu.SMEM)
    wspec = lambda w: pl.BlockSpec((1,) + w.shape[1:], lambda i, e: (e, 0, 0))
    xs_rows = TOP_K * tt + N_EXPERTS * MOE_ALIGN + MOE_CHUNK + MOE_ALIGN
    return pl.pallas_call(
        _moe_ffn_kernel,
        grid=(nt, N_EXPERTS),
        in_specs=[smem((1, TOP_K, tt)), smem((1, TOP_K, tt)), smem((1, 1, LANES)),
                  pl.BlockSpec((tt, D_MODEL), lambda i, e: (i, 0)),
                  wspec(wg), wspec(wu), wspec(wd),
                  pl.BlockSpec((1, D_MODEL), lambda i, e: (0, 0)),
                  pl.BlockSpec((1, D_MODEL), lambda i, e: (0, 0))],
        out_specs=pl.BlockSpec((tt, D_MODEL), lambda i, e: (i, 0)),
        out_shape=jax.ShapeDtypeStruct((t, D_MODEL), F32),
        scratch_shapes=[pltpu.VMEM((xs_rows, D_MODEL), F32)],
        compiler_params=_cparams(("arbitrary", "arbitrary")),
        name="moe_ffn",
    )(pos, wts, meta, x2, wg, wu, wd, g, b)


def _nsa_weight_layout(w_in, b_gate):
    G, R, Dh = NSA_GROUPS, NSA_HPG, HEAD_DIM
    d = np.arange(Dh)
    q_cols = np.concatenate([(g * R + r) * Dh + d for r in range(R) for g in range(G)])
    base = NSA_Q_COLS

    def kv(i):
        return base + i * NSA_KV_COLS + np.arange(NSA_KV_COLS)

    kc, vc, ks, vs, kw, vw = (kv(i) for i in range(6))
    gate_base = base + 6 * NSA_KV_COLS
    gate_src = np.array([(g * R + r) * 3 + br for br in range(3) for r in range(R) for g in range(G)])
    cols = np.concatenate([q_cols, ks, kw, kc, vc, vs, vw, gate_base + gate_src])
    w = jnp.take(w_in, jnp.asarray(cols), axis=1)
    pad = _NSA_PROJ_COLS - w.shape[1]
    w = jnp.pad(w, ((0, 0), (0, pad))).astype(MXU_DTYPE)
    bg = jnp.pad(jnp.take(b_gate, jnp.asarray(gate_src)), (0, LANES - gate_src.size))[None, :]
    return w, bg


def _nsa_out_layout(w_out):
    G, R, Dh = NSA_GROUPS, NSA_HPG, HEAD_DIM
    d = np.arange(Dh)
    rows = np.concatenate([(g * R + r) * Dh + d for r in range(R) for g in range(G)])
    return jnp.take(w_out, jnp.asarray(rows), axis=0).astype(MXU_DTYPE)


def _compress_weight_layout(pe, w1, w2):
    G, Dh, L = NSA_GROUPS, HEAD_DIM, CMP_BLOCK
    eye = jnp.eye(G, dtype=w1.dtype)
    w1r = w1.reshape(L, Dh, Dh)
    w1big = jnp.einsum("ldj,gh->lgdhj", w1r, eye).reshape(L * G * Dh, G * Dh)
    w2big = jnp.einsum("dj,gh->gdhj", w2, eye).reshape(G * Dh, G * Dh)
    pebig = jnp.broadcast_to(pe[:, None, :], (L, G, Dh)).reshape(1, L * G * Dh)
    return pebig, w1big.astype(MXU_DTYPE), w2big.astype(MXU_DTYPE)


def _block_diag(w):
    h, bi, bj = w.shape
    eye = jnp.eye(h, dtype=w.dtype)
    return jnp.einsum("hij,hk->hikj", w, eye).reshape(h * bi, h * bj).astype(MXU_DTYPE)


def _nsa_constants(s):
    nc = s // CMP_STRIDE
    n_cmp = (s - CMP_BLOCK) // CMP_STRIDE + 1
    n_sel = s // SEL_BLOCK
    cmp_start = np.arange(nc) * CMP_STRIDE
    cmp_end = cmp_start + CMP_BLOCK - 1
    sel_start = np.arange(n_sel) * SEL_BLOCK
    ov = ((cmp_start[:, None] <= sel_start[None, :] + SEL_BLOCK - 1)
          & (cmp_end[:, None] >= sel_start[None, :])
          & (np.arange(nc)[:, None] < n_cmp)).astype(np.float32)
    ov = np.pad(ov, ((0, 0), (0, LANES - n_sel)))
    et = (np.arange(s)[:, None] // SEL_BLOCK == np.arange(LANES)[None, :]).astype(np.float32)
    return jnp.asarray(ov), jnp.asarray(et, dtype=MXU_DTYPE)


def _nsa_layer(x2, bsz, s, tables, cmp_tables, w_in, b_gate, pos_k, w1_k, w2_k, pos_v, w1_v, w2_v,
               w_out, g, b):
    w_perm, bg_perm = _nsa_weight_layout(w_in, b_gate)
    q, ks, kw, kc, vc, vs, vw, gates = _nsa_proj(x2, w_perm, bg_perm, *tables)
    nc = s // CMP_STRIDE
    feat = CMP_STRIDE * NSA_KV_COLS
    pek, w1k, w2k = _compress_weight_layout(pos_k, w1_k, w2_k)
    pev, w1v, w2v = _compress_weight_layout(pos_v, w1_v, w2_v)
    kcmp, vcmp = _nsa_compress(kc.reshape(bsz, nc, feat), vc.reshape(bsz, nc, feat),
                               pek, pev, w1k, w1v, w2k, w2v, *cmp_tables)
    ov, et = _nsa_constants(s)
    r3 = lambda a: a.reshape(bsz, s, a.shape[-1])
    o = _nsa_attn(r3(q), r3(ks), r3(vs), r3(kw), r3(vw), kcmp, vcmp, r3(gates), ov, et)
    return _mm_res_ln(o.reshape(bsz * s, NSA_Q_COLS), _nsa_out_layout(w_out), x2, g, b)


def _moe_layer(x2, rwt, rb, tri, wg, wu, wd, g, b):
    pos, wts, meta = _moe_router(x2, rwt, rb, tri)
    return _moe_ffn(pos, wts, meta, x2, wg.astype(MXU_DTYPE), wu.astype(MXU_DTYPE),
                    wd.astype(MXU_DTYPE), g, b)


def kernel(x, positions, nsa_w_in, nsa_b_gate, nsa_cmp_pos_k, nsa_cmp_w1_k, nsa_cmp_w2_k, nsa_cmp_pos_v, nsa_cmp_w1_v, nsa_cmp_w2_v, nsa_w_out, sc_w_in, sc_conv_w, sc_conv_b, sc_w_out, lru_w_in, lru_conv_w, lru_conv_b, lru_wa, lru_ba, lru_wx, lru_bx, lru_lambda, lru_w_out, router_w, router_b, moe_w_gate, moe_w_up, moe_w_down, ln_g, ln_b):
    bsz, s, d = x.shape
    t = bsz * s
    x2 = x.reshape(t, d)
    row = lambda v: v.reshape(1, -1)

    tables = _rope_tables(positions.reshape(t))
    nc = s // CMP_STRIDE
    cmp_idx = np.minimum(np.arange(nc) * CMP_STRIDE + CMP_BLOCK - 1, s - 1)
    cmp_tables = tuple(tb.reshape(bsz, nc, LANES)
                       for tb in _rope_tables(positions[:, cmp_idx].reshape(bsz * nc)))

    tt = min(MOE_TOKENS, t)
    tri = jnp.asarray(np.triu(np.ones((tt, tt), np.float32)), dtype=MXU_DTYPE)
    rwt = router_w.T
    rb = router_b.reshape(N_EXPERTS, 1)

    for i in range(DEPTH):
        kind, j = i % N_MIXERS, i // N_MIXERS
        g0, b0 = row(ln_g[i, 0]), row(ln_b[i, 0])
        if kind == 0:
            x2 = _nsa_layer(x2, bsz, s, tables, cmp_tables, nsa_w_in[j], nsa_b_gate[j],
                            nsa_cmp_pos_k[j], nsa_cmp_w1_k[j], nsa_cmp_w2_k[j],
                            nsa_cmp_pos_v[j], nsa_cmp_w1_v[j], nsa_cmp_w2_v[j], nsa_w_out[j], g0, b0)
        elif kind == 1:
            x2 = _shortconv_layer(x2.reshape(bsz, s, d), sc_w_in[j].astype(MXU_DTYPE), sc_conv_w[j],
                                  row(sc_conv_b[j]), sc_w_out[j].astype(MXU_DTYPE), g0, b0).reshape(t, d)
        else:
            x2 = _rglru_layer(x2.reshape(bsz, s, d), lru_w_in[j].astype(MXU_DTYPE), lru_conv_w[j],
                              row(lru_conv_b[j]), _block_diag(lru_wa[j]), row(lru_ba[j]),
                              _block_diag(lru_wx[j]), row(lru_bx[j]), row(lru_lambda[j]),
                              lru_w_out[j].astype(MXU_DTYPE), g0, b0).reshape(t, d)
        x2 = _moe_layer(x2, rwt, rb, tri, moe_w_gate[i], moe_w_up[i], moe_w_down[i],
                        row(ln_g[i, 1]), row(ln_b[i, 1]))
    return x2.reshape(bsz, s, d)
```

```python
import functools

import numpy as np
import jax
import jax.numpy as jnp
from jax import lax
from jax.experimental import pallas as pl
from jax.experimental.pallas import tpu as pltpu

F32 = jnp.float32
BF16 = jnp.bfloat16
MXU_DTYPE = BF16

D_MODEL = 1024
DEPTH = 4
N_MIXERS = 3
NSA_HEADS = 16
NSA_GROUPS = 4
NSA_HPG = NSA_HEADS // NSA_GROUPS
HEAD_DIM = D_MODEL // NSA_HEADS
CMP_BLOCK = 32
CMP_STRIDE = 16
SEL_BLOCK = 64
SEL_TOP = 16
WINDOW = 512
ROPE_THETA = 10000.0
FORCE_SCORE = 1e6
NSA_Q_COLS = NSA_HEADS * HEAD_DIM
NSA_KV_COLS = NSA_GROUPS * HEAD_DIM
SC_WIDTH = 3
LRU_WIDTH = 1280
LRU_BLOCKS = 16
LRU_BW = LRU_WIDTH // LRU_BLOCKS
LRU_CONV = 4
LRU_C = 8.0
N_EXPERTS = 16
N_GROUPS = 4
EXPERTS_PER_GROUP = N_EXPERTS // N_GROUPS
TOP_K = 2
D_FF = 1024
DN_ALPHA = (2.0 * DEPTH) ** 0.25
LN_EPS = 1e-5
NEG_INF = -1e30

LANES = 128
SUBLANES = 8
VMEM_LIMIT_BYTES = 56 * 1024 * 1024
ROW_TILE = 512
LRU_ROW_TILE = 256
ATT_TQ = 256
ATT_TK = 256
MOE_TOKENS = 2048
MOE_ROW_UNROLL = 4
MOE_CHUNK = 128
MOE_ALIGN = SUBLANES


def _cparams(sem):
    return pltpu.CompilerParams(dimension_semantics=sem, vmem_limit_bytes=VMEM_LIMIT_BYTES)


def _dot(a, b):
    return jnp.dot(a, b, preferred_element_type=F32)


def _dot_nt(a, b, precision=None):
    return lax.dot_general(a, b, (((1,), (1,)), ((), ())), preferred_element_type=F32,
                           precision=precision)


def _layer_norm_rows(y, g, b):
    mu = jnp.mean(y, axis=-1, keepdims=True)
    d = y - mu
    var = jnp.mean(d * d, axis=-1, keepdims=True)
    return d * lax.rsqrt(var + LN_EPS) * g + b


def _const_spec(shape):
    nd = len(shape)
    return pl.BlockSpec(shape, lambda *_: (0,) * nd)


def _rope_tables_kernel(pos_ref, inv_ref, cos_ref, sina_ref, sinb_ref):
    ang = pos_ref[...] * inv_ref[...]
    c = jnp.cos(ang)
    s = jnp.sin(ang)
    lane = lax.broadcasted_iota(jnp.int32, ang.shape, 1)
    first = (lane % HEAD_DIM) < (HEAD_DIM // 2)
    cos_ref[...] = c
    sina_ref[...] = jnp.where(first, -s, 0.0)
    sinb_ref[...] = jnp.where(first, 0.0, s)


def _rope_tables(pos_flat):
    n = pos_flat.shape[0]
    tm = min(n, 1024)
    half = HEAD_DIM // 2
    inv = ROPE_THETA ** (-jnp.arange(half, dtype=F32) / half)
    inv_lane = jnp.tile(inv, LANES // half)[None, :]
    posb = jnp.broadcast_to(pos_flat.astype(F32)[:, None], (n, LANES))
    spec = pl.BlockSpec((tm, LANES), lambda i: (i, 0))
    out = jax.ShapeDtypeStruct((n, LANES), F32)
    return pl.pallas_call(
        _rope_tables_kernel,
        grid=(n // tm,),
        in_specs=[spec, _const_spec((1, LANES))],
        out_specs=[spec, spec, spec],
        out_shape=[out, out, out],
        compiler_params=_cparams(("arbitrary",)),
        name="rope_tables",
    )(posb, inv_lane)


def _rope128(v, cos, sina, sinb):
    half = HEAD_DIM // 2
    return (v * cos + pltpu.roll(v, LANES - half, 1) * sina + pltpu.roll(v, half, 1) * sinb)


_NSA_SEG = {
    "q": 0, "ks": 1024, "kw": 1280, "kc": 1536, "vc": 1792, "vs": 2048, "vw": 2304, "gate": 2560,
}
_NSA_GATE_COLS = NSA_GROUPS * LANES
_NSA_PROJ_COLS = 2560 + _NSA_GATE_COLS


def _nsa_proj_kernel(x_ref, w_ref, bg_ref, cos_ref, sina_ref, sinb_ref,
                     q_ref, ks_ref, kw_ref, kc_ref, vc_ref, vs_ref, vw_ref, g_ref):
    xb = x_ref[...].astype(MXU_DTYPE)
    cos = cos_ref[...]
    sina = sina_ref[...]
    sinb = sinb_ref[...]
    scale = HEAD_DIM ** -0.5

    def seg(name, width=256, c=0):
        o = _NSA_SEG[name] + c * 256
        return _dot(xb, w_ref[:, o:o + width])

    for c in range(NSA_Q_COLS // 256):
        acc = seg("q", c=c)
        for h in range(2):
            r = _rope128(acc[:, h * LANES:(h + 1) * LANES], cos, sina, sinb) * scale
            q_ref[:, c * 256 + h * LANES:c * 256 + (h + 1) * LANES] = r.astype(q_ref.dtype)
    for name, ref in (("ks", ks_ref), ("kw", kw_ref)):
        acc = seg(name)
        for h in range(2):
            r = _rope128(acc[:, h * LANES:(h + 1) * LANES], cos, sina, sinb)
            ref[:, h * LANES:(h + 1) * LANES] = r.astype(ref.dtype)
    kc_ref[...] = seg("kc")
    vc_ref[...] = seg("vc")
    vs_ref[...] = seg("vs").astype(vs_ref.dtype)
    vw_ref[...] = seg("vw").astype(vw_ref.dtype)
    g_ref[...] = jax.nn.sigmoid(seg("gate", width=_NSA_GATE_COLS) + bg_ref[...])


def _nsa_proj(x2, w_perm, bg_perm, cos_t, sina_t, sinb_t):
    t = x2.shape[0]
    tm = ROW_TILE
    row = lambda w: pl.BlockSpec((tm, w), lambda i: (i, 0))
    outs = [
        jax.ShapeDtypeStruct((t, NSA_Q_COLS), MXU_DTYPE),
        jax.ShapeDtypeStruct((t, NSA_KV_COLS), MXU_DTYPE),
        jax.ShapeDtypeStruct((t, NSA_KV_COLS), MXU_DTYPE),
        jax.ShapeDtypeStruct((t, NSA_KV_COLS), F32),
        jax.ShapeDtypeStruct((t, NSA_KV_COLS), F32),
        jax.ShapeDtypeStruct((t, NSA_KV_COLS), MXU_DTYPE),
        jax.ShapeDtypeStruct((t, NSA_KV_COLS), MXU_DTYPE),
        jax.ShapeDtypeStruct((t, _NSA_GATE_COLS), F32),
    ]
    return pl.pallas_call(
        _nsa_proj_kernel,
        grid=(t // tm,),
        in_specs=[row(D_MODEL), _const_spec((D_MODEL, _NSA_PROJ_COLS)),
                  _const_spec((1, _NSA_GATE_COLS)), row(LANES), row(LANES), row(LANES)],
        out_specs=[row(NSA_Q_COLS)] + [row(NSA_KV_COLS)] * 6 + [row(_NSA_GATE_COLS)],
        out_shape=outs,
        compiler_params=_cparams(("arbitrary",)),
        name="nsa_proj",
    )(x2, w_perm, bg_perm, cos_t, sina_t, sinb_t)


def _nsa_compress_kernel(kc_ref, vc_ref, pek_ref, pev_ref, w1k_ref, w1v_ref, w2k_ref, w2v_ref,
                         cos_ref, sina_ref, sinb_ref, ko_ref, vo_ref):
    nc = kc_ref.shape[1]
    half_feat = kc_ref.shape[2]
    rows = lax.broadcasted_iota(jnp.int32, (nc, 1), 0)
    valid = rows < nc - 1

    def compress(tok, pe_ref, w1_ref, w2_ref):
        a = (tok + pe_ref[:, :half_feat]).astype(MXU_DTYPE)
        b = (tok + pe_ref[:, half_feat:]).astype(MXU_DTYPE)
        pa = _dot(a, w1_ref[:half_feat, :])
        pb = _dot(b, w1_ref[half_feat:, :])
        pre = pa + pltpu.roll(pb, nc - 1, 0)
        h = jax.nn.gelu(pre)
        return _dot(h.astype(MXU_DTYPE), w2_ref[...])

    kcmp = compress(kc_ref[0], pek_ref, w1k_ref, w2k_ref)
    vcmp = compress(vc_ref[0], pev_ref, w1v_ref, w2v_ref)
    cos = cos_ref[0]
    sina = sina_ref[0]
    sinb = sinb_ref[0]
    for h in range(NSA_KV_COLS // LANES):
        r = _rope128(kcmp[:, h * LANES:(h + 1) * LANES], cos, sina, sinb)
        ko_ref[0, :, h * LANES:(h + 1) * LANES] = jnp.where(valid, r, 0.0).astype(ko_ref.dtype)
    vo_ref[0] = jnp.where(valid, vcmp, 0.0).astype(vo_ref.dtype)


def _nsa_compress(kc3, vc3, pek, pev, w1k, w1v, w2k, w2v, cos_c, sina_c, sinb_c):
    b, nc, feat = kc3.shape
    tok = pl.BlockSpec((1, nc, feat), lambda i: (i, 0, 0))
    tab = pl.BlockSpec((1, nc, LANES), lambda i: (i, 0, 0))
    out = pl.BlockSpec((1, nc, NSA_KV_COLS), lambda i: (i, 0, 0))
    oshape = jax.ShapeDtypeStruct((b, nc, NSA_KV_COLS), MXU_DTYPE)
    return pl.pallas_call(
        _nsa_compress_kernel,
        grid=(b,),
        in_specs=[tok, tok, _const_spec(pek.shape), _const_spec(pev.shape),
                  _const_spec(w1k.shape), _const_spec(w1v.shape),
                  _const_spec(w2k.shape), _const_spec(w2v.shape), tab, tab, tab],
        out_specs=[out, out],
        out_shape=[oshape, oshape],
        compiler_params=_cparams(("arbitrary",)),
        name="nsa_compress",
    )(kc3, vc3, pek, pev, w1k, w1v, w2k, w2v, cos_c, sina_c, sinb_c)


def _nsa_attn_kernel(q_ref, ks_ref, vs_ref, kw_ref, vw_ref, kc_ref, vc_ref, g_ref, ov_ref, et_ref,
                     o_ref, m_ref, acc_ref, out_ref, *, n_sel, k_sel):
    tq = q_ref.shape[1]
    tk = ATT_TK
    nc = kc_ref.shape[1]
    R = NSA_HPG
    qi = pl.program_id(1)
    g = pl.program_id(2)
    t0 = qi * tq
    trow = t0 + lax.broadcasted_iota(jnp.int32, (tq, 1), 0)
    gmask = lax.broadcasted_iota(jnp.int32, (1, NSA_KV_COLS), 1) // HEAD_DIM == g
    gates = g_ref[0]
    qs = jnp.concatenate(
        [jnp.where(gmask, q_ref[0, :, r * NSA_KV_COLS:(r + 1) * NSA_KV_COLS], 0)
         for r in range(R)], axis=0)
    rows = lambda a, r: a[r * tq:(r + 1) * tq]

    def flash(k_ref, v_ref, j_lo, j_hi, bias_fn):
        m_ref[...] = jnp.full(m_ref.shape, NEG_INF, F32)
        acc_ref[...] = jnp.zeros(acc_ref.shape, F32)

        def body(j, carry):
            k0 = pl.multiple_of(j * tk, tk)
            kt = k_ref[0, pl.ds(k0, tk), :]
            vt = jnp.where(gmask, v_ref[0, pl.ds(k0, tk), :], 1)
            kpos = k0 + lax.broadcasted_iota(jnp.int32, (1, tk), 1)
            bias = bias_fn(k0, kpos)
            s_all = _dot_nt(qs, kt)
            ps, alphas = [], []
            for r in range(R):
                s = rows(s_all, r) + bias
                m_old = m_ref[r]
                m_new = jnp.maximum(m_old, jnp.max(s, axis=-1, keepdims=True))
                m_ref[r] = m_new
                alphas.append(jnp.exp(m_old - m_new))
                m_wide = jnp.concatenate([m_new] * (tk // LANES), axis=1)
                ps.append(jnp.exp(s - m_wide).astype(MXU_DTYPE))
            pv = _dot(jnp.concatenate(ps, axis=0), vt)
            for r in range(R):
                a_wide = jnp.concatenate([alphas[r]] * (NSA_KV_COLS // LANES), axis=1)
                acc_ref[r] = a_wide * acc_ref[r] + rows(pv, r)
            return carry

        lax.fori_loop(j_lo, j_hi, body, 0)
        outs = []
        for r in range(R):
            acc = acc_ref[r]
            outs.append(acc / pltpu.roll(acc, HEAD_DIM, 1))
        return outs

    n_idx = lax.broadcasted_iota(jnp.int32, (1, nc), 1)
    cmp_ok = (n_idx * CMP_STRIDE + CMP_BLOCK - 1) <= trow
    s_all = _dot_nt(qs, kc_ref[0])
    psum = jnp.zeros((tq, nc), F32)
    ps = []
    for r in range(R):
        s = jnp.where(cmp_ok, rows(s_all, r), NEG_INF)
        e = jnp.where(cmp_ok, jnp.exp(s - jnp.max(s, axis=-1, keepdims=True)), 0.0)
        p = e / jnp.maximum(jnp.sum(e, axis=-1, keepdims=True), 1e-30)
        psum = psum + p
        ps.append(p.astype(MXU_DTYPE))
    o_cmp = _dot(jnp.concatenate(ps, axis=0), vc_ref[0])

    imp = jnp.dot(psum, ov_ref[...], preferred_element_type=F32,
                  precision=lax.Precision.HIGHEST)
    blk = lax.broadcasted_iota(jnp.int32, (1, LANES), 1)
    rel = trow // SEL_BLOCK - blk
    imp = jnp.where(rel >= 0, imp, -1.0)
    forced = (blk == 0) | (rel == 0) | (rel == 1)
    imp = jnp.where(forced, FORCE_SCORE, imp)
    imp_t = jnp.transpose(imp)[:n_sel, :]
    midx = lax.broadcasted_iota(jnp.int32, (n_sel, 1), 0)
    cnt = jnp.zeros((n_sel, tq), F32)
    for mp in range(n_sel):
        row = imp_t[mp:mp + 1, :]
        beats = (row > imp_t) | ((row == imp_t) & (midx > mp))
        cnt = cnt + jnp.where(beats, 1.0, 0.0)
    sel_t = jnp.where(cnt < k_sel, 1.0, 0.0)
    if n_sel < LANES:
        sel_t = jnp.concatenate([sel_t, jnp.zeros((LANES - n_sel, tq), F32)], axis=0)
    sel = jnp.transpose(sel_t).astype(MXU_DTYPE)

    def sel_bias(k0, kpos):
        selm = _dot_nt(sel, et_ref[pl.ds(k0, tk), :])
        return jnp.where((selm > 0.5) & (kpos <= trow), 0.0, NEG_INF)

    o_slc = flash(ks_ref, vs_ref, 0, (t0 + tq) // tk, sel_bias)

    def win_bias(k0, kpos):
        d = trow - kpos
        return jnp.where((d >= 0) & (d < WINDOW), 0.0, NEG_INF)

    o_win = flash(kw_ref, vw_ref, jnp.maximum(t0 - WINDOW, 0) // tk, (t0 + tq) // tk, win_bias)

    @pl.when(g == 0)
    def _():
        out_ref[...] = jnp.zeros(out_ref.shape, F32)

    for r in range(R):
        o_r = (gates[:, r:r + 1] * rows(o_cmp, r) + gates[:, R + r:R + r + 1] * o_slc[r]
               + gates[:, 2 * R + r:2 * R + r + 1] * o_win[r])
        out_ref[r] = jnp.where(gmask, o_r, out_ref[r])

    @pl.when(g == NSA_GROUPS - 1)
    def _():
        for r in range(R):
            o_ref[0, :, r * NSA_KV_COLS:(r + 1) * NSA_KV_COLS] = out_ref[r].astype(o_ref.dtype)


def _nsa_attn(q, ks, vs, kw, vw, kcmp, vcmp, gates, ov, et):
    b, s, _ = q.shape
    nc = kcmp.shape[1]
    tq = ATT_TQ
    n_sel = s // SEL_BLOCK
    k_sel = min(SEL_TOP, n_sel)
    qspec = pl.BlockSpec((1, tq, NSA_Q_COLS), lambda i, j, g: (i, j, 0))
    kv = pl.BlockSpec((1, s, NSA_KV_COLS), lambda i, j, g: (i, 0, 0))
    cmp_spec = pl.BlockSpec((1, nc, NSA_KV_COLS), lambda i, j, g: (i, 0, 0))
    gspec = pl.BlockSpec((1, tq, LANES), lambda i, j, g: (i, j, g))
    return pl.pallas_call(
        functools.partial(_nsa_attn_kernel, n_sel=n_sel, k_sel=k_sel),
        grid=(b, s // tq, NSA_GROUPS),
        in_specs=[qspec, kv, kv, kv, kv, cmp_spec, cmp_spec, gspec,
                  _const_spec(ov.shape), _const_spec(et.shape)],
        out_specs=qspec,
        out_shape=jax.ShapeDtypeStruct((b, s, NSA_Q_COLS), MXU_DTYPE),
        scratch_shapes=[pltpu.VMEM((NSA_HPG, tq, LANES), F32),
                        pltpu.VMEM((NSA_HPG, tq, NSA_KV_COLS), F32),
                        pltpu.VMEM((NSA_HPG, tq, NSA_KV_COLS), F32)],
        compiler_params=_cparams(("arbitrary", "arbitrary", "arbitrary")),
        name="nsa_attn",
    )(q, ks, vs, kw, vw, kcmp, vcmp, gates, ov, et)


def _mm_res_ln_kernel(a_ref, w_ref, x_ref, g_ref, b_ref, o_ref):
    m = _dot(a_ref[...].astype(MXU_DTYPE), w_ref[...])
    o_ref[...] = _layer_norm_rows(DN_ALPHA * x_ref[...] + m, g_ref[...], b_ref[...])


def _mm_res_ln(a, w, x2, g, b):
    t, k = a.shape
    tm = ROW_TILE
    return pl.pallas_call(
        _mm_res_ln_kernel,
        grid=(t // tm,),
        in_specs=[pl.BlockSpec((tm, k), lambda i: (i, 0)), _const_spec(w.shape),
                  pl.BlockSpec((tm, D_MODEL), lambda i: (i, 0)),
                  _const_spec((1, D_MODEL)), _const_spec((1, D_MODEL))],
        out_specs=pl.BlockSpec((tm, D_MODEL), lambda i: (i, 0)),
        out_shape=jax.ShapeDtypeStruct((t, D_MODEL), F32),
        compiler_params=_cparams(("arbitrary",)),
        name="mm_res_ln",
    )(a, w, x2, g, b)


def _shift_rows(v, prev, k):
    out = pltpu.roll(v, k, 0)
    rows = lax.broadcasted_iota(jnp.int32, (v.shape[0], 1), 0)
    np_rows = prev.shape[0]
    for i in range(k):
        out = jnp.where(rows == i, prev[np_rows - k + i:np_rows - k + i + 1, :], out)
    return out


def _shortconv_kernel(x_ref, win_ref, cw_ref, cb_ref, wout_ref, g_ref, b_ref, o_ref, carry_ref):
    @pl.when(pl.program_id(1) == 0)
    def _():
        carry_ref[...] = jnp.zeros(carry_ref.shape, F32)

    x = x_ref[0]
    xb = x.astype(MXU_DTYPE)
    d = D_MODEL
    u = _dot(xb, win_ref[:, 0:d])
    bg = _dot(xb, win_ref[:, d:2 * d])
    cg = _dot(xb, win_ref[:, 2 * d:3 * d])
    v = cg * u
    prev = carry_ref[...]
    z = cb_ref[...] + cw_ref[SC_WIDTH - 1:SC_WIDTH, :] * v
    for k in range(1, SC_WIDTH):
        z = z + cw_ref[SC_WIDTH - 1 - k:SC_WIDTH - k, :] * _shift_rows(v, prev, k)
    carry_ref[...] = v[v.shape[0] - SUBLANES:, :]
    m = _dot((bg * z).astype(MXU_DTYPE), wout_ref[...])
    o_ref[0] = _layer_norm_rows(DN_ALPHA * x + m, g_ref[...], b_ref[...])


def _shortconv_layer(x3, win, cw, cb, wout, g, b):
    bsz, s, d = x3.shape
    tm = ROW_TILE
    xspec = pl.BlockSpec((1, tm, d), lambda i, j: (i, j, 0))
    return pl.pallas_call(
        _shortconv_kernel,
        grid=(bsz, s // tm),
        in_specs=[xspec, _const_spec(win.shape), _const_spec(cw.shape), _const_spec(cb.shape),
                  _const_spec(wout.shape), _const_spec(g.shape), _const_spec(b.shape)],
        out_specs=xspec,
        out_shape=jax.ShapeDtypeStruct(x3.shape, F32),
        scratch_shapes=[pltpu.VMEM((SUBLANES, d), F32)],
        compiler_params=_cparams(("arbitrary", "arbitrary")),
        name="shortconv_layer",
    )(x3, win, cw, cb, wout, g, b)


def _rglru_kernel(x_ref, win_ref, cw_ref, cb_ref, wa_ref, ba_ref, wx_ref, bx_ref, lam_ref,
                  wout_ref, g_ref, b_ref, o_ref, carry_ref, h_ref, a_s, b_s):
    @pl.when(pl.program_id(1) == 0)
    def _():
        carry_ref[...] = jnp.zeros(carry_ref.shape, F32)
        h_ref[...] = jnp.zeros(h_ref.shape, F32)

    x = x_ref[0]
    tm = x.shape[0]
    xb = x.astype(MXU_DTYPE)
    w = LRU_WIDTH
    gate_br = _dot(xb, win_ref[:, 0:w])
    rec = _dot(xb, win_ref[:, w:2 * w])
    prev = carry_ref[...]
    u = cb_ref[...] + cw_ref[LRU_CONV - 1:LRU_CONV, :] * rec
    for k in range(1, LRU_CONV):
        u = u + cw_ref[LRU_CONV - 1 - k:LRU_CONV - k, :] * _shift_rows(rec, prev, k)
    carry_ref[...] = rec[tm - SUBLANES:, :]
    ub = u.astype(MXU_DTYPE)
    r = jax.nn.sigmoid(_dot(ub, wa_ref[...]) + ba_ref[...])
    ig = jax.nn.sigmoid(_dot(ub, wx_ref[...]) + bx_ref[...])
    log_a = LRU_C * r * jax.nn.log_sigmoid(lam_ref[...])
    a_s[...] = jnp.exp(log_a)
    th = jnp.tanh(log_a)
    b_s[...] = jnp.sqrt(-2.0 * th / (1.0 - th)) * (ig * u)

    rows = lax.broadcasted_iota(jnp.int32, (SUBLANES, 1), 0)

    def group(i, hc):
        r0 = pl.multiple_of(i * SUBLANES, SUBLANES)
        a = a_s[pl.ds(r0, SUBLANES), :]
        bv = b_s[pl.ds(r0, SUBLANES), :]
        for dlt in (1, 2, 4):
            a_sh = jnp.where(rows >= dlt, pltpu.roll(a, dlt, 0), 1.0)
            b_sh = jnp.where(rows >= dlt, pltpu.roll(bv, dlt, 0), 0.0)
            bv = a * b_sh + bv
            a = a * a_sh
        h = a * hc + bv
        b_s[pl.ds(r0, SUBLANES), :] = h
        return h[SUBLANES - 1:SUBLANES, :]

    h_ref[...] = lax.fori_loop(0, tm // SUBLANES, group, h_ref[...])
    y = jax.nn.gelu(gate_br) * b_s[...]
    m = _dot(y.astype(MXU_DTYPE), wout_ref[...])
    o_ref[0] = _layer_norm_rows(DN_ALPHA * x + m, g_ref[...], b_ref[...])


def _rglru_layer(x3, win, cw, cb, wa, ba, wx, bx, lam, wout, g, b):
    bsz, s, d = x3.shape
    tm = LRU_ROW_TILE
    xspec = pl.BlockSpec((1, tm, d), lambda i, j: (i, j, 0))
    consts = [win, cw, cb, wa, ba, wx, bx, lam, wout, g, b]
    return pl.pallas_call(
        _rglru_kernel,
        grid=(bsz, s // tm),
        in_specs=[xspec] + [_const_spec(c.shape) for c in consts],
        out_specs=xspec,
        out_shape=jax.ShapeDtypeStruct(x3.shape, F32),
        scratch_shapes=[pltpu.VMEM((SUBLANES, LRU_WIDTH), F32), pltpu.VMEM((1, LRU_WIDTH), F32),
                        pltpu.VMEM((tm, LRU_WIDTH), F32), pltpu.VMEM((tm, LRU_WIDTH), F32)],
        compiler_params=_cparams(("arbitrary", "arbitrary")),
        name="rglru_layer",
    )(x3, win, cw, cb, wa, ba, wx, bx, lam, wout, g, b)


def _moe_router_kernel(x_ref, rwt_ref, rb_ref, tri_ref, pos_ref, wts_ref, meta_ref):
    tt = x_ref.shape[0]
    logits = _dot_nt(rwt_ref[...], x_ref[...], precision=lax.Precision.HIGHEST)
    scores = jax.nn.sigmoid(logits)
    biased = scores + rb_ref[...]
    rows = [biased[e:e + 1, :] for e in range(N_EXPERTS)]

    gscore = []
    for gi in range(N_GROUPS):
        a, b, c, d = rows[gi * EXPERTS_PER_GROUP:(gi + 1) * EXPERTS_PER_GROUP]
        hi1, lo1 = jnp.maximum(a, b), jnp.minimum(a, b)
        hi2, lo2 = jnp.maximum(c, d), jnp.minimum(c, d)
        gscore.append(jnp.maximum(hi1, hi2) + jnp.maximum(jnp.minimum(hi1, hi2),
                                                          jnp.maximum(lo1, lo2)))
    best = gscore[0]
    top_g = jnp.zeros((1, tt), jnp.int32)
    for gi in range(1, N_GROUPS):
        better = gscore[gi] > best
        best = jnp.where(better, gscore[gi], best)
        top_g = jnp.where(better, gi, top_g)

    def argmax_in_group(exclude):
        bv = jnp.full((1, tt), -jnp.inf, F32)
        bi = jnp.full((1, tt), -1, jnp.int32)
        for e in range(N_EXPERTS):
            ok = top_g == (e // EXPERTS_PER_GROUP)
            if exclude is not None:
                ok = ok & (exclude != e)
            better = ok & (rows[e] > bv)
            bv = jnp.where(better, rows[e], bv)
            bi = jnp.where(better, e, bi)
        return bi

    e1 = argmax_in_group(None)
    e2 = argmax_in_group(e1)
    eidx = lax.broadcasted_iota(jnp.int32, (N_EXPERTS, 1), 0)
    oh1 = eidx == e1
    oh2 = eidx == e2
    s1 = jnp.sum(jnp.where(oh1, scores, 0.0), axis=0, keepdims=True)
    s2 = jnp.sum(jnp.where(oh2, scores, 0.0), axis=0, keepdims=True)
    wsum = s1 + s2
    wts_ref[0, 0:1, :] = s1 / wsum
    wts_ref[0, 1:2, :] = s2 / wsum

    oh = jnp.where(oh1 | oh2, 1.0, 0.0)
    csum = _dot(oh.astype(MXU_DTYPE), tri_ref[...])
    excl = csum - oh
    cnt = csum[:, tt - 1:tt]
    padded = jnp.ceil(cnt * (1.0 / MOE_ALIGN)) * MOE_ALIGN
    run = jnp.zeros((1, 1), F32)
    off_rows = []
    for e in range(N_EXPERTS):
        off_rows.append(run)
        run = run + padded[e:e + 1, :]
    off = jnp.concatenate(off_rows, axis=0)
    slot = off + excl
    pos_ref[0, 0:1, :] = jnp.sum(jnp.where(oh1, slot, 0.0), axis=0, keepdims=True).astype(jnp.int32)
    pos_ref[0, 1:2, :] = jnp.sum(jnp.where(oh2, slot, 0.0), axis=0, keepdims=True).astype(jnp.int32)
    lane = lax.broadcasted_iota(jnp.int32, (1, LANES), 1)
    meta = jnp.zeros((1, LANES), F32)
    for e in range(N_EXPERTS):
        meta = jnp.where(lane == e, off[e:e + 1, :], meta)
        meta = jnp.where(lane == N_EXPERTS + e, cnt[e:e + 1, :], meta)
    meta_ref[0] = meta.astype(jnp.int32)


def _moe_router(x2, rwt, rb, tri):
    t = x2.shape[0]
    tt = tri.shape[0]
    nt = t // tt
    return pl.pallas_call(
        _moe_router_kernel,
        grid=(nt,),
        in_specs=[pl.BlockSpec((tt, D_MODEL), lambda i: (i, 0)), _const_spec(rwt.shape),
                  _const_spec(rb.shape), _const_spec(tri.shape)],
        out_specs=[pl.BlockSpec((1, TOP_K, tt), lambda i: (i, 0, 0)),
                   pl.BlockSpec((1, TOP_K, tt), lambda i: (i, 0, 0)),
                   pl.BlockSpec((1, 1, LANES), lambda i: (i, 0, 0))],
        out_shape=[jax.ShapeDtypeStruct((nt, TOP_K, tt), jnp.int32),
                   jax.ShapeDtypeStruct((nt, TOP_K, tt), F32),
                   jax.ShapeDtypeStruct((nt, 1, LANES), jnp.int32)],
        compiler_params=_cparams(("arbitrary",)),
        name="moe_router",
    )(x2, rwt, rb, tri)


def _moe_ffn_kernel(pos_ref, wts_ref, meta_ref, x_ref, wg_ref, wu_ref, wd_ref, g_ref, b_ref,
                    o_ref, xs_ref):
    tt = x_ref.shape[0]
    ch = MOE_CHUNK
    e = pl.program_id(1)

    @pl.when(e == 0)
    def _dispatch():
        def zero_gap(ei, carry):
            start = meta_ref[0, 0, ei] + meta_ref[0, 0, N_EXPERTS + ei]
            stop = jnp.where(ei == N_EXPERTS - 1, start + ch + MOE_ALIGN,
                             meta_ref[0, 0, jnp.minimum(ei + 1, N_EXPERTS - 1)])

            def zrow(r, c):
                xs_ref[pl.ds(r, 1), :] = jnp.zeros((1, D_MODEL), F32)
                return c

            return lax.fori_loop(start, stop, zrow, carry)

        lax.fori_loop(0, N_EXPERTS, zero_gap, 0)

        def scatter(t, carry):
            row = x_ref[pl.ds(t, 1), :]
            xs_ref[pl.ds(pos_ref[0, 0, t], 1), :] = row
            xs_ref[pl.ds(pos_ref[0, 1, t], 1), :] = row
            return carry

        lax.fori_loop(0, tt, scatter, 0, unroll=MOE_ROW_UNROLL)

    off = meta_ref[0, 0, e]
    cnt = meta_ref[0, 0, N_EXPERTS + e]
    rid = lax.broadcasted_iota(jnp.int32, (ch, 1), 0)

    def chunk(c, carry):
        r0 = pl.multiple_of(off + c * ch, MOE_ALIGN)
        xs = xs_ref[pl.ds(r0, ch), :]
        xb = xs.astype(MXU_DTYPE)
        hid = jax.nn.silu(_dot(xb, wg_ref[0])) * _dot(xb, wu_ref[0])
        out = _dot(hid.astype(MXU_DTYPE), wd_ref[0])
        xs_ref[pl.ds(r0, ch), :] = jnp.where(rid + c * ch < cnt, out, xs)
        return carry

    lax.fori_loop(0, (cnt + ch - 1) // ch, chunk, 0)

    @pl.when(e == N_EXPERTS - 1)
    def _combine():
        def gather(t, carry):
            f = (wts_ref[0, 0, t] * xs_ref[pl.ds(pos_ref[0, 0, t], 1), :]
                 + wts_ref[0, 1, t] * xs_ref[pl.ds(pos_ref[0, 1, t], 1), :])
            o_ref[pl.ds(t, 1), :] = f
            return carry

        lax.fori_loop(0, tt, gather, 0, unroll=MOE_ROW_UNROLL)

        def norm(i, carry):
            r0 = pl.multiple_of(i * ch, ch)
            y = DN_ALPHA * x_ref[pl.ds(r0, ch), :] + o_ref[pl.ds(r0, ch), :]
            o_ref[pl.ds(r0, ch), :] = _layer_norm_rows(y, g_ref[...], b_ref[...])
            return carry

        lax.fori_loop(0, tt // ch, norm, 0)


def _moe_ffn(pos, wts, meta, x2, wg, wu, wd, g, b):
    t = x2.shape[0]
    tt = pos.shape[2]
    nt = t // tt
    smem = lambda shape: pl.BlockSpec(shape, lambda i, e: (i, 0, 0), memory_space=pltpu.SMEM)
    wspec = lambda w: pl.BlockSpec((1,) + w.shape[1:], lambda i, e: (e, 0, 0))
    xs_rows = TOP_K * tt + N_EXPERTS * MOE_ALIGN + MOE_CHUNK + MOE_ALIGN
    return pl.pallas_call(
        _moe_ffn_kernel,
        grid=(nt, N_EXPERTS),
        in_specs=[smem((1, TOP_K, tt)), smem((1, TOP_K, tt)), smem((1, 1, LANES)),
                  pl.BlockSpec((tt, D_MODEL), lambda i, e: (i, 0), pipeline_mode=pl.Buffered(1)),
                  wspec(wg), wspec(wu), wspec(wd),
                  pl.BlockSpec((1, D_MODEL), lambda i, e: (0, 0)),
                  pl.BlockSpec((1, D_MODEL), lambda i, e: (0, 0))],
        out_specs=pl.BlockSpec((tt, D_MODEL), lambda i, e: (i, 0), pipeline_mode=pl.Buffered(1)),
        out_shape=jax.ShapeDtypeStruct((t, D_MODEL), F32),
        scratch_shapes=[pltpu.VMEM((xs_rows, D_MODEL), F32)],
        compiler_params=_cparams(("arbitrary", "arbitrary")),
        name="moe_ffn",
    )(pos, wts, meta, x2, wg, wu, wd, g, b)


def _nsa_weight_layout(w_in, b_gate):
    G, R, Dh = NSA_GROUPS, NSA_HPG, HEAD_DIM
    d = np.arange(Dh)
    q_cols = np.concatenate([(g * R + r) * Dh + d for r in range(R) for g in range(G)])
    base = NSA_Q_COLS

    def kv(i):
        return base + i * NSA_KV_COLS + np.arange(NSA_KV_COLS)

    kc, vc, ks, vs, kw, vw = (kv(i) for i in range(6))
    gate_base = base + 6 * NSA_KV_COLS
    cols = np.concatenate([q_cols, ks, kw, kc, vc, vs, vw])
    gate_src = np.array([(g * R + r) * 3 + br for g in range(G) for br in range(3) for r in range(R)])
    gate_dst = np.array([g * LANES + br * R + r for g in range(G) for br in range(3) for r in range(R)])
    wg = jnp.zeros((w_in.shape[0], _NSA_GATE_COLS), w_in.dtype)
    wg = wg.at[:, gate_dst].set(jnp.take(w_in, jnp.asarray(gate_base + gate_src), axis=1))
    w = jnp.concatenate([jnp.take(w_in, jnp.asarray(cols), axis=1), wg], axis=1).astype(MXU_DTYPE)
    bg = jnp.zeros((1, _NSA_GATE_COLS), b_gate.dtype).at[0, gate_dst].set(b_gate[gate_src])
    return w, bg


def _nsa_out_layout(w_out):
    G, R, Dh = NSA_GROUPS, NSA_HPG, HEAD_DIM
    d = np.arange(Dh)
    rows = np.concatenate([(g * R + r) * Dh + d for r in range(R) for g in range(G)])
    return jnp.take(w_out, jnp.asarray(rows), axis=0).astype(MXU_DTYPE)


def _compress_weight_layout(pe, w1, w2):
    G, Dh, L = NSA_GROUPS, HEAD_DIM, CMP_BLOCK
    eye = jnp.eye(G, dtype=w1.dtype)
    w1r = w1.reshape(L, Dh, Dh)
    w1big = jnp.einsum("ldj,gh->lgdhj", w1r, eye).reshape(L * G * Dh, G * Dh)
    w2big = jnp.einsum("dj,gh->gdhj", w2, eye).reshape(G * Dh, G * Dh)
    pebig = jnp.broadcast_to(pe[:, None, :], (L, G, Dh)).reshape(1, L * G * Dh)
    return pebig, w1big.astype(MXU_DTYPE), w2big.astype(MXU_DTYPE)


def _block_diag(w):
    h, bi, bj = w.shape
    eye = jnp.eye(h, dtype=w.dtype)
    return jnp.einsum("hij,hk->hikj", w, eye).reshape(h * bi, h * bj).astype(MXU_DTYPE)


def _nsa_constants(s):
    nc = s // CMP_STRIDE
    n_cmp = (s - CMP_BLOCK) // CMP_STRIDE + 1
    n_sel = s // SEL_BLOCK
    cmp_start = np.arange(nc) * CMP_STRIDE
    cmp_end = cmp_start + CMP_BLOCK - 1
    sel_start = np.arange(n_sel) * SEL_BLOCK
    ov = ((cmp_start[:, None] <= sel_start[None, :] + SEL_BLOCK - 1)
          & (cmp_end[:, None] >= sel_start[None, :])
          & (np.arange(nc)[:, None] < n_cmp)).astype(np.float32)
    ov = np.pad(ov, ((0, 0), (0, LANES - n_sel)))
    et = (np.arange(s)[:, None] // SEL_BLOCK == np.arange(LANES)[None, :]).astype(np.float32)
    return jnp.asarray(ov), jnp.asarray(et, dtype=MXU_DTYPE)


def _nsa_layer(x2, bsz, s, tables, cmp_tables, w_in, b_gate, pos_k, w1_k, w2_k, pos_v, w1_v, w2_v,
               w_out, g, b):
    w_perm, bg_perm = _nsa_weight_layout(w_in, b_gate)
    q, ks, kw, kc, vc, vs, vw, gates = _nsa_proj(x2, w_perm, bg_perm, *tables)
    nc = s // CMP_STRIDE
    feat = CMP_STRIDE * NSA_KV_COLS
    pek, w1k, w2k = _compress_weight_layout(pos_k, w1_k, w2_k)
    pev, w1v, w2v = _compress_weight_layout(pos_v, w1_v, w2_v)
    kcmp, vcmp = _nsa_compress(kc.reshape(bsz, nc, feat), vc.reshape(bsz, nc, feat),
                               pek, pev, w1k, w1v, w2k, w2v, *cmp_tables)
    ov, et = _nsa_constants(s)
    r3 = lambda a: a.reshape(bsz, s, a.shape[-1])
    o = _nsa_attn(r3(q), r3(ks), r3(vs), r3(kw), r3(vw), kcmp, vcmp, r3(gates), ov, et)
    return _mm_res_ln(o.reshape(bsz * s, NSA_Q_COLS), _nsa_out_layout(w_out), x2, g, b)


def _moe_layer(x2, rwt, rb, tri, wg, wu, wd, g, b):
    pos, wts, meta = _moe_router(x2, rwt, rb, tri)
    return _moe_ffn(pos, wts, meta, x2, wg.astype(MXU_DTYPE), wu.astype(MXU_DTYPE),
                    wd.astype(MXU_DTYPE), g, b)


def kernel(x, positions, nsa_w_in, nsa_b_gate, nsa_cmp_pos_k, nsa_cmp_w1_k, nsa_cmp_w2_k, nsa_cmp_pos_v, nsa_cmp_w1_v, nsa_cmp_w2_v, nsa_w_out, sc_w_in, sc_conv_w, sc_conv_b, sc_w_out, lru_w_in, lru_conv_w, lru_conv_b, lru_wa, lru_ba, lru_wx, lru_bx, lru_lambda, lru_w_out, router_w, router_b, moe_w_gate, moe_w_up, moe_w_down, ln_g, ln_b):
    bsz, s, d = x.shape
    t = bsz * s
    x2 = x.reshape(t, d)
    row = lambda v: v.reshape(1, -1)

    tables = _rope_tables(positions.reshape(t))
    nc = s // CMP_STRIDE
    cmp_idx = np.minimum(np.arange(nc) * CMP_STRIDE + CMP_BLOCK - 1, s - 1)
    cmp_tables = tuple(tb.reshape(bsz, nc, LANES)
                       for tb in _rope_tables(positions[:, cmp_idx].reshape(bsz * nc)))

    tt = min(MOE_TOKENS, t)
    tri = jnp.asarray(np.triu(np.ones((tt, tt), np.float32)), dtype=MXU_DTYPE)
    rwt = router_w.T
    rb = router_b.reshape(N_EXPERTS, 1)

    for i in range(DEPTH):
        kind, j = i % N_MIXERS, i // N_MIXERS
        g0, b0 = row(ln_g[i, 0]), row(ln_b[i, 0])
        if kind == 0:
            x2 = _nsa_layer(x2, bsz, s, tables, cmp_tables, nsa_w_in[j], nsa_b_gate[j],
                            nsa_cmp_pos_k[j], nsa_cmp_w1_k[j], nsa_cmp_w2_k[j],
                            nsa_cmp_pos_v[j], nsa_cmp_w1_v[j], nsa_cmp_w2_v[j], nsa_w_out[j], g0, b0)
        elif kind == 1:
            x2 = _shortconv_layer(x2.reshape(bsz, s, d), sc_w_in[j].astype(MXU_DTYPE), sc_conv_w[j],
                                  row(sc_conv_b[j]), sc_w_out[j].astype(MXU_DTYPE), g0, b0).reshape(t, d)
        else:
            x2 = _rglru_layer(x2.reshape(bsz, s, d), lru_w_in[j].astype(MXU_DTYPE), lru_conv_w[j],
                              row(lru_conv_b[j]), _block_diag(lru_wa[j]), row(lru_ba[j]),
                              _block_diag(lru_wx[j]), row(lru_bx[j]), row(lru_lambda[j]),
                              lru_w_out[j].astype(MXU_DTYPE), g0, b0).reshape(t, d)
        x2 = _moe_layer(x2, rwt, rb, tri, moe_w_gate[i], moe_w_up[i], moe_w_down[i],
                        row(ln_g[i, 1]), row(ln_b[i, 1]))
    return x2.reshape(bsz, s, d)
```

```python
import functools

import numpy as np
import jax
import jax.numpy as jnp
from jax import lax
from jax.experimental import pallas as pl
from jax.experimental.pallas import tpu as pltpu

F32 = jnp.float32
BF16 = jnp.bfloat16
MXU_DTYPE = BF16

D_MODEL = 1024
DEPTH = 4
N_MIXERS = 3
NSA_HEADS = 16
NSA_GROUPS = 4
NSA_HPG = NSA_HEADS // NSA_GROUPS
HEAD_DIM = D_MODEL // NSA_HEADS
CMP_BLOCK = 32
CMP_STRIDE = 16
SEL_BLOCK = 64
SEL_TOP = 16
WINDOW = 512
ROPE_THETA = 10000.0
FORCE_SCORE = 1e6
NSA_Q_COLS = NSA_HEADS * HEAD_DIM
NSA_KV_COLS = NSA_GROUPS * HEAD_DIM
SC_WIDTH = 3
LRU_WIDTH = 1280
LRU_BLOCKS = 16
LRU_BW = LRU_WIDTH // LRU_BLOCKS
LRU_CONV = 4
LRU_C = 8.0
N_EXPERTS = 16
N_GROUPS = 4
EXPERTS_PER_GROUP = N_EXPERTS // N_GROUPS
TOP_K = 2
D_FF = 1024
DN_ALPHA = (2.0 * DEPTH) ** 0.25
LN_EPS = 1e-5
NEG_INF = -1e30
LOG2_E = 1.4426950408889634

LANES = 128
SUBLANES = 8
VMEM_LIMIT_BYTES = 56 * 1024 * 1024
ROW_TILE = 512
LRU_ROW_TILE = 256
ATT_TQ = 256
ATT_TK = 256
MOE_TOKENS = 2048
MOE_CHUNK = 128
MOE_ALIGN = SUBLANES


def _cparams(sem):
    return pltpu.CompilerParams(dimension_semantics=sem, vmem_limit_bytes=VMEM_LIMIT_BYTES)


def _dot(a, b):
    return jnp.dot(a, b, preferred_element_type=F32)


def _dot_nt(a, b, precision=None):
    return lax.dot_general(a, b, (((1,), (1,)), ((), ())), preferred_element_type=F32,
                           precision=precision)


def _layer_norm_rows(y, g, b):
    mu = jnp.mean(y, axis=-1, keepdims=True)
    d = y - mu
    var = jnp.mean(d * d, axis=-1, keepdims=True)
    return d * lax.rsqrt(var + LN_EPS) * g + b


def _const_spec(shape):
    nd = len(shape)
    return pl.BlockSpec(shape, lambda *_: (0,) * nd)


def _rope_tables_kernel(pos_ref, inv_ref, cos_ref, sina_ref, sinb_ref):
    ang = pos_ref[...] * inv_ref[...]
    c = jnp.cos(ang)
    s = jnp.sin(ang)
    lane = lax.broadcasted_iota(jnp.int32, ang.shape, 1)
    first = (lane % HEAD_DIM) < (HEAD_DIM // 2)
    cos_ref[...] = c
    sina_ref[...] = jnp.where(first, -s, 0.0)
    sinb_ref[...] = jnp.where(first, 0.0, s)


def _rope_tables(pos_flat):
    n = pos_flat.shape[0]
    tm = min(n, 1024)
    half = HEAD_DIM // 2
    inv = ROPE_THETA ** (-jnp.arange(half, dtype=F32) / half)
    inv_lane = jnp.tile(inv, LANES // half)[None, :]
    posb = jnp.broadcast_to(pos_flat.astype(F32)[:, None], (n, LANES))
    spec = pl.BlockSpec((tm, LANES), lambda i: (i, 0))
    out = jax.ShapeDtypeStruct((n, LANES), F32)
    return pl.pallas_call(
        _rope_tables_kernel,
        grid=(n // tm,),
        in_specs=[spec, _const_spec((1, LANES))],
        out_specs=[spec, spec, spec],
        out_shape=[out, out, out],
        compiler_params=_cparams(("arbitrary",)),
        name="rope_tables",
    )(posb, inv_lane)


def _rope128(v, cos, sina, sinb):
    half = HEAD_DIM // 2
    return (v * cos + pltpu.roll(v, LANES - half, 1) * sina + pltpu.roll(v, half, 1) * sinb)


_NSA_SEG = {
    "q": 0, "ks": 1024, "kw": 1280, "kc": 1536, "vc": 1792, "vs": 2048, "vw": 2304, "gate": 2560,
}
_NSA_GATE_COLS = NSA_GROUPS * LANES
_NSA_PROJ_COLS = 2560 + _NSA_GATE_COLS


def _nsa_proj_kernel(x_ref, w_ref, bg_ref, cos_ref, sina_ref, sinb_ref,
                     q_ref, ks_ref, kw_ref, kc_ref, vc_ref, vs_ref, vw_ref, g_ref):
    xb = x_ref[...].astype(MXU_DTYPE)
    cos = cos_ref[...]
    sina = sina_ref[...]
    sinb = sinb_ref[...]
    scale = HEAD_DIM ** -0.5 * LOG2_E

    def seg(name, width=256, c=0):
        o = _NSA_SEG[name] + c * 256
        return _dot(xb, w_ref[:, o:o + width])

    for c in range(NSA_Q_COLS // 256):
        acc = seg("q", c=c)
        for h in range(2):
            r = _rope128(acc[:, h * LANES:(h + 1) * LANES], cos, sina, sinb) * scale
            q_ref[:, c * 256 + h * LANES:c * 256 + (h + 1) * LANES] = r.astype(q_ref.dtype)
    for name, ref in (("ks", ks_ref), ("kw", kw_ref)):
        acc = seg(name)
        for h in range(2):
            r = _rope128(acc[:, h * LANES:(h + 1) * LANES], cos, sina, sinb)
            ref[:, h * LANES:(h + 1) * LANES] = r.astype(ref.dtype)
    kc_ref[...] = seg("kc")
    vc_ref[...] = seg("vc")
    vs_ref[...] = seg("vs").astype(vs_ref.dtype)
    vw_ref[...] = seg("vw").astype(vw_ref.dtype)
    g_ref[...] = jax.nn.sigmoid(seg("gate", width=_NSA_GATE_COLS) + bg_ref[...])


def _nsa_proj(x2, w_perm, bg_perm, cos_t, sina_t, sinb_t):
    t = x2.shape[0]
    tm = ROW_TILE
    row = lambda w: pl.BlockSpec((tm, w), lambda i: (i, 0))
    outs = [
        jax.ShapeDtypeStruct((t, NSA_Q_COLS), MXU_DTYPE),
        jax.ShapeDtypeStruct((t, NSA_KV_COLS), MXU_DTYPE),
        jax.ShapeDtypeStruct((t, NSA_KV_COLS), MXU_DTYPE),
        jax.ShapeDtypeStruct((t, NSA_KV_COLS), F32),
        jax.ShapeDtypeStruct((t, NSA_KV_COLS), F32),
        jax.ShapeDtypeStruct((t, NSA_KV_COLS), MXU_DTYPE),
        jax.ShapeDtypeStruct((t, NSA_KV_COLS), MXU_DTYPE),
        jax.ShapeDtypeStruct((t, _NSA_GATE_COLS), F32),
    ]
    return pl.pallas_call(
        _nsa_proj_kernel,
        grid=(t // tm,),
        in_specs=[row(D_MODEL), _const_spec((D_MODEL, _NSA_PROJ_COLS)),
                  _const_spec((1, _NSA_GATE_COLS)), row(LANES), row(LANES), row(LANES)],
        out_specs=[row(NSA_Q_COLS)] + [row(NSA_KV_COLS)] * 6 + [row(_NSA_GATE_COLS)],
        out_shape=outs,
        compiler_params=_cparams(("arbitrary",)),
        name="nsa_proj",
    )(x2, w_perm, bg_perm, cos_t, sina_t, sinb_t)


def _nsa_compress_kernel(kc_ref, vc_ref, pek_ref, pev_ref, w1k_ref, w1v_ref, w2k_ref, w2v_ref,
                         cos_ref, sina_ref, sinb_ref, ko_ref, vo_ref):
    nc = kc_ref.shape[1]
    half_feat = kc_ref.shape[2]
    rows = lax.broadcasted_iota(jnp.int32, (nc, 1), 0)
    valid = rows < nc - 1

    def compress(tok, pe_ref, w1_ref, w2_ref):
        a = (tok + pe_ref[:, :half_feat]).astype(MXU_DTYPE)
        b = (tok + pe_ref[:, half_feat:]).astype(MXU_DTYPE)
        pa = _dot(a, w1_ref[:half_feat, :])
        pb = _dot(b, w1_ref[half_feat:, :])
        pre = pa + pltpu.roll(pb, nc - 1, 0)
        h = jax.nn.gelu(pre)
        return _dot(h.astype(MXU_DTYPE), w2_ref[...])

    kcmp = compress(kc_ref[0], pek_ref, w1k_ref, w2k_ref)
    vcmp = compress(vc_ref[0], pev_ref, w1v_ref, w2v_ref)
    cos = cos_ref[0]
    sina = sina_ref[0]
    sinb = sinb_ref[0]
    for h in range(NSA_KV_COLS // LANES):
        r = _rope128(kcmp[:, h * LANES:(h + 1) * LANES], cos, sina, sinb)
        ko_ref[0, :, h * LANES:(h + 1) * LANES] = jnp.where(valid, r, 0.0).astype(ko_ref.dtype)
    vo_ref[0] = jnp.where(valid, vcmp, 0.0).astype(vo_ref.dtype)


def _nsa_compress(kc3, vc3, pek, pev, w1k, w1v, w2k, w2v, cos_c, sina_c, sinb_c):
    b, nc, feat = kc3.shape
    tok = pl.BlockSpec((1, nc, feat), lambda i: (i, 0, 0))
    tab = pl.BlockSpec((1, nc, LANES), lambda i: (i, 0, 0))
    out = pl.BlockSpec((1, nc, NSA_KV_COLS), lambda i: (i, 0, 0))
    oshape = jax.ShapeDtypeStruct((b, nc, NSA_KV_COLS), MXU_DTYPE)
    return pl.pallas_call(
        _nsa_compress_kernel,
        grid=(b,),
        in_specs=[tok, tok, _const_spec(pek.shape), _const_spec(pev.shape),
                  _const_spec(w1k.shape), _const_spec(w1v.shape),
                  _const_spec(w2k.shape), _const_spec(w2v.shape), tab, tab, tab],
        out_specs=[out, out],
        out_shape=[oshape, oshape],
        compiler_params=_cparams(("arbitrary",)),
        name="nsa_compress",
    )(kc3, vc3, pek, pev, w1k, w1v, w2k, w2v, cos_c, sina_c, sinb_c)


def _nsa_attn_kernel(q_ref, ks_ref, vs_ref, kw_ref, vw_ref, kc_ref, vc_ref, g_ref, ov_ref, et_ref,
                     o_ref, m_ref, acc_ref, out_ref, *, n_sel, k_sel):
    tq = q_ref.shape[1]
    tk = ATT_TK
    nc = kc_ref.shape[1]
    R = NSA_HPG
    qi = pl.program_id(1)
    g = pl.program_id(2)
    t0 = qi * tq
    trow = t0 + lax.broadcasted_iota(jnp.int32, (tq, 1), 0)
    gmask = lax.broadcasted_iota(jnp.int32, (1, NSA_KV_COLS), 1) // HEAD_DIM == g
    gates = g_ref[0]
    qs = jnp.concatenate(
        [jnp.where(gmask, q_ref[0, :, r * NSA_KV_COLS:(r + 1) * NSA_KV_COLS], 0)
         for r in range(R)], axis=0)
    rows = lambda a, r: a[r * tq:(r + 1) * tq]

    def tile_update(c, k_ref, v_ref, k0, bias):
        kt = k_ref[0, pl.ds(k0, tk), :]
        vt = jnp.where(gmask, v_ref[0, pl.ds(k0, tk), :], 1)
        s_all = _dot_nt(qs, kt)
        ps, alphas = [], []
        for r in range(R):
            s = rows(s_all, r) + bias
            m_old = m_ref[c, r]
            m_new = jnp.maximum(m_old, jnp.max(s, axis=-1, keepdims=True))
            m_ref[c, r] = m_new
            alphas.append(jnp.exp2(m_old - m_new))
            m_wide = jnp.concatenate([m_new] * (tk // LANES), axis=1)
            ps.append(jnp.exp2(s - m_wide).astype(MXU_DTYPE))
        pv = _dot(jnp.concatenate(ps, axis=0), vt)
        for r in range(R):
            a_wide = jnp.concatenate([alphas[r]] * (NSA_KV_COLS // LANES), axis=1)
            acc_ref[c, r] = a_wide * acc_ref[c, r] + rows(pv, r)

    n_idx = lax.broadcasted_iota(jnp.int32, (1, nc), 1)
    cmp_ok = (n_idx * CMP_STRIDE + CMP_BLOCK - 1) <= trow
    s_all = _dot_nt(qs, kc_ref[0])
    psum = jnp.zeros((tq, nc), F32)
    ps = []
    for r in range(R):
        s = jnp.where(cmp_ok, rows(s_all, r), NEG_INF)
        e = jnp.where(cmp_ok, jnp.exp2(s - jnp.max(s, axis=-1, keepdims=True)), 0.0)
        p = e * (1.0 / jnp.maximum(jnp.sum(e, axis=-1, keepdims=True), 1e-30))
        psum = psum + p
        ps.append(p.astype(MXU_DTYPE))
    o_cmp = _dot(jnp.concatenate(ps, axis=0), vc_ref[0])

    imp = jnp.dot(psum, ov_ref[...], preferred_element_type=F32,
                  precision=lax.Precision.HIGHEST)
    blk = lax.broadcasted_iota(jnp.int32, (1, LANES), 1)
    rel = trow // SEL_BLOCK - blk
    imp = jnp.where(rel >= 0, imp, -1.0)
    forced = (blk == 0) | (rel == 0) | (rel == 1)
    imp = jnp.where(forced, FORCE_SCORE, imp)
    imp_t = jnp.transpose(imp)[:n_sel, :]
    midx = lax.broadcasted_iota(jnp.int32, (n_sel, 1), 0)
    cnt = jnp.zeros((n_sel, tq), F32)
    for mp in range(n_sel):
        row = imp_t[mp:mp + 1, :]
        beats = (row > imp_t) | ((row == imp_t) & (midx > mp))
        cnt = cnt + jnp.where(beats, 1.0, 0.0)
    sel_t = jnp.where(cnt < k_sel, 1.0, 0.0)
    if n_sel < LANES:
        sel_t = jnp.concatenate([sel_t, jnp.zeros((LANES - n_sel, tq), F32)], axis=0)
    sel = jnp.transpose(sel_t).astype(MXU_DTYPE)

    SEL, WIN = 0, 1
    m_ref[...] = jnp.full(m_ref.shape, NEG_INF, F32)
    acc_ref[...] = jnp.zeros(acc_ref.shape, F32)

    def sel_update(k0, kpos):
        selm = _dot_nt(sel, et_ref[pl.ds(k0, tk), :])
        bias = jnp.where((selm > 0.5) & (kpos <= trow), 0.0, NEG_INF)
        tile_update(SEL, ks_ref, vs_ref, k0, bias)

    def win_update(k0, kpos):
        d = trow - kpos
        bias = jnp.where((d >= 0) & (d < WINDOW), 0.0, NEG_INF)
        tile_update(WIN, kw_ref, vw_ref, k0, bias)

    def key_tile(j):
        k0 = pl.multiple_of(j * tk, tk)
        return k0, k0 + lax.broadcasted_iota(jnp.int32, (1, tk), 1)

    def sel_only(j, carry):
        sel_update(*key_tile(j))
        return carry

    def sel_and_win(j, carry):
        k0, kpos = key_tile(j)
        sel_update(k0, kpos)
        win_update(k0, kpos)
        return carry

    j_win = jnp.maximum(t0 - WINDOW, 0) // tk
    lax.fori_loop(0, j_win, sel_only, 0)
    lax.fori_loop(j_win, (t0 + tq) // tk, sel_and_win, 0)

    @pl.when(g == 0)
    def _():
        out_ref[...] = jnp.zeros(out_ref.shape, F32)

    def normalized(c, r):
        den = jnp.where(g < NSA_GROUPS // 2, acc_ref[c, r, :, LANES:], acc_ref[c, r, :, :LANES])
        inv = 1.0 / den
        return acc_ref[c, r] * jnp.concatenate([inv, inv], axis=1)

    for r in range(R):
        o_r = (gates[:, r:r + 1] * rows(o_cmp, r)
               + gates[:, R + r:R + r + 1] * normalized(SEL, r)
               + gates[:, 2 * R + r:2 * R + r + 1] * normalized(WIN, r))
        out_ref[r] = jnp.where(gmask, o_r, out_ref[r])

    @pl.when(g == NSA_GROUPS - 1)
    def _():
        for r in range(R):
            o_ref[0, :, r * NSA_KV_COLS:(r + 1) * NSA_KV_COLS] = out_ref[r].astype(o_ref.dtype)


def _nsa_attn(q, ks, vs, kw, vw, kcmp, vcmp, gates, ov, et):
    b, s, _ = q.shape
    nc = kcmp.shape[1]
    tq = ATT_TQ
    n_sel = s // SEL_BLOCK
    k_sel = min(SEL_TOP, n_sel)
    qspec = pl.BlockSpec((1, tq, NSA_Q_COLS), lambda i, j, g: (i, j, 0))
    kv = pl.BlockSpec((1, s, NSA_KV_COLS), lambda i, j, g: (i, 0, 0))
    cmp_spec = pl.BlockSpec((1, nc, NSA_KV_COLS), lambda i, j, g: (i, 0, 0))
    gspec = pl.BlockSpec((1, tq, LANES), lambda i, j, g: (i, j, g))
    return pl.pallas_call(
        functools.partial(_nsa_attn_kernel, n_sel=n_sel, k_sel=k_sel),
        grid=(b, s // tq, NSA_GROUPS),
        in_specs=[qspec, kv, kv, kv, kv, cmp_spec, cmp_spec, gspec,
                  _const_spec(ov.shape), _const_spec(et.shape)],
        out_specs=qspec,
        out_shape=jax.ShapeDtypeStruct((b, s, NSA_Q_COLS), MXU_DTYPE),
        scratch_shapes=[pltpu.VMEM((2, NSA_HPG, tq, LANES), F32),
                        pltpu.VMEM((2, NSA_HPG, tq, NSA_KV_COLS), F32),
                        pltpu.VMEM((NSA_HPG, tq, NSA_KV_COLS), F32)],
        compiler_params=_cparams(("arbitrary", "arbitrary", "arbitrary")),
        name="nsa_attn",
    )(q, ks, vs, kw, vw, kcmp, vcmp, gates, ov, et)


def _mm_res_ln_kernel(a_ref, w_ref, x_ref, g_ref, b_ref, o_ref):
    m = _dot(a_ref[...].astype(MXU_DTYPE), w_ref[...])
    o_ref[...] = _layer_norm_rows(DN_ALPHA * x_ref[...] + m, g_ref[...], b_ref[...])


def _mm_res_ln(a, w, x2, g, b):
    t, k = a.shape
    tm = ROW_TILE
    return pl.pallas_call(
        _mm_res_ln_kernel,
        grid=(t // tm,),
        in_specs=[pl.BlockSpec((tm, k), lambda i: (i, 0)), _const_spec(w.shape),
                  pl.BlockSpec((tm, D_MODEL), lambda i: (i, 0)),
                  _const_spec((1, D_MODEL)), _const_spec((1, D_MODEL))],
        out_specs=pl.BlockSpec((tm, D_MODEL), lambda i: (i, 0)),
        out_shape=jax.ShapeDtypeStruct((t, D_MODEL), F32),
        compiler_params=_cparams(("arbitrary",)),
        name="mm_res_ln",
    )(a, w, x2, g, b)


def _shift_rows(v, prev, k):
    out = pltpu.roll(v, k, 0)
    rows = lax.broadcasted_iota(jnp.int32, (v.shape[0], 1), 0)
    np_rows = prev.shape[0]
    for i in range(k):
        out = jnp.where(rows == i, prev[np_rows - k + i:np_rows - k + i + 1, :], out)
    return out


def _shortconv_kernel(x_ref, win_ref, cw_ref, cb_ref, wout_ref, g_ref, b_ref, o_ref, carry_ref):
    @pl.when(pl.program_id(1) == 0)
    def _():
        carry_ref[...] = jnp.zeros(carry_ref.shape, F32)

    x = x_ref[0]
    xb = x.astype(MXU_DTYPE)
    d = D_MODEL
    u = _dot(xb, win_ref[:, 0:d])
    bg = _dot(xb, win_ref[:, d:2 * d])
    cg = _dot(xb, win_ref[:, 2 * d:3 * d])
    v = cg * u
    prev = carry_ref[...]
    z = cb_ref[...] + cw_ref[SC_WIDTH - 1:SC_WIDTH, :] * v
    for k in range(1, SC_WIDTH):
        z = z + cw_ref[SC_WIDTH - 1 - k:SC_WIDTH - k, :] * _shift_rows(v, prev, k)
    carry_ref[...] = v[v.shape[0] - SUBLANES:, :]
    m = _dot((bg * z).astype(MXU_DTYPE), wout_ref[...])
    o_ref[0] = _layer_norm_rows(DN_ALPHA * x + m, g_ref[...], b_ref[...])


def _shortconv_layer(x3, win, cw, cb, wout, g, b):
    bsz, s, d = x3.shape
    tm = ROW_TILE
    xspec = pl.BlockSpec((1, tm, d), lambda i, j: (i, j, 0))
    return pl.pallas_call(
        _shortconv_kernel,
        grid=(bsz, s // tm),
        in_specs=[xspec, _const_spec(win.shape), _const_spec(cw.shape), _const_spec(cb.shape),
                  _const_spec(wout.shape), _const_spec(g.shape), _const_spec(b.shape)],
        out_specs=xspec,
        out_shape=jax.ShapeDtypeStruct(x3.shape, F32),
        scratch_shapes=[pltpu.VMEM((SUBLANES, d), F32)],
        compiler_params=_cparams(("arbitrary", "arbitrary")),
        name="shortconv_layer",
    )(x3, win, cw, cb, wout, g, b)


def _rglru_kernel(x_ref, win_ref, cw_ref, cb_ref, wa_ref, ba_ref, wx_ref, bx_ref, lam_ref,
                  wout_ref, g_ref, b_ref, o_ref, carry_ref, h_ref, a_s, b_s):
    @pl.when(pl.program_id(1) == 0)
    def _():
        carry_ref[...] = jnp.zeros(carry_ref.shape, F32)
        h_ref[...] = jnp.zeros(h_ref.shape, F32)

    x = x_ref[0]
    tm = x.shape[0]
    xb = x.astype(MXU_DTYPE)
    w = LRU_WIDTH
    gate_br = _dot(xb, win_ref[:, 0:w])
    rec = _dot(xb, win_ref[:, w:2 * w])
    prev = carry_ref[...]
    u = cb_ref[...] + cw_ref[LRU_CONV - 1:LRU_CONV, :] * rec
    for k in range(1, LRU_CONV):
        u = u + cw_ref[LRU_CONV - 1 - k:LRU_CONV - k, :] * _shift_rows(rec, prev, k)
    carry_ref[...] = rec[tm - SUBLANES:, :]
    ub = u.astype(MXU_DTYPE)
    r = jax.nn.sigmoid(_dot(ub, wa_ref[...]) + ba_ref[...])
    ig = jax.nn.sigmoid(_dot(ub, wx_ref[...]) + bx_ref[...])
    log_a = LRU_C * r * jax.nn.log_sigmoid(lam_ref[...])
    a_s[...] = jnp.exp(log_a)
    th = jnp.tanh(log_a)
    b_s[...] = jnp.sqrt(-2.0 * th / (1.0 - th)) * (ig * u)

    rows = lax.broadcasted_iota(jnp.int32, (SUBLANES, 1), 0)

    def group(i, hc):
        r0 = pl.multiple_of(i * SUBLANES, SUBLANES)
        a = a_s[pl.ds(r0, SUBLANES), :]
        bv = b_s[pl.ds(r0, SUBLANES), :]
        for dlt in (1, 2, 4):
            a_sh = jnp.where(rows >= dlt, pltpu.roll(a, dlt, 0), 1.0)
            b_sh = jnp.where(rows >= dlt, pltpu.roll(bv, dlt, 0), 0.0)
            bv = a * b_sh + bv
            a = a * a_sh
        h = a * hc + bv
        b_s[pl.ds(r0, SUBLANES), :] = h
        return h[SUBLANES - 1:SUBLANES, :]

    h_ref[...] = lax.fori_loop(0, tm // SUBLANES, group, h_ref[...])
    y = jax.nn.gelu(gate_br) * b_s[...]
    m = _dot(y.astype(MXU_DTYPE), wout_ref[...])
    o_ref[0] = _layer_norm_rows(DN_ALPHA * x + m, g_ref[...], b_ref[...])


def _rglru_layer(x3, win, cw, cb, wa, ba, wx, bx, lam, wout, g, b):
    bsz, s, d = x3.shape
    tm = LRU_ROW_TILE
    xspec = pl.BlockSpec((1, tm, d), lambda i, j: (i, j, 0))
    consts = [win, cw, cb, wa, ba, wx, bx, lam, wout, g, b]
    return pl.pallas_call(
        _rglru_kernel,
        grid=(bsz, s // tm),
        in_specs=[xspec] + [_const_spec(c.shape) for c in consts],
        out_specs=xspec,
        out_shape=jax.ShapeDtypeStruct(x3.shape, F32),
        scratch_shapes=[pltpu.VMEM((SUBLANES, LRU_WIDTH), F32), pltpu.VMEM((1, LRU_WIDTH), F32),
                        pltpu.VMEM((tm, LRU_WIDTH), F32), pltpu.VMEM((tm, LRU_WIDTH), F32)],
        compiler_params=_cparams(("arbitrary", "arbitrary")),
        name="rglru_layer",
    )(x3, win, cw, cb, wa, ba, wx, bx, lam, wout, g, b)


def _moe_router_kernel(x_ref, rwt_ref, rb_ref, tri_ref, pos_ref, wts_ref, meta_ref):
    tt = x_ref.shape[0]
    x = x_ref[...]
    w = rwt_ref[...]
    xh = x.astype(MXU_DTYPE)
    wh = w.astype(MXU_DTYPE)
    xl = (x - xh.astype(F32)).astype(MXU_DTYPE)
    wl = (w - wh.astype(F32)).astype(MXU_DTYPE)
    logits = _dot_nt(wh, xh) + _dot_nt(wh, xl) + _dot_nt(wl, xh)
    scores = jax.nn.sigmoid(logits)
    biased = scores + rb_ref[...]
    rows = [biased[e:e + 1, :] for e in range(N_EXPERTS)]

    gscore = []
    for gi in range(N_GROUPS):
        a, b, c, d = rows[gi * EXPERTS_PER_GROUP:(gi + 1) * EXPERTS_PER_GROUP]
        hi1, lo1 = jnp.maximum(a, b), jnp.minimum(a, b)
        hi2, lo2 = jnp.maximum(c, d), jnp.minimum(c, d)
        gscore.append(jnp.maximum(hi1, hi2) + jnp.maximum(jnp.minimum(hi1, hi2),
                                                          jnp.maximum(lo1, lo2)))
    best = gscore[0]
    top_g = jnp.zeros((1, tt), jnp.int32)
    for gi in range(1, N_GROUPS):
        better = gscore[gi] > best
        best = jnp.where(better, gscore[gi], best)
        top_g = jnp.where(better, gi, top_g)

    def argmax_in_group(exclude):
        bv = jnp.full((1, tt), -jnp.inf, F32)
        bi = jnp.full((1, tt), -1, jnp.int32)
        for e in range(N_EXPERTS):
            ok = top_g == (e // EXPERTS_PER_GROUP)
            if exclude is not None:
                ok = ok & (exclude != e)
            better = ok & (rows[e] > bv)
            bv = jnp.where(better, rows[e], bv)
            bi = jnp.where(better, e, bi)
        return bi

    e1 = argmax_in_group(None)
    e2 = argmax_in_group(e1)
    eidx = lax.broadcasted_iota(jnp.int32, (N_EXPERTS, 1), 0)
    oh1 = eidx == e1
    oh2 = eidx == e2
    s1 = jnp.sum(jnp.where(oh1, scores, 0.0), axis=0, keepdims=True)
    s2 = jnp.sum(jnp.where(oh2, scores, 0.0), axis=0, keepdims=True)
    wsum = s1 + s2
    wts_ref[0, 0:1, :] = s1 / wsum
    wts_ref[0, 1:2, :] = s2 / wsum

    oh = jnp.where(oh1 | oh2, 1.0, 0.0)
    csum = _dot(oh.astype(MXU_DTYPE), tri_ref[...])
    excl = csum - oh
    cnt = csum[:, tt - 1:tt]
    padded = jnp.ceil(cnt * (1.0 / MOE_ALIGN)) * MOE_ALIGN
    run = jnp.zeros((1, 1), F32)
    off_rows = []
    for e in range(N_EXPERTS):
        off_rows.append(run)
        run = run + padded[e:e + 1, :]
    off = jnp.concatenate(off_rows, axis=0)
    slot = off + excl
    pos_ref[0, 0:1, :] = jnp.sum(jnp.where(oh1, slot, 0.0), axis=0, keepdims=True).astype(jnp.int32)
    pos_ref[0, 1:2, :] = jnp.sum(jnp.where(oh2, slot, 0.0), axis=0, keepdims=True).astype(jnp.int32)
    lane = lax.broadcasted_iota(jnp.int32, (1, LANES), 1)
    meta = jnp.zeros((1, LANES), F32)
    for e in range(N_EXPERTS):
        meta = jnp.where(lane == e, off[e:e + 1, :], meta)
        meta = jnp.where(lane == N_EXPERTS + e, cnt[e:e + 1, :], meta)
    meta_ref[0] = meta.astype(jnp.int32)


def _moe_router(x2, rwt, rb, tri):
    t = x2.shape[0]
    tt = tri.shape[0]
    nt = t // tt
    return pl.pallas_call(
        _moe_router_kernel,
        grid=(nt,),
        in_specs=[pl.BlockSpec((tt, D_MODEL), lambda i: (i, 0)), _const_spec(rwt.shape),
                  _const_spec(rb.shape), _const_spec(tri.shape)],
        out_specs=[pl.BlockSpec((1, TOP_K, tt), lambda i: (i, 0, 0)),
                   pl.BlockSpec((1, TOP_K, tt), lambda i: (i, 0, 0)),
                   pl.BlockSpec((1, 1, LANES), lambda i: (i, 0, 0))],
        out_shape=[jax.ShapeDtypeStruct((nt, TOP_K, tt), jnp.int32),
                   jax.ShapeDtypeStruct((nt, TOP_K, tt), F32),
                   jax.ShapeDtypeStruct((nt, 1, LANES), jnp.int32)],
        compiler_params=_cparams(("arbitrary",)),
        name="moe_router",
    )(x2, rwt, rb, tri)


def _moe_ffn_kernel(pos_ref, wts_ref, meta_ref, x_ref, wg_ref, wu_ref, wd_ref, g_ref, b_ref,
                    o_ref, xs_ref):
    tt = x_ref.shape[0]
    ch = MOE_CHUNK
    e = pl.program_id(1)

    def token_group(body):
        def group(i, carry):
            base = pl.multiple_of(i * SUBLANES, SUBLANES)
            for k in range(SUBLANES):
                body(base + k)
            return carry

        lax.fori_loop(0, tt // SUBLANES, group, 0)

    @pl.when(e == 0)
    def _dispatch():
        def zero_gap(ei, carry):
            start = meta_ref[ei] + meta_ref[N_EXPERTS + ei]
            stop = jnp.where(ei == N_EXPERTS - 1, start + ch + MOE_ALIGN,
                             meta_ref[jnp.minimum(ei + 1, N_EXPERTS - 1)])

            def zrow(r, c):
                xs_ref[pl.ds(r, 1), :] = jnp.zeros((1, D_MODEL), F32)
                return c

            return lax.fori_loop(start, stop, zrow, carry)

        lax.fori_loop(0, N_EXPERTS, zero_gap, 0)

        def scatter(t):
            row = x_ref[pl.ds(t, 1), :]
            xs_ref[pl.ds(pos_ref[t], 1), :] = row
            xs_ref[pl.ds(pos_ref[tt + t], 1), :] = row

        token_group(scatter)

    off = meta_ref[e]
    cnt = meta_ref[N_EXPERTS + e]
    rid = lax.broadcasted_iota(jnp.int32, (ch, 1), 0)

    def chunk(c, carry):
        r0 = pl.multiple_of(off + c * ch, MOE_ALIGN)
        xs = xs_ref[pl.ds(r0, ch), :]
        xb = xs.astype(MXU_DTYPE)
        hid = jax.nn.silu(_dot(xb, wg_ref[0])) * _dot(xb, wu_ref[0])
        out = _dot(hid.astype(MXU_DTYPE), wd_ref[0])
        xs_ref[pl.ds(r0, ch), :] = jnp.where(rid + c * ch < cnt, out, xs)
        return carry

    lax.fori_loop(0, (cnt + ch - 1) // ch, chunk, 0)

    @pl.when(e == N_EXPERTS - 1)
    def _combine():
        def gather(t):
            o_ref[pl.ds(t, 1), :] = (wts_ref[t] * xs_ref[pl.ds(pos_ref[t], 1), :]
                                     + wts_ref[tt + t] * xs_ref[pl.ds(pos_ref[tt + t], 1), :])

        token_group(gather)

        def norm(i, carry):
            r0 = pl.multiple_of(i * ch, ch)
            y = DN_ALPHA * x_ref[pl.ds(r0, ch), :] + o_ref[pl.ds(r0, ch), :]
            o_ref[pl.ds(r0, ch), :] = _layer_norm_rows(y, g_ref[...], b_ref[...])
            return carry

        lax.fori_loop(0, tt // ch, norm, 0)


def _moe_ffn(pos, wts, meta, x2, wg, wu, wd, g, b):
    t = x2.shape[0]
    tt = pos.shape[2]
    nt = t // tt
    smem = lambda n: pl.BlockSpec((n,), lambda i, e: (i,), memory_space=pltpu.SMEM)
    wspec = lambda w: pl.BlockSpec((1,) + w.shape[1:], lambda i, e: (e, 0, 0))
    xs_rows = TOP_K * tt + N_EXPERTS * MOE_ALIGN + MOE_CHUNK + MOE_ALIGN
    pos, wts, meta = pos.reshape(-1), wts.reshape(-1), meta.reshape(-1)
    return pl.pallas_call(
        _moe_ffn_kernel,
        grid=(nt, N_EXPERTS),
        in_specs=[smem(TOP_K * tt), smem(TOP_K * tt), smem(LANES),
                  pl.BlockSpec((tt, D_MODEL), lambda i, e: (i, 0), pipeline_mode=pl.Buffered(1)),
                  wspec(wg), wspec(wu), wspec(wd),
                  pl.BlockSpec((1, D_MODEL), lambda i, e: (0, 0)),
                  pl.BlockSpec((1, D_MODEL), lambda i, e: (0, 0))],
        out_specs=pl.BlockSpec((tt, D_MODEL), lambda i, e: (i, 0), pipeline_mode=pl.Buffered(1)),
        out_shape=jax.ShapeDtypeStruct((t, D_MODEL), F32),
        scratch_shapes=[pltpu.VMEM((xs_rows, D_MODEL), F32)],
        compiler_params=_cparams(("arbitrary", "arbitrary")),
        name="moe_ffn",
    )(pos, wts, meta, x2, wg, wu, wd, g, b)


def _nsa_weight_layout(w_in, b_gate):
    G, R, Dh = NSA_GROUPS, NSA_HPG, HEAD_DIM
    d = np.arange(Dh)
    q_cols = np.concatenate([(g * R + r) * Dh + d for r in range(R) for g in range(G)])
    base = NSA_Q_COLS

    def kv(i):
        return base + i * NSA_KV_COLS + np.arange(NSA_KV_COLS)

    kc, vc, ks, vs, kw, vw = (kv(i) for i in range(6))
    gate_base = base + 6 * NSA_KV_COLS
    cols = np.concatenate([q_cols, ks, kw, kc, vc, vs, vw])
    gate_src = np.array([(g * R + r) * 3 + br for g in range(G) for br in range(3) for r in range(R)])
    gate_dst = np.array([g * LANES + br * R + r for g in range(G) for br in range(3) for r in range(R)])
    wg = jnp.zeros((w_in.shape[0], _NSA_GATE_COLS), w_in.dtype)
    wg = wg.at[:, gate_dst].set(jnp.take(w_in, jnp.asarray(gate_base + gate_src), axis=1))
    w = jnp.concatenate([jnp.take(w_in, jnp.asarray(cols), axis=1), wg], axis=1).astype(MXU_DTYPE)
    bg = jnp.zeros((1, _NSA_GATE_COLS), b_gate.dtype).at[0, gate_dst].set(b_gate[gate_src])
    return w, bg


def _nsa_out_layout(w_out):
    G, R, Dh = NSA_GROUPS, NSA_HPG, HEAD_DIM
    d = np.arange(Dh)
    rows = np.concatenate([(g * R + r) * Dh + d for r in range(R) for g in range(G)])
    return jnp.take(w_out, jnp.asarray(rows), axis=0).astype(MXU_DTYPE)


def _compress_weight_layout(pe, w1, w2):
    G, Dh, L = NSA_GROUPS, HEAD_DIM, CMP_BLOCK
    eye = jnp.eye(G, dtype=w1.dtype)
    w1r = w1.reshape(L, Dh, Dh)
    w1big = jnp.einsum("ldj,gh->lgdhj", w1r, eye).reshape(L * G * Dh, G * Dh)
    w2big = jnp.einsum("dj,gh->gdhj", w2, eye).reshape(G * Dh, G * Dh)
    pebig = jnp.broadcast_to(pe[:, None, :], (L, G, Dh)).reshape(1, L * G * Dh)
    return pebig, w1big.astype(MXU_DTYPE), w2big.astype(MXU_DTYPE)


def _block_diag(w):
    h, bi, bj = w.shape
    eye = jnp.eye(h, dtype=w.dtype)
    return jnp.einsum("hij,hk->hikj", w, eye).reshape(h * bi, h * bj).astype(MXU_DTYPE)


def _nsa_constants(s):
    nc = s // CMP_STRIDE
    n_cmp = (s - CMP_BLOCK) // CMP_STRIDE + 1
    n_sel = s // SEL_BLOCK
    cmp_start = np.arange(nc) * CMP_STRIDE
    cmp_end = cmp_start + CMP_BLOCK - 1
    sel_start = np.arange(n_sel) * SEL_BLOCK
    ov = ((cmp_start[:, None] <= sel_start[None, :] + SEL_BLOCK - 1)
          & (cmp_end[:, None] >= sel_start[None, :])
          & (np.arange(nc)[:, None] < n_cmp)).astype(np.float32)
    ov = np.pad(ov, ((0, 0), (0, LANES - n_sel)))
    et = (np.arange(s)[:, None] // SEL_BLOCK == np.arange(LANES)[None, :]).astype(np.float32)
    return jnp.asarray(ov), jnp.asarray(et, dtype=MXU_DTYPE)


def _nsa_layer(x2, bsz, s, tables, cmp_tables, w_in, b_gate, pos_k, w1_k, w2_k, pos_v, w1_v, w2_v,
               w_out, g, b):
    w_perm, bg_perm = _nsa_weight_layout(w_in, b_gate)
    q, ks, kw, kc, vc, vs, vw, gates = _nsa_proj(x2, w_perm, bg_perm, *tables)
    nc = s // CMP_STRIDE
    feat = CMP_STRIDE * NSA_KV_COLS
    pek, w1k, w2k = _compress_weight_layout(pos_k, w1_k, w2_k)
    pev, w1v, w2v = _compress_weight_layout(pos_v, w1_v, w2_v)
    kcmp, vcmp = _nsa_compress(kc.reshape(bsz, nc, feat), vc.reshape(bsz, nc, feat),
                               pek, pev, w1k, w1v, w2k, w2v, *cmp_tables)
    ov, et = _nsa_constants(s)
    r3 = lambda a: a.reshape(bsz, s, a.shape[-1])
    o = _nsa_attn(r3(q), r3(ks), r3(vs), r3(kw), r3(vw), kcmp, vcmp, r3(gates), ov, et)
    return _mm_res_ln(o.reshape(bsz * s, NSA_Q_COLS), _nsa_out_layout(w_out), x2, g, b)


def _moe_layer(x2, rwt, rb, tri, wg, wu, wd, g, b):
    pos, wts, meta = _moe_router(x2, rwt, rb, tri)
    return _moe_ffn(pos, wts, meta, x2, wg.astype(MXU_DTYPE), wu.astype(MXU_DTYPE),
                    wd.astype(MXU_DTYPE), g, b)


def kernel(x, positions, nsa_w_in, nsa_b_gate, nsa_cmp_pos_k, nsa_cmp_w1_k, nsa_cmp_w2_k, nsa_cmp_pos_v, nsa_cmp_w1_v, nsa_cmp_w2_v, nsa_w_out, sc_w_in, sc_conv_w, sc_conv_b, sc_w_out, lru_w_in, lru_conv_w, lru_conv_b, lru_wa, lru_ba, lru_wx, lru_bx, lru_lambda, lru_w_out, router_w, router_b, moe_w_gate, moe_w_up, moe_w_down, ln_g, ln_b):
    bsz, s, d = x.shape
    t = bsz * s
    x2 = x.reshape(t, d)
    row = lambda v: v.reshape(1, -1)

    tables = _rope_tables(positions.reshape(t))
    nc = s // CMP_STRIDE
    cmp_idx = np.minimum(np.arange(nc) * CMP_STRIDE + CMP_BLOCK - 1, s - 1)
    cmp_tables = tuple(tb.reshape(bsz, nc, LANES)
                       for tb in _rope_tables(positions[:, cmp_idx].reshape(bsz * nc)))

    tt = min(MOE_TOKENS, t)
    tri = jnp.asarray(np.triu(np.ones((tt, tt), np.float32)), dtype=MXU_DTYPE)
    rwt = router_w.T
    rb = router_b.reshape(N_EXPERTS, 1)

    for i in range(DEPTH):
        kind, j = i % N_MIXERS, i // N_MIXERS
        g0, b0 = row(ln_g[i, 0]), row(ln_b[i, 0])
        if kind == 0:
            x2 = _nsa_layer(x2, bsz, s, tables, cmp_tables, nsa_w_in[j], nsa_b_gate[j],
                            nsa_cmp_pos_k[j], nsa_cmp_w1_k[j], nsa_cmp_w2_k[j],
                            nsa_cmp_pos_v[j], nsa_cmp_w1_v[j], nsa_cmp_w2_v[j], nsa_w_out[j], g0, b0)
        elif kind == 1:
            x2 = _shortconv_layer(x2.reshape(bsz, s, d), sc_w_in[j].astype(MXU_DTYPE), sc_conv_w[j],
                                  row(sc_conv_b[j]), sc_w_out[j].astype(MXU_DTYPE), g0, b0).reshape(t, d)
        else:
            x2 = _rglru_layer(x2.reshape(bsz, s, d), lru_w_in[j].astype(MXU_DTYPE), lru_conv_w[j],
                              row(lru_conv_b[j]), _block_diag(lru_wa[j]), row(lru_ba[j]),
                              _block_diag(lru_wx[j]), row(lru_bx[j]), row(lru_lambda[j]),
                              lru_w_out[j].astype(MXU_DTYPE), g0, b0).reshape(t, d)
        x2 = _moe_layer(x2, rwt, rb, tri, moe_w_gate[i], moe_w_up[i], moe_w_down[i],
                        row(ln_g[i, 1]), row(ln_b[i, 1]))
    return x2.reshape(bsz, s, d)
```

```python
import functools

import numpy as np
import jax
import jax.numpy as jnp
from jax import lax
from jax.experimental import pallas as pl
from jax.experimental.pallas import tpu as pltpu

F32 = jnp.float32
BF16 = jnp.bfloat16
MXU_DTYPE = BF16

D_MODEL = 1024
DEPTH = 4
N_MIXERS = 3
NSA_HEADS = 16
NSA_GROUPS = 4
NSA_HPG = NSA_HEADS // NSA_GROUPS
HEAD_DIM = D_MODEL // NSA_HEADS
CMP_BLOCK = 32
CMP_STRIDE = 16
SEL_BLOCK = 64
SEL_TOP = 16
WINDOW = 512
ROPE_THETA = 10000.0
FORCE_SCORE = 1e6
NSA_Q_COLS = NSA_HEADS * HEAD_DIM
NSA_KV_COLS = NSA_GROUPS * HEAD_DIM
SC_WIDTH = 3
LRU_WIDTH = 1280
LRU_BLOCKS = 16
LRU_BW = LRU_WIDTH // LRU_BLOCKS
LRU_CONV = 4
LRU_C = 8.0
N_EXPERTS = 16
N_GROUPS = 4
EXPERTS_PER_GROUP = N_EXPERTS // N_GROUPS
TOP_K = 2
D_FF = 1024
DN_ALPHA = (2.0 * DEPTH) ** 0.25
LN_EPS = 1e-5
NEG_INF = -1e30
LOG2_E = 1.4426950408889634

LANES = 128
SUBLANES = 8
VMEM_LIMIT_BYTES = 56 * 1024 * 1024
ROW_TILE = 512
LRU_ROW_TILE = 256
ATT_TQ = 512
ATT_TK = 256
MOE_TOKENS = 2048
MOE_CHUNK = 128
MOE_ALIGN = SUBLANES


def _cparams(sem):
    return pltpu.CompilerParams(dimension_semantics=sem, vmem_limit_bytes=VMEM_LIMIT_BYTES)


def _dot(a, b):
    return jnp.dot(a, b, preferred_element_type=F32)


def _dot_nt(a, b, precision=None):
    return lax.dot_general(a, b, (((1,), (1,)), ((), ())), preferred_element_type=F32,
                           precision=precision)


def _layer_norm_rows(y, g, b):
    mu = jnp.mean(y, axis=-1, keepdims=True)
    d = y - mu
    var = jnp.mean(d * d, axis=-1, keepdims=True)
    return d * lax.rsqrt(var + LN_EPS) * g + b


def _const_spec(shape):
    nd = len(shape)
    return pl.BlockSpec(shape, lambda *_: (0,) * nd)


def _rope_tables_kernel(pos_ref, inv_ref, cos_ref, sina_ref, sinb_ref):
    ang = pos_ref[...] * inv_ref[...]
    c = jnp.cos(ang)
    s = jnp.sin(ang)
    lane = lax.broadcasted_iota(jnp.int32, ang.shape, 1)
    first = (lane % HEAD_DIM) < (HEAD_DIM // 2)
    cos_ref[...] = c
    sina_ref[...] = jnp.where(first, -s, 0.0)
    sinb_ref[...] = jnp.where(first, 0.0, s)


def _rope_tables(pos_flat):
    n = pos_flat.shape[0]
    tm = min(n, 1024)
    half = HEAD_DIM // 2
    inv = ROPE_THETA ** (-jnp.arange(half, dtype=F32) / half)
    inv_lane = jnp.tile(inv, LANES // half)[None, :]
    posb = jnp.broadcast_to(pos_flat.astype(F32)[:, None], (n, LANES))
    spec = pl.BlockSpec((tm, LANES), lambda i: (i, 0))
    out = jax.ShapeDtypeStruct((n, LANES), F32)
    return pl.pallas_call(
        _rope_tables_kernel,
        grid=(n // tm,),
        in_specs=[spec, _const_spec((1, LANES))],
        out_specs=[spec, spec, spec],
        out_shape=[out, out, out],
        compiler_params=_cparams(("arbitrary",)),
        name="rope_tables",
    )(posb, inv_lane)


def _rope128(v, cos, sina, sinb):
    half = HEAD_DIM // 2
    return (v * cos + pltpu.roll(v, LANES - half, 1) * sina + pltpu.roll(v, half, 1) * sinb)


_NSA_SEG = {
    "q": 0, "ks": 1024, "kw": 1280, "kc": 1536, "vc": 1792, "vs": 2048, "vw": 2304, "gate": 2560,
}
_NSA_GATE_COLS = NSA_GROUPS * LANES
_NSA_PROJ_COLS = 2560 + _NSA_GATE_COLS


def _nsa_proj_kernel(x_ref, w_ref, bg_ref, cos_ref, sina_ref, sinb_ref,
                     q_ref, ks_ref, kw_ref, kc_ref, vc_ref, vs_ref, vw_ref, g_ref):
    xb = x_ref[...].astype(MXU_DTYPE)
    cos = cos_ref[...]
    sina = sina_ref[...]
    sinb = sinb_ref[...]
    scale = HEAD_DIM ** -0.5 * LOG2_E

    def seg(name, width=256, c=0):
        o = _NSA_SEG[name] + c * 256
        return _dot(xb, w_ref[:, o:o + width])

    for c in range(NSA_Q_COLS // 256):
        acc = seg("q", c=c)
        for h in range(2):
            r = _rope128(acc[:, h * LANES:(h + 1) * LANES], cos, sina, sinb) * scale
            q_ref[:, c * 256 + h * LANES:c * 256 + (h + 1) * LANES] = r.astype(q_ref.dtype)
    for name, ref in (("ks", ks_ref), ("kw", kw_ref)):
        acc = seg(name)
        for h in range(2):
            r = _rope128(acc[:, h * LANES:(h + 1) * LANES], cos, sina, sinb)
            ref[:, h * LANES:(h + 1) * LANES] = r.astype(ref.dtype)
    kc_ref[...] = seg("kc")
    vc_ref[...] = seg("vc")
    vs_ref[...] = seg("vs").astype(vs_ref.dtype)
    vw_ref[...] = seg("vw").astype(vw_ref.dtype)
    g_ref[...] = jax.nn.sigmoid(seg("gate", width=_NSA_GATE_COLS) + bg_ref[...])


def _nsa_proj(x2, w_perm, bg_perm, cos_t, sina_t, sinb_t):
    t = x2.shape[0]
    tm = ROW_TILE
    row = lambda w: pl.BlockSpec((tm, w), lambda i: (i, 0))
    outs = [
        jax.ShapeDtypeStruct((t, NSA_Q_COLS), MXU_DTYPE),
        jax.ShapeDtypeStruct((t, NSA_KV_COLS), MXU_DTYPE),
        jax.ShapeDtypeStruct((t, NSA_KV_COLS), MXU_DTYPE),
        jax.ShapeDtypeStruct((t, NSA_KV_COLS), F32),
        jax.ShapeDtypeStruct((t, NSA_KV_COLS), F32),
        jax.ShapeDtypeStruct((t, NSA_KV_COLS), MXU_DTYPE),
        jax.ShapeDtypeStruct((t, NSA_KV_COLS), MXU_DTYPE),
        jax.ShapeDtypeStruct((t, _NSA_GATE_COLS), F32),
    ]
    return pl.pallas_call(
        _nsa_proj_kernel,
        grid=(t // tm,),
        in_specs=[row(D_MODEL), _const_spec((D_MODEL, _NSA_PROJ_COLS)),
                  _const_spec((1, _NSA_GATE_COLS)), row(LANES), row(LANES), row(LANES)],
        out_specs=[row(NSA_Q_COLS)] + [row(NSA_KV_COLS)] * 6 + [row(_NSA_GATE_COLS)],
        out_shape=outs,
        compiler_params=_cparams(("arbitrary",)),
        name="nsa_proj",
    )(x2, w_perm, bg_perm, cos_t, sina_t, sinb_t)


def _nsa_compress_kernel(kc_ref, vc_ref, pek_ref, pev_ref, w1k_ref, w1v_ref, w2k_ref, w2v_ref,
                         cos_ref, sina_ref, sinb_ref, ko_ref, vo_ref):
    nc = kc_ref.shape[1]
    half_feat = kc_ref.shape[2]
    rows = lax.broadcasted_iota(jnp.int32, (nc, 1), 0)
    valid = rows < nc - 1

    def compress(tok, pe_ref, w1_ref, w2_ref):
        a = (tok + pe_ref[:, :half_feat]).astype(MXU_DTYPE)
        b = (tok + pe_ref[:, half_feat:]).astype(MXU_DTYPE)
        pa = _dot(a, w1_ref[:half_feat, :])
        pb = _dot(b, w1_ref[half_feat:, :])
        pre = pa + pltpu.roll(pb, nc - 1, 0)
        h = jax.nn.gelu(pre)
        return _dot(h.astype(MXU_DTYPE), w2_ref[...])

    kcmp = compress(kc_ref[0], pek_ref, w1k_ref, w2k_ref)
    vcmp = compress(vc_ref[0], pev_ref, w1v_ref, w2v_ref)
    cos = cos_ref[0]
    sina = sina_ref[0]
    sinb = sinb_ref[0]
    for h in range(NSA_KV_COLS // LANES):
        r = _rope128(kcmp[:, h * LANES:(h + 1) * LANES], cos, sina, sinb)
        ko_ref[0, :, h * LANES:(h + 1) * LANES] = jnp.where(valid, r, 0.0).astype(ko_ref.dtype)
    vo_ref[0] = jnp.where(valid, vcmp, 0.0).astype(vo_ref.dtype)


def _nsa_compress(kc3, vc3, pek, pev, w1k, w1v, w2k, w2v, cos_c, sina_c, sinb_c):
    b, nc, feat = kc3.shape
    tok = pl.BlockSpec((1, nc, feat), lambda i: (i, 0, 0))
    tab = pl.BlockSpec((1, nc, LANES), lambda i: (i, 0, 0))
    out = pl.BlockSpec((1, nc, NSA_KV_COLS), lambda i: (i, 0, 0))
    oshape = jax.ShapeDtypeStruct((b, nc, NSA_KV_COLS), MXU_DTYPE)
    return pl.pallas_call(
        _nsa_compress_kernel,
        grid=(b,),
        in_specs=[tok, tok, _const_spec(pek.shape), _const_spec(pev.shape),
                  _const_spec(w1k.shape), _const_spec(w1v.shape),
                  _const_spec(w2k.shape), _const_spec(w2v.shape), tab, tab, tab],
        out_specs=[out, out],
        out_shape=[oshape, oshape],
        compiler_params=_cparams(("arbitrary",)),
        name="nsa_compress",
    )(kc3, vc3, pek, pev, w1k, w1v, w2k, w2v, cos_c, sina_c, sinb_c)


def _nsa_attn_kernel(q_ref, ks_ref, vs_ref, kw_ref, vw_ref, kc_ref, vc_ref, g_ref, ov_ref, et_ref,
                     o_ref, m_ref, acc_ref, sel_ref, *, n_sel, k_sel):
    tq = q_ref.shape[1]
    tk = ATT_TK
    nc = kc_ref.shape[1]
    R = NSA_HPG
    qi = pl.program_id(1)
    g = pl.program_id(2)
    t0 = qi * tq
    trow = t0 + lax.broadcasted_iota(jnp.int32, (tq, 1), 0)
    gmask = lax.broadcasted_iota(jnp.int32, (1, NSA_KV_COLS), 1) // HEAD_DIM == g
    gates = g_ref[0]
    qs = jnp.concatenate(
        [jnp.where(gmask, q_ref[0, :, r * NSA_KV_COLS:(r + 1) * NSA_KV_COLS], 0)
         for r in range(R)], axis=0)
    rows = lambda a, r: a[r * tq:(r + 1) * tq]

    def tile_update(c, k_ref, v_ref, k0, bias):
        kt = k_ref[0, pl.ds(k0, tk), :]
        vt = jnp.where(gmask, v_ref[0, pl.ds(k0, tk), :], 1)
        s_all = _dot_nt(qs, kt)
        ps, alphas = [], []
        for r in range(R):
            s = rows(s_all, r) + bias
            m_old = m_ref[c, r]
            m_new = jnp.maximum(m_old, jnp.max(s, axis=-1, keepdims=True))
            m_ref[c, r] = m_new
            alphas.append(jnp.exp2(m_old - m_new))
            m_wide = jnp.concatenate([m_new] * (tk // LANES), axis=1)
            ps.append(jnp.exp2(s - m_wide).astype(MXU_DTYPE))
        pv = _dot(jnp.concatenate(ps, axis=0), vt)
        for r in range(R):
            a_wide = jnp.concatenate([alphas[r]] * (NSA_KV_COLS // LANES), axis=1)
            acc_ref[c, r] = a_wide * acc_ref[c, r] + rows(pv, r)

    n_idx = lax.broadcasted_iota(jnp.int32, (1, nc), 1)
    cmp_ok = (n_idx * CMP_STRIDE + CMP_BLOCK - 1) <= trow
    s_all = _dot_nt(qs, kc_ref[0])
    psum = jnp.zeros((tq, nc), F32)
    ps = []
    for r in range(R):
        s = jnp.where(cmp_ok, rows(s_all, r), NEG_INF)
        e = jnp.where(cmp_ok, jnp.exp2(s - jnp.max(s, axis=-1, keepdims=True)), 0.0)
        p = e * (1.0 / jnp.maximum(jnp.sum(e, axis=-1, keepdims=True), 1e-30))
        psum = psum + p
        ps.append(p.astype(MXU_DTYPE))
    o_cmp = _dot(jnp.concatenate(ps, axis=0), vc_ref[0])

    all_selected = t0 + tq <= k_sel * SEL_BLOCK

    @pl.when(all_selected)
    def _():
        sel_ref[...] = jnp.ones(sel_ref.shape, sel_ref.dtype)

    @pl.when(jnp.logical_not(all_selected))
    def _():
        imp = jnp.dot(psum, ov_ref[...], preferred_element_type=F32,
                      precision=lax.Precision.HIGHEST)
        blk = lax.broadcasted_iota(jnp.int32, (1, LANES), 1)
        rel = trow // SEL_BLOCK - blk
        imp = jnp.where(rel >= 0, imp, -1.0)
        forced = (blk == 0) | (rel == 0) | (rel == 1)
        imp = jnp.where(forced, FORCE_SCORE, imp)
        imp_t = jnp.transpose(imp)[:n_sel, :]
        midx = lax.broadcasted_iota(jnp.int32, (n_sel, 1), 0)
        cnt = jnp.zeros((n_sel, tq), F32)
        for mp in range(n_sel):
            row = imp_t[mp:mp + 1, :]
            beats = (row > imp_t) | ((row == imp_t) & (midx > mp))
            cnt = cnt + jnp.where(beats, 1.0, 0.0)
        sel_t = jnp.where(cnt < k_sel, 1.0, 0.0)
        if n_sel < LANES:
            sel_t = jnp.concatenate([sel_t, jnp.zeros((LANES - n_sel, tq), F32)], axis=0)
        sel_ref[...] = jnp.transpose(sel_t).astype(sel_ref.dtype)

    sel = sel_ref[...]

    SEL, WIN = 0, 1
    m_ref[...] = jnp.full(m_ref.shape, NEG_INF, F32)
    acc_ref[...] = jnp.zeros(acc_ref.shape, F32)

    def sel_update(k0, kpos):
        selm = _dot_nt(sel, et_ref[pl.ds(k0, tk), :])
        bias = jnp.where((selm > 0.5) & (kpos <= trow), 0.0, NEG_INF)
        tile_update(SEL, ks_ref, vs_ref, k0, bias)

    def win_update(k0, kpos):
        d = trow - kpos
        bias = jnp.where((d >= 0) & (d < WINDOW), 0.0, NEG_INF)
        tile_update(WIN, kw_ref, vw_ref, k0, bias)

    def key_tile(j):
        k0 = pl.multiple_of(j * tk, tk)
        return k0, k0 + lax.broadcasted_iota(jnp.int32, (1, tk), 1)

    def sel_only(j, carry):
        sel_update(*key_tile(j))
        return carry

    def sel_and_win(j, carry):
        k0, kpos = key_tile(j)
        sel_update(k0, kpos)
        win_update(k0, kpos)
        return carry

    j_win = jnp.maximum(t0 - WINDOW, 0) // tk
    lax.fori_loop(0, j_win, sel_only, 0)
    lax.fori_loop(j_win, (t0 + tq) // tk, sel_and_win, 0)

    def normalized(c, r):
        den = jnp.where(g < NSA_GROUPS // 2, acc_ref[c, r, :, LANES:], acc_ref[c, r, :, :LANES])
        inv = 1.0 / den
        return acc_ref[c, r] * jnp.concatenate([inv, inv], axis=1)

    o_heads = [gates[:, r:r + 1] * rows(o_cmp, r)
               + gates[:, R + r:R + r + 1] * normalized(SEL, r)
               + gates[:, 2 * R + r:2 * R + r + 1] * normalized(WIN, r) for r in range(R)]

    for gg in range(NSA_GROUPS):
        @pl.when(g == gg)
        def _(gg=gg):
            for r in range(R):
                lo = r * NSA_KV_COLS + gg * HEAD_DIM
                o_ref[0, :, lo:lo + HEAD_DIM] = (
                    o_heads[r][:, gg * HEAD_DIM:(gg + 1) * HEAD_DIM].astype(o_ref.dtype))


def _nsa_attn(q, ks, vs, kw, vw, kcmp, vcmp, gates, ov, et):
    b, s, _ = q.shape
    nc = kcmp.shape[1]
    tq = ATT_TQ
    n_sel = s // SEL_BLOCK
    k_sel = min(SEL_TOP, n_sel)
    qspec = pl.BlockSpec((1, tq, NSA_Q_COLS), lambda i, j, g: (i, j, 0))
    kv = pl.BlockSpec((1, s, NSA_KV_COLS), lambda i, j, g: (i, 0, 0))
    cmp_spec = pl.BlockSpec((1, nc, NSA_KV_COLS), lambda i, j, g: (i, 0, 0))
    gspec = pl.BlockSpec((1, tq, LANES), lambda i, j, g: (i, j, g))
    return pl.pallas_call(
        functools.partial(_nsa_attn_kernel, n_sel=n_sel, k_sel=k_sel),
        grid=(b, s // tq, NSA_GROUPS),
        in_specs=[qspec, kv, kv, kv, kv, cmp_spec, cmp_spec, gspec,
                  _const_spec(ov.shape), _const_spec(et.shape)],
        out_specs=qspec,
        out_shape=jax.ShapeDtypeStruct((b, s, NSA_Q_COLS), MXU_DTYPE),
        scratch_shapes=[pltpu.VMEM((2, NSA_HPG, tq, LANES), F32),
                        pltpu.VMEM((2, NSA_HPG, tq, NSA_KV_COLS), F32),
                        pltpu.VMEM((tq, LANES), MXU_DTYPE)],
        compiler_params=_cparams(("arbitrary", "arbitrary", "arbitrary")),
        name="nsa_attn",
    )(q, ks, vs, kw, vw, kcmp, vcmp, gates, ov, et)


def _mm_res_ln_kernel(a_ref, w_ref, x_ref, g_ref, b_ref, o_ref):
    m = _dot(a_ref[...].astype(MXU_DTYPE), w_ref[...])
    o_ref[...] = _layer_norm_rows(DN_ALPHA * x_ref[...] + m, g_ref[...], b_ref[...])


def _mm_res_ln(a, w, x2, g, b):
    t, k = a.shape
    tm = ROW_TILE
    return pl.pallas_call(
        _mm_res_ln_kernel,
        grid=(t // tm,),
        in_specs=[pl.BlockSpec((tm, k), lambda i: (i, 0)), _const_spec(w.shape),
                  pl.BlockSpec((tm, D_MODEL), lambda i: (i, 0)),
                  _const_spec((1, D_MODEL)), _const_spec((1, D_MODEL))],
        out_specs=pl.BlockSpec((tm, D_MODEL), lambda i: (i, 0)),
        out_shape=jax.ShapeDtypeStruct((t, D_MODEL), F32),
        compiler_params=_cparams(("arbitrary",)),
        name="mm_res_ln",
    )(a, w, x2, g, b)


def _shift_rows(v, prev, k):
    out = pltpu.roll(v, k, 0)
    rows = lax.broadcasted_iota(jnp.int32, (v.shape[0], 1), 0)
    np_rows = prev.shape[0]
    for i in range(k):
        out = jnp.where(rows == i, prev[np_rows - k + i:np_rows - k + i + 1, :], out)
    return out


def _shortconv_kernel(x_ref, win_ref, cw_ref, cb_ref, wout_ref, g_ref, b_ref, o_ref, carry_ref):
    @pl.when(pl.program_id(1) == 0)
    def _():
        carry_ref[...] = jnp.zeros(carry_ref.shape, F32)

    x = x_ref[0]
    xb = x.astype(MXU_DTYPE)
    d = D_MODEL
    u = _dot(xb, win_ref[:, 0:d])
    bg = _dot(xb, win_ref[:, d:2 * d])
    cg = _dot(xb, win_ref[:, 2 * d:3 * d])
    v = cg * u
    prev = carry_ref[...]
    z = cb_ref[...] + cw_ref[SC_WIDTH - 1:SC_WIDTH, :] * v
    for k in range(1, SC_WIDTH):
        z = z + cw_ref[SC_WIDTH - 1 - k:SC_WIDTH - k, :] * _shift_rows(v, prev, k)
    carry_ref[...] = v[v.shape[0] - SUBLANES:, :]
    m = _dot((bg * z).astype(MXU_DTYPE), wout_ref[...])
    o_ref[0] = _layer_norm_rows(DN_ALPHA * x + m, g_ref[...], b_ref[...])


def _shortconv_layer(x3, win, cw, cb, wout, g, b):
    bsz, s, d = x3.shape
    tm = ROW_TILE
    xspec = pl.BlockSpec((1, tm, d), lambda i, j: (i, j, 0))
    return pl.pallas_call(
        _shortconv_kernel,
        grid=(bsz, s // tm),
        in_specs=[xspec, _const_spec(win.shape), _const_spec(cw.shape), _const_spec(cb.shape),
                  _const_spec(wout.shape), _const_spec(g.shape), _const_spec(b.shape)],
        out_specs=xspec,
        out_shape=jax.ShapeDtypeStruct(x3.shape, F32),
        scratch_shapes=[pltpu.VMEM((SUBLANES, d), F32)],
        compiler_params=_cparams(("arbitrary", "arbitrary")),
        name="shortconv_layer",
    )(x3, win, cw, cb, wout, g, b)


def _round_down(x, m):
    return x // m * m


def _round_up(x, m):
    return -(-x // m) * m


def _block_diag_dot(u, w_ref):
    tile = 2 * LANES
    outs = []
    for c0 in range(0, LRU_WIDTH, tile):
        c1 = min(c0 + tile, LRU_WIDTH)
        lo = _round_down(_round_down(c0, LRU_BW), LANES)
        hi = min(_round_up(_round_up(c1, LRU_BW), LANES), LRU_WIDTH)
        outs.append(_dot(u[:, lo:hi], w_ref[lo:hi, c0:c1]))
    return jnp.concatenate(outs, axis=1)


def _rglru_kernel(x_ref, win_ref, cw_ref, cb_ref, wa_ref, ba_ref, wx_ref, bx_ref, lam_ref,
                  wout_ref, g_ref, b_ref, o_ref, carry_ref, h_ref, a_s, b_s):
    @pl.when(pl.program_id(1) == 0)
    def _():
        carry_ref[...] = jnp.zeros(carry_ref.shape, F32)
        h_ref[...] = jnp.zeros(h_ref.shape, F32)

    x = x_ref[0]
    tm = x.shape[0]
    xb = x.astype(MXU_DTYPE)
    w = LRU_WIDTH
    gate_br = _dot(xb, win_ref[:, 0:w])
    rec = _dot(xb, win_ref[:, w:2 * w])
    prev = carry_ref[...]
    u = cb_ref[...] + cw_ref[LRU_CONV - 1:LRU_CONV, :] * rec
    for k in range(1, LRU_CONV):
        u = u + cw_ref[LRU_CONV - 1 - k:LRU_CONV - k, :] * _shift_rows(rec, prev, k)
    carry_ref[...] = rec[tm - SUBLANES:, :]
    ub = u.astype(MXU_DTYPE)
    r = jax.nn.sigmoid(_block_diag_dot(ub, wa_ref) + ba_ref[...])
    ig = jax.nn.sigmoid(_block_diag_dot(ub, wx_ref) + bx_ref[...])
    log_a = LRU_C * r * jax.nn.log_sigmoid(lam_ref[...])
    a_s[...] = jnp.exp(log_a)
    th = jnp.tanh(log_a)
    b_s[...] = jnp.sqrt(-2.0 * th / (1.0 - th)) * (ig * u)

    rows = lax.broadcasted_iota(jnp.int32, (SUBLANES, 1), 0)

    def group(i, hc):
        r0 = pl.multiple_of(i * SUBLANES, SUBLANES)
        a = a_s[pl.ds(r0, SUBLANES), :]
        bv = b_s[pl.ds(r0, SUBLANES), :]
        for dlt in (1, 2, 4):
            a_sh = jnp.where(rows >= dlt, pltpu.roll(a, dlt, 0), 1.0)
            b_sh = jnp.where(rows >= dlt, pltpu.roll(bv, dlt, 0), 0.0)
            bv = a * b_sh + bv
            a = a * a_sh
        h = a * hc + bv
        b_s[pl.ds(r0, SUBLANES), :] = h
        return h[SUBLANES - 1:SUBLANES, :]

    h_ref[...] = lax.fori_loop(0, tm // SUBLANES, group, h_ref[...])
    y = jax.nn.gelu(gate_br) * b_s[...]
    m = _dot(y.astype(MXU_DTYPE), wout_ref[...])
    o_ref[0] = _layer_norm_rows(DN_ALPHA * x + m, g_ref[...], b_ref[...])


def _rglru_layer(x3, win, cw, cb, wa, ba, wx, bx, lam, wout, g, b):
    bsz, s, d = x3.shape
    tm = LRU_ROW_TILE
    xspec = pl.BlockSpec((1, tm, d), lambda i, j: (i, j, 0))
    consts = [win, cw, cb, wa, ba, wx, bx, lam, wout, g, b]
    return pl.pallas_call(
        _rglru_kernel,
        grid=(bsz, s // tm),
        in_specs=[xspec] + [_const_spec(c.shape) for c in consts],
        out_specs=xspec,
        out_shape=jax.ShapeDtypeStruct(x3.shape, F32),
        scratch_shapes=[pltpu.VMEM((SUBLANES, LRU_WIDTH), F32), pltpu.VMEM((1, LRU_WIDTH), F32),
                        pltpu.VMEM((tm, LRU_WIDTH), F32), pltpu.VMEM((tm, LRU_WIDTH), F32)],
        compiler_params=_cparams(("arbitrary", "arbitrary")),
        name="rglru_layer",
    )(x3, win, cw, cb, wa, ba, wx, bx, lam, wout, g, b)


def _moe_router_kernel(x_ref, rwt_ref, rb_ref, tri_ref, pos_ref, wts_ref, meta_ref):
    tt = x_ref.shape[0]
    x = x_ref[...]
    w = rwt_ref[...]
    xh = x.astype(MXU_DTYPE)
    wh = w.astype(MXU_DTYPE)
    xl = (x - xh.astype(F32)).astype(MXU_DTYPE)
    wl = (w - wh.astype(F32)).astype(MXU_DTYPE)
    logits = _dot_nt(wh, xh) + _dot_nt(wh, xl) + _dot_nt(wl, xh)
    scores = jax.nn.sigmoid(logits)
    biased = scores + rb_ref[...]
    rows = [biased[e:e + 1, :] for e in range(N_EXPERTS)]

    gscore = []
    for gi in range(N_GROUPS):
        a, b, c, d = rows[gi * EXPERTS_PER_GROUP:(gi + 1) * EXPERTS_PER_GROUP]
        hi1, lo1 = jnp.maximum(a, b), jnp.minimum(a, b)
        hi2, lo2 = jnp.maximum(c, d), jnp.minimum(c, d)
        gscore.append(jnp.maximum(hi1, hi2) + jnp.maximum(jnp.minimum(hi1, hi2),
                                                          jnp.maximum(lo1, lo2)))
    best = gscore[0]
    top_g = jnp.zeros((1, tt), jnp.int32)
    for gi in range(1, N_GROUPS):
        better = gscore[gi] > best
        best = jnp.where(better, gscore[gi], best)
        top_g = jnp.where(better, gi, top_g)

    def argmax_in_group(exclude):
        bv = jnp.full((1, tt), -jnp.inf, F32)
        bi = jnp.full((1, tt), -1, jnp.int32)
        for e in range(N_EXPERTS):
            ok = top_g == (e // EXPERTS_PER_GROUP)
            if exclude is not None:
                ok = ok & (exclude != e)
            better = ok & (rows[e] > bv)
            bv = jnp.where(better, rows[e], bv)
            bi = jnp.where(better, e, bi)
        return bi

    e1 = argmax_in_group(None)
    e2 = argmax_in_group(e1)
    eidx = lax.broadcasted_iota(jnp.int32, (N_EXPERTS, 1), 0)
    oh1 = eidx == e1
    oh2 = eidx == e2
    s1 = jnp.sum(jnp.where(oh1, scores, 0.0), axis=0, keepdims=True)
    s2 = jnp.sum(jnp.where(oh2, scores, 0.0), axis=0, keepdims=True)
    wsum = s1 + s2
    wts_ref[0, 0:1, :] = s1 / wsum
    wts_ref[0, 1:2, :] = s2 / wsum

    oh = jnp.where(oh1 | oh2, 1.0, 0.0)
    csum = _dot(oh.astype(MXU_DTYPE), tri_ref[...])
    excl = csum - oh
    cnt = csum[:, tt - 1:tt]
    padded = jnp.ceil(cnt * (1.0 / MOE_ALIGN)) * MOE_ALIGN
    run = jnp.zeros((1, 1), F32)
    off_rows = []
    for e in range(N_EXPERTS):
        off_rows.append(run)
        run = run + padded[e:e + 1, :]
    off = jnp.concatenate(off_rows, axis=0)
    slot = off + excl
    pos_ref[0, 0:1, :] = jnp.sum(jnp.where(oh1, slot, 0.0), axis=0, keepdims=True).astype(jnp.int32)
    pos_ref[0, 1:2, :] = jnp.sum(jnp.where(oh2, slot, 0.0), axis=0, keepdims=True).astype(jnp.int32)
    lane = lax.broadcasted_iota(jnp.int32, (1, LANES), 1)
    meta = jnp.zeros((1, LANES), F32)
    for e in range(N_EXPERTS):
        meta = jnp.where(lane == e, off[e:e + 1, :], meta)
        meta = jnp.where(lane == N_EXPERTS + e, cnt[e:e + 1, :], meta)
    meta_ref[0] = meta.astype(jnp.int32)


def _moe_router(x2, rwt, rb, tri):
    t = x2.shape[0]
    tt = tri.shape[0]
    nt = t // tt
    return pl.pallas_call(
        _moe_router_kernel,
        grid=(nt,),
        in_specs=[pl.BlockSpec((tt, D_MODEL), lambda i: (i, 0)), _const_spec(rwt.shape),
                  _const_spec(rb.shape), _const_spec(tri.shape)],
        out_specs=[pl.BlockSpec((1, TOP_K, tt), lambda i: (i, 0, 0)),
                   pl.BlockSpec((1, TOP_K, tt), lambda i: (i, 0, 0)),
                   pl.BlockSpec((1, 1, LANES), lambda i: (i, 0, 0))],
        out_shape=[jax.ShapeDtypeStruct((nt, TOP_K, tt), jnp.int32),
                   jax.ShapeDtypeStruct((nt, TOP_K, tt), F32),
                   jax.ShapeDtypeStruct((nt, 1, LANES), jnp.int32)],
        compiler_params=_cparams(("arbitrary",)),
        name="moe_router",
    )(x2, rwt, rb, tri)


def _moe_ffn_kernel(pos_ref, wts_ref, meta_ref, x_ref, wg_ref, wu_ref, wd_ref, g_ref, b_ref,
                    o_ref, xs_ref):
    tt = x_ref.shape[0]
    ch = MOE_CHUNK
    e = pl.program_id(1)

    def token_group(body):
        def group(i, carry):
            base = pl.multiple_of(i * SUBLANES, SUBLANES)
            for k in range(SUBLANES):
                body(base + k)
            return carry

        lax.fori_loop(0, tt // SUBLANES, group, 0)

    @pl.when(e == 0)
    def _dispatch():
        def zero_gap(ei, carry):
            start = meta_ref[ei] + meta_ref[N_EXPERTS + ei]
            stop = jnp.where(ei == N_EXPERTS - 1, start + ch + MOE_ALIGN,
                             meta_ref[jnp.minimum(ei + 1, N_EXPERTS - 1)])

            def zrow(r, c):
                xs_ref[pl.ds(r, 1), :] = jnp.zeros((1, D_MODEL), F32)
                return c

            return lax.fori_loop(start, stop, zrow, carry)

        lax.fori_loop(0, N_EXPERTS, zero_gap, 0)

        def scatter(t):
            row = x_ref[pl.ds(t, 1), :]
            xs_ref[pl.ds(pos_ref[t], 1), :] = row
            xs_ref[pl.ds(pos_ref[tt + t], 1), :] = row

        token_group(scatter)

    off = meta_ref[e]
    cnt = meta_ref[N_EXPERTS + e]
    rid = lax.broadcasted_iota(jnp.int32, (ch, 1), 0)

    def chunk(c, carry):
        r0 = pl.multiple_of(off + c * ch, MOE_ALIGN)
        xs = xs_ref[pl.ds(r0, ch), :]
        xb = xs.astype(MXU_DTYPE)
        hid = jax.nn.silu(_dot(xb, wg_ref[0])) * _dot(xb, wu_ref[0])
        out = _dot(hid.astype(MXU_DTYPE), wd_ref[0])
        xs_ref[pl.ds(r0, ch), :] = jnp.where(rid + c * ch < cnt, out, xs)
        return carry

    lax.fori_loop(0, (cnt + ch - 1) // ch, chunk, 0)

    @pl.when(e == N_EXPERTS - 1)
    def _combine():
        def gather(t):
            o_ref[pl.ds(t, 1), :] = (wts_ref[t] * xs_ref[pl.ds(pos_ref[t], 1), :]
                                     + wts_ref[tt + t] * xs_ref[pl.ds(pos_ref[tt + t], 1), :])

        token_group(gather)

        def norm(i, carry):
            r0 = pl.multiple_of(i * ch, ch)
            y = DN_ALPHA * x_ref[pl.ds(r0, ch), :] + o_ref[pl.ds(r0, ch), :]
            o_ref[pl.ds(r0, ch), :] = _layer_norm_rows(y, g_ref[...], b_ref[...])
            return carry

        lax.fori_loop(0, tt // ch, norm, 0)


def _moe_ffn(pos, wts, meta, x2, wg, wu, wd, g, b):
    t = x2.shape[0]
    tt = pos.shape[2]
    nt = t // tt
    smem = lambda n: pl.BlockSpec((n,), lambda i, e: (i,), memory_space=pltpu.SMEM)
    wspec = lambda w: pl.BlockSpec((1,) + w.shape[1:], lambda i, e: (e, 0, 0))
    xs_rows = TOP_K * tt + N_EXPERTS * MOE_ALIGN + MOE_CHUNK + MOE_ALIGN
    pos, wts, meta = pos.reshape(-1), wts.reshape(-1), meta.reshape(-1)
    return pl.pallas_call(
        _moe_ffn_kernel,
        grid=(nt, N_EXPERTS),
        in_specs=[smem(TOP_K * tt), smem(TOP_K * tt), smem(LANES),
                  pl.BlockSpec((tt, D_MODEL), lambda i, e: (i, 0), pipeline_mode=pl.Buffered(1)),
                  wspec(wg), wspec(wu), wspec(wd),
                  pl.BlockSpec((1, D_MODEL), lambda i, e: (0, 0)),
                  pl.BlockSpec((1, D_MODEL), lambda i, e: (0, 0))],
        out_specs=pl.BlockSpec((tt, D_MODEL), lambda i, e: (i, 0), pipeline_mode=pl.Buffered(1)),
        out_shape=jax.ShapeDtypeStruct((t, D_MODEL), F32),
        scratch_shapes=[pltpu.VMEM((xs_rows, D_MODEL), F32)],
        compiler_params=_cparams(("arbitrary", "arbitrary")),
        name="moe_ffn",
    )(pos, wts, meta, x2, wg, wu, wd, g, b)


def _nsa_weight_layout(w_in, b_gate):
    G, R, Dh = NSA_GROUPS, NSA_HPG, HEAD_DIM
    d_in = w_in.shape[0]
    kvw = NSA_KV_COLS
    wq = w_in[:, :NSA_Q_COLS].reshape(d_in, G, R, Dh).transpose(0, 2, 1, 3).reshape(d_in, NSA_Q_COLS)
    kc, vc, ks, vs, kw, vw = (w_in[:, NSA_Q_COLS + i * kvw:NSA_Q_COLS + (i + 1) * kvw] for i in range(6))
    gate_base = NSA_Q_COLS + 6 * kvw

    def gate_layout(v):
        lead = v.shape[:-1]
        v = jnp.swapaxes(v.reshape(lead + (G, R, 3)), -1, -2).reshape(lead + (G, 3 * R))
        v = jnp.pad(v, [(0, 0)] * (len(lead) + 1) + [(0, LANES - 3 * R)])
        return v.reshape(lead + (_NSA_GATE_COLS,))

    wg = gate_layout(w_in[:, gate_base:gate_base + 3 * NSA_HEADS])
    w = jnp.concatenate([wq, ks, kw, kc, vc, vs, vw, wg], axis=1).astype(MXU_DTYPE)
    return w, gate_layout(b_gate)[None, :]


def _nsa_out_layout(w_out):
    G, R, Dh = NSA_GROUPS, NSA_HPG, HEAD_DIM
    d_out = w_out.shape[1]
    w = w_out.reshape(G, R, Dh, d_out).transpose(1, 0, 2, 3).reshape(NSA_Q_COLS, d_out)
    return w.astype(MXU_DTYPE)


def _compress_weight_layout(pe, w1, w2):
    G, Dh, L = NSA_GROUPS, HEAD_DIM, CMP_BLOCK
    eye = jnp.eye(G, dtype=w1.dtype)
    w1r = w1.reshape(L, Dh, Dh)
    w1big = jnp.einsum("ldj,gh->lgdhj", w1r, eye).reshape(L * G * Dh, G * Dh)
    w2big = jnp.einsum("dj,gh->gdhj", w2, eye).reshape(G * Dh, G * Dh)
    pebig = jnp.broadcast_to(pe[:, None, :], (L, G, Dh)).reshape(1, L * G * Dh)
    return pebig, w1big.astype(MXU_DTYPE), w2big.astype(MXU_DTYPE)


def _block_diag(w):
    h, bi, bj = w.shape
    eye = jnp.eye(h, dtype=w.dtype)
    return jnp.einsum("hij,hk->hikj", w, eye).reshape(h * bi, h * bj).astype(MXU_DTYPE)


def _nsa_constants(s):
    nc = s // CMP_STRIDE
    n_cmp = (s - CMP_BLOCK) // CMP_STRIDE + 1
    n_sel = s // SEL_BLOCK
    cmp_start = np.arange(nc) * CMP_STRIDE
    cmp_end = cmp_start + CMP_BLOCK - 1
    sel_start = np.arange(n_sel) * SEL_BLOCK
    ov = ((cmp_start[:, None] <= sel_start[None, :] + SEL_BLOCK - 1)
          & (cmp_end[:, None] >= sel_start[None, :])
          & (np.arange(nc)[:, None] < n_cmp)).astype(np.float32)
    ov = np.pad(ov, ((0, 0), (0, LANES - n_sel)))
    et = (np.arange(s)[:, None] // SEL_BLOCK == np.arange(LANES)[None, :]).astype(np.float32)
    return jnp.asarray(ov), jnp.asarray(et, dtype=MXU_DTYPE)


def _nsa_layer(x2, bsz, s, tables, cmp_tables, w_in, b_gate, pos_k, w1_k, w2_k, pos_v, w1_v, w2_v,
               w_out, g, b):
    w_perm, bg_perm = _nsa_weight_layout(w_in, b_gate)
    q, ks, kw, kc, vc, vs, vw, gates = _nsa_proj(x2, w_perm, bg_perm, *tables)
    nc = s // CMP_STRIDE
    feat = CMP_STRIDE * NSA_KV_COLS
    pek, w1k, w2k = _compress_weight_layout(pos_k, w1_k, w2_k)
    pev, w1v, w2v = _compress_weight_layout(pos_v, w1_v, w2_v)
    kcmp, vcmp = _nsa_compress(kc.reshape(bsz, nc, feat), vc.reshape(bsz, nc, feat),
                               pek, pev, w1k, w1v, w2k, w2v, *cmp_tables)
    ov, et = _nsa_constants(s)
    r3 = lambda a: a.reshape(bsz, s, a.shape[-1])
    o = _nsa_attn(r3(q), r3(ks), r3(vs), r3(kw), r3(vw), kcmp, vcmp, r3(gates), ov, et)
    return _mm_res_ln(o.reshape(bsz * s, NSA_Q_COLS), _nsa_out_layout(w_out), x2, g, b)


def _moe_layer(x2, rwt, rb, tri, wg, wu, wd, g, b):
    pos, wts, meta = _moe_router(x2, rwt, rb, tri)
    return _moe_ffn(pos, wts, meta, x2, wg.astype(MXU_DTYPE), wu.astype(MXU_DTYPE),
                    wd.astype(MXU_DTYPE), g, b)


def kernel(x, positions, nsa_w_in, nsa_b_gate, nsa_cmp_pos_k, nsa_cmp_w1_k, nsa_cmp_w2_k, nsa_cmp_pos_v, nsa_cmp_w1_v, nsa_cmp_w2_v, nsa_w_out, sc_w_in, sc_conv_w, sc_conv_b, sc_w_out, lru_w_in, lru_conv_w, lru_conv_b, lru_wa, lru_ba, lru_wx, lru_bx, lru_lambda, lru_w_out, router_w, router_b, moe_w_gate, moe_w_up, moe_w_down, ln_g, ln_b):
    bsz, s, d = x.shape
    t = bsz * s
    x2 = x.reshape(t, d)
    row = lambda v: v.reshape(1, -1)

    tables = _rope_tables(positions.reshape(t))
    nc = s // CMP_STRIDE
    cmp_idx = np.minimum(np.arange(nc) * CMP_STRIDE + CMP_BLOCK - 1, s - 1)
    cmp_tables = tuple(tb.reshape(bsz, nc, LANES)
                       for tb in _rope_tables(positions[:, cmp_idx].reshape(bsz * nc)))

    tt = min(MOE_TOKENS, t)
    tri = jnp.asarray(np.triu(np.ones((tt, tt), np.float32)), dtype=MXU_DTYPE)
    rwt = router_w.T
    rb = router_b.reshape(N_EXPERTS, 1)

    for i in range(DEPTH):
        kind, j = i % N_MIXERS, i // N_MIXERS
        g0, b0 = row(ln_g[i, 0]), row(ln_b[i, 0])
        if kind == 0:
            x2 = _nsa_layer(x2, bsz, s, tables, cmp_tables, nsa_w_in[j], nsa_b_gate[j],
                            nsa_cmp_pos_k[j], nsa_cmp_w1_k[j], nsa_cmp_w2_k[j],
                            nsa_cmp_pos_v[j], nsa_cmp_w1_v[j], nsa_cmp_w2_v[j], nsa_w_out[j], g0, b0)
        elif kind == 1:
            x2 = _shortconv_layer(x2.reshape(bsz, s, d), sc_w_in[j].astype(MXU_DTYPE), sc_conv_w[j],
                                  row(sc_conv_b[j]), sc_w_out[j].astype(MXU_DTYPE), g0, b0).reshape(t, d)
        else:
            x2 = _rglru_layer(x2.reshape(bsz, s, d), lru_w_in[j].astype(MXU_DTYPE), lru_conv_w[j],
                              row(lru_conv_b[j]), _block_diag(lru_wa[j]), row(lru_ba[j]),
                              _block_diag(lru_wx[j]), row(lru_bx[j]), row(lru_lambda[j]),
                              lru_w_out[j].astype(MXU_DTYPE), g0, b0).reshape(t, d)
        x2 = _moe_layer(x2, rwt, rb, tri, moe_w_gate[i], moe_w_up[i], moe_w_down[i],
                        row(ln_g[i, 1]), row(ln_b[i, 1]))
    return x2.reshape(bsz, s, d)
```

```python
import functools

import numpy as np
import jax
import jax.numpy as jnp
from jax import lax
from jax.experimental import pallas as pl
from jax.experimental.pallas import tpu as pltpu

F32 = jnp.float32
BF16 = jnp.bfloat16
MXU_DTYPE = BF16

D_MODEL = 1024
DEPTH = 4
N_MIXERS = 3
NSA_HEADS = 16
NSA_GROUPS = 4
NSA_HPG = NSA_HEADS // NSA_GROUPS
HEAD_DIM = D_MODEL // NSA_HEADS
CMP_BLOCK = 32
CMP_STRIDE = 16
SEL_BLOCK = 64
SEL_TOP = 16
WINDOW = 512
ROPE_THETA = 10000.0
FORCE_SCORE = 1e6
NSA_Q_COLS = NSA_HEADS * HEAD_DIM
NSA_KV_COLS = NSA_GROUPS * HEAD_DIM
SC_WIDTH = 3
LRU_WIDTH = 1280
LRU_BLOCKS = 16
LRU_BW = LRU_WIDTH // LRU_BLOCKS
LRU_CONV = 4
LRU_C = 8.0
N_EXPERTS = 16
N_GROUPS = 4
EXPERTS_PER_GROUP = N_EXPERTS // N_GROUPS
TOP_K = 2
D_FF = 1024
DN_ALPHA = (2.0 * DEPTH) ** 0.25
LN_EPS = 1e-5
NEG_INF = -1e30
LOG2_E = 1.4426950408889634

LANES = 128
SUBLANES = 8
VMEM_LIMIT_BYTES = 56 * 1024 * 1024
ROW_TILE = 512
LRU_ROW_TILE = 256
ATT_TQ = 512
ATT_TK = 256
MOE_TOKENS = 2048
MOE_CHUNK = 288
MOE_NORM_ROWS = 128
MOE_ALIGN = SUBLANES
_SUBLANE_SHIFT = 3
_LANE_TILES = D_MODEL // LANES
_TILE_ROW = SUBLANES * _LANE_TILES


def _cparams(sem):
    return pltpu.CompilerParams(dimension_semantics=sem, vmem_limit_bytes=VMEM_LIMIT_BYTES)


def _dot(a, b):
    return jnp.dot(a, b, preferred_element_type=F32)


def _dot_nt(a, b, precision=None):
    return lax.dot_general(a, b, (((1,), (1,)), ((), ())), preferred_element_type=F32,
                           precision=precision)


def _layer_norm_rows(y, g, b):
    mu = jnp.mean(y, axis=-1, keepdims=True)
    d = y - mu
    var = jnp.mean(d * d, axis=-1, keepdims=True)
    return d * lax.rsqrt(var + LN_EPS) * g + b


def _const_spec(shape):
    nd = len(shape)
    return pl.BlockSpec(shape, lambda *_: (0,) * nd)


def _rope_tables_kernel(pos_ref, inv_ref, cos_ref, sina_ref, sinb_ref):
    ang = pos_ref[...] * inv_ref[...]
    c = jnp.cos(ang)
    s = jnp.sin(ang)
    lane = lax.broadcasted_iota(jnp.int32, ang.shape, 1)
    first = (lane % HEAD_DIM) < (HEAD_DIM // 2)
    cos_ref[...] = c
    sina_ref[...] = jnp.where(first, -s, 0.0)
    sinb_ref[...] = jnp.where(first, 0.0, s)


def _rope_tables(pos_flat):
    n = pos_flat.shape[0]
    tm = min(n, 1024)
    half = HEAD_DIM // 2
    inv = ROPE_THETA ** (-jnp.arange(half, dtype=F32) / half)
    inv_lane = jnp.tile(inv, LANES // half)[None, :]
    posb = jnp.broadcast_to(pos_flat.astype(F32)[:, None], (n, LANES))
    spec = pl.BlockSpec((tm, LANES), lambda i: (i, 0))
    out = jax.ShapeDtypeStruct((n, LANES), F32)
    return pl.pallas_call(
        _rope_tables_kernel,
        grid=(n // tm,),
        in_specs=[spec, _const_spec((1, LANES))],
        out_specs=[spec, spec, spec],
        out_shape=[out, out, out],
        compiler_params=_cparams(("arbitrary",)),
        name="rope_tables",
    )(posb, inv_lane)


def _rope128(v, cos, sina, sinb):
    half = HEAD_DIM // 2
    return (v * cos + pltpu.roll(v, LANES - half, 1) * sina + pltpu.roll(v, half, 1) * sinb)


_NSA_SEG = {
    "q": 0, "ks": 1024, "kw": 1280, "kc": 1536, "vc": 1792, "vs": 2048, "vw": 2304, "gate": 2560,
}
_NSA_GATE_COLS = NSA_GROUPS * LANES
_NSA_PROJ_COLS = 2560 + _NSA_GATE_COLS


def _nsa_proj_kernel(x_ref, w_ref, bg_ref, cos_ref, sina_ref, sinb_ref,
                     q_ref, ks_ref, kw_ref, kc_ref, vc_ref, vs_ref, vw_ref, g_ref):
    xb = x_ref[...].astype(MXU_DTYPE)
    cos = cos_ref[...]
    sina = sina_ref[...]
    sinb = sinb_ref[...]
    scale = HEAD_DIM ** -0.5 * LOG2_E

    def seg(name, width=256, c=0):
        o = _NSA_SEG[name] + c * 256
        return _dot(xb, w_ref[:, o:o + width])

    for c in range(NSA_Q_COLS // 256):
        acc = seg("q", c=c)
        for h in range(2):
            r = _rope128(acc[:, h * LANES:(h + 1) * LANES], cos, sina, sinb) * scale
            q_ref[:, c * 256 + h * LANES:c * 256 + (h + 1) * LANES] = r.astype(q_ref.dtype)
    for name, ref in (("ks", ks_ref), ("kw", kw_ref)):
        acc = seg(name)
        for h in range(2):
            r = _rope128(acc[:, h * LANES:(h + 1) * LANES], cos, sina, sinb)
            ref[:, h * LANES:(h + 1) * LANES] = r.astype(ref.dtype)
    kc_ref[...] = seg("kc")
    vc_ref[...] = seg("vc")
    vs_ref[...] = seg("vs").astype(vs_ref.dtype)
    vw_ref[...] = seg("vw").astype(vw_ref.dtype)
    g_ref[...] = jax.nn.sigmoid(seg("gate", width=_NSA_GATE_COLS) + bg_ref[...])


def _nsa_proj(x2, w_perm, bg_perm, cos_t, sina_t, sinb_t):
    t = x2.shape[0]
    tm = ROW_TILE
    row = lambda w: pl.BlockSpec((tm, w), lambda i: (i, 0))
    outs = [
        jax.ShapeDtypeStruct((t, NSA_Q_COLS), MXU_DTYPE),
        jax.ShapeDtypeStruct((t, NSA_KV_COLS), MXU_DTYPE),
        jax.ShapeDtypeStruct((t, NSA_KV_COLS), MXU_DTYPE),
        jax.ShapeDtypeStruct((t, NSA_KV_COLS), F32),
        jax.ShapeDtypeStruct((t, NSA_KV_COLS), F32),
        jax.ShapeDtypeStruct((t, NSA_KV_COLS), MXU_DTYPE),
        jax.ShapeDtypeStruct((t, NSA_KV_COLS), MXU_DTYPE),
        jax.ShapeDtypeStruct((t, _NSA_GATE_COLS), F32),
    ]
    return pl.pallas_call(
        _nsa_proj_kernel,
        grid=(t // tm,),
        in_specs=[row(D_MODEL), _const_spec((D_MODEL, _NSA_PROJ_COLS)),
                  _const_spec((1, _NSA_GATE_COLS)), row(LANES), row(LANES), row(LANES)],
        out_specs=[row(NSA_Q_COLS)] + [row(NSA_KV_COLS)] * 6 + [row(_NSA_GATE_COLS)],
        out_shape=outs,
        compiler_params=_cparams(("arbitrary",)),
        name="nsa_proj",
    )(x2, w_perm, bg_perm, cos_t, sina_t, sinb_t)


def _nsa_compress_kernel(kc_ref, vc_ref, pek_ref, pev_ref, w1k_ref, w1v_ref, w2k_ref, w2v_ref,
                         cos_ref, sina_ref, sinb_ref, ko_ref, vo_ref):
    nc = kc_ref.shape[1]
    half_feat = kc_ref.shape[2]
    rows = lax.broadcasted_iota(jnp.int32, (nc, 1), 0)
    valid = rows < nc - 1

    def compress(tok, pe_ref, w1_ref, w2_ref):
        a = (tok + pe_ref[:, :half_feat]).astype(MXU_DTYPE)
        b = (tok + pe_ref[:, half_feat:]).astype(MXU_DTYPE)
        pa = _dot(a, w1_ref[:half_feat, :])
        pb = _dot(b, w1_ref[half_feat:, :])
        pre = pa + pltpu.roll(pb, nc - 1, 0)
        h = jax.nn.gelu(pre)
        return _dot(h.astype(MXU_DTYPE), w2_ref[...])

    kcmp = compress(kc_ref[0], pek_ref, w1k_ref, w2k_ref)
    vcmp = compress(vc_ref[0], pev_ref, w1v_ref, w2v_ref)
    cos = cos_ref[0]
    sina = sina_ref[0]
    sinb = sinb_ref[0]
    for h in range(NSA_KV_COLS // LANES):
        r = _rope128(kcmp[:, h * LANES:(h + 1) * LANES], cos, sina, sinb)
        ko_ref[0, :, h * LANES:(h + 1) * LANES] = jnp.where(valid, r, 0.0).astype(ko_ref.dtype)
    vo_ref[0] = jnp.where(valid, vcmp, 0.0).astype(vo_ref.dtype)


def _nsa_compress(kc3, vc3, pek, pev, w1k, w1v, w2k, w2v, cos_c, sina_c, sinb_c):
    b, nc, feat = kc3.shape
    tok = pl.BlockSpec((1, nc, feat), lambda i: (i, 0, 0))
    tab = pl.BlockSpec((1, nc, LANES), lambda i: (i, 0, 0))
    out = pl.BlockSpec((1, nc, NSA_KV_COLS), lambda i: (i, 0, 0))
    oshape = jax.ShapeDtypeStruct((b, nc, NSA_KV_COLS), MXU_DTYPE)
    return pl.pallas_call(
        _nsa_compress_kernel,
        grid=(b,),
        in_specs=[tok, tok, _const_spec(pek.shape), _const_spec(pev.shape),
                  _const_spec(w1k.shape), _const_spec(w1v.shape),
                  _const_spec(w2k.shape), _const_spec(w2v.shape), tab, tab, tab],
        out_specs=[out, out],
        out_shape=[oshape, oshape],
        compiler_params=_cparams(("arbitrary",)),
        name="nsa_compress",
    )(kc3, vc3, pek, pev, w1k, w1v, w2k, w2v, cos_c, sina_c, sinb_c)


def _nsa_attn_kernel(q_ref, ks_ref, vs_ref, kw_ref, vw_ref, kc_ref, vc_ref, g_ref, ov_ref, et_ref,
                     o_ref, m_ref, acc_ref, sel_ref, *, n_sel, k_sel):
    tq = q_ref.shape[1]
    tk = ATT_TK
    nc = kc_ref.shape[1]
    R = NSA_HPG
    qi = pl.program_id(1)
    g = pl.program_id(2)
    t0 = qi * tq
    trow = t0 + lax.broadcasted_iota(jnp.int32, (tq, 1), 0)
    gmask = lax.broadcasted_iota(jnp.int32, (1, NSA_KV_COLS), 1) // HEAD_DIM == g
    gates = g_ref[0]
    qs = jnp.concatenate(
        [jnp.where(gmask, q_ref[0, :, r * NSA_KV_COLS:(r + 1) * NSA_KV_COLS], 0)
         for r in range(R)], axis=0)
    rows = lambda a, r: a[r * tq:(r + 1) * tq]

    def tile_update(c, k_ref, v_ref, k0, bias):
        kt = k_ref[0, pl.ds(k0, tk), :]
        vt = jnp.where(gmask, v_ref[0, pl.ds(k0, tk), :], 1)
        s_all = _dot_nt(qs, kt)
        ps, alphas = [], []
        for r in range(R):
            s = rows(s_all, r) + bias
            m_old = m_ref[c, r]
            m_new = jnp.maximum(m_old, jnp.max(s, axis=-1, keepdims=True))
            m_ref[c, r] = m_new
            alphas.append(jnp.exp2(m_old - m_new))
            m_wide = jnp.concatenate([m_new] * (tk // LANES), axis=1)
            ps.append(jnp.exp2(s - m_wide).astype(MXU_DTYPE))
        pv = _dot(jnp.concatenate(ps, axis=0), vt)
        for r in range(R):
            a_wide = jnp.concatenate([alphas[r]] * (NSA_KV_COLS // LANES), axis=1)
            acc_ref[c, r] = a_wide * acc_ref[c, r] + rows(pv, r)

    n_idx = lax.broadcasted_iota(jnp.int32, (1, nc), 1)
    cmp_ok = (n_idx * CMP_STRIDE + CMP_BLOCK - 1) <= trow
    s_all = _dot_nt(qs, kc_ref[0])
    psum = jnp.zeros((tq, nc), F32)
    ps = []
    for r in range(R):
        s = jnp.where(cmp_ok, rows(s_all, r), NEG_INF)
        e = jnp.where(cmp_ok, jnp.exp2(s - jnp.max(s, axis=-1, keepdims=True)), 0.0)
        p = e * (1.0 / jnp.maximum(jnp.sum(e, axis=-1, keepdims=True), 1e-30))
        psum = psum + p
        ps.append(p.astype(MXU_DTYPE))
    o_cmp = _dot(jnp.concatenate(ps, axis=0), vc_ref[0])

    all_selected = t0 + tq <= k_sel * SEL_BLOCK

    @pl.when(all_selected)
    def _():
        sel_ref[...] = jnp.ones(sel_ref.shape, sel_ref.dtype)

    @pl.when(jnp.logical_not(all_selected))
    def _():
        imp = jnp.dot(psum, ov_ref[...], preferred_element_type=F32,
                      precision=lax.Precision.HIGHEST)
        blk = lax.broadcasted_iota(jnp.int32, (1, LANES), 1)
        rel = trow // SEL_BLOCK - blk
        imp = jnp.where(rel >= 0, imp, -1.0)
        forced = (blk == 0) | (rel == 0) | (rel == 1)
        imp = jnp.where(forced, FORCE_SCORE, imp)
        imp_t = jnp.transpose(imp)[:n_sel, :]
        midx = lax.broadcasted_iota(jnp.int32, (n_sel, 1), 0)
        cnt = jnp.zeros((n_sel, tq), F32)
        for mp in range(n_sel):
            row = imp_t[mp:mp + 1, :]
            beats = (row > imp_t) | ((row == imp_t) & (midx > mp))
            cnt = cnt + jnp.where(beats, 1.0, 0.0)
        sel_t = jnp.where(cnt < k_sel, 1.0, 0.0)
        if n_sel < LANES:
            sel_t = jnp.concatenate([sel_t, jnp.zeros((LANES - n_sel, tq), F32)], axis=0)
        sel_ref[...] = jnp.transpose(sel_t).astype(sel_ref.dtype)

    sel = sel_ref[...]

    SEL, WIN = 0, 1
    m_ref[...] = jnp.full(m_ref.shape, NEG_INF, F32)
    acc_ref[...] = jnp.zeros(acc_ref.shape, F32)

    def sel_update(k0, kpos):
        selm = _dot_nt(sel, et_ref[pl.ds(k0, tk), :])
        bias = jnp.where((selm > 0.5) & (kpos <= trow), 0.0, NEG_INF)
        tile_update(SEL, ks_ref, vs_ref, k0, bias)

    def win_update(k0, kpos):
        d = trow - kpos
        bias = jnp.where((d >= 0) & (d < WINDOW), 0.0, NEG_INF)
        tile_update(WIN, kw_ref, vw_ref, k0, bias)

    def key_tile(j):
        k0 = pl.multiple_of(j * tk, tk)
        return k0, k0 + lax.broadcasted_iota(jnp.int32, (1, tk), 1)

    def sel_only(j, carry):
        sel_update(*key_tile(j))
        return carry

    def sel_and_win(j, carry):
        k0, kpos = key_tile(j)
        sel_update(k0, kpos)
        win_update(k0, kpos)
        return carry

    j_win = jnp.maximum(t0 - WINDOW, 0) // tk
    lax.fori_loop(0, j_win, sel_only, 0)
    lax.fori_loop(j_win, (t0 + tq) // tk, sel_and_win, 0)

    def normalized(c, r):
        den = jnp.where(g < NSA_GROUPS // 2, acc_ref[c, r, :, LANES:], acc_ref[c, r, :, :LANES])
        inv = 1.0 / den
        return acc_ref[c, r] * jnp.concatenate([inv, inv], axis=1)

    o_heads = [gates[:, r:r + 1] * rows(o_cmp, r)
               + gates[:, R + r:R + r + 1] * normalized(SEL, r)
               + gates[:, 2 * R + r:2 * R + r + 1] * normalized(WIN, r) for r in range(R)]

    for gg in range(NSA_GROUPS):
        @pl.when(g == gg)
        def _(gg=gg):
            for r in range(R):
                lo = r * NSA_KV_COLS + gg * HEAD_DIM
                o_ref[0, :, lo:lo + HEAD_DIM] = (
                    o_heads[r][:, gg * HEAD_DIM:(gg + 1) * HEAD_DIM].astype(o_ref.dtype))


def _nsa_attn(q, ks, vs, kw, vw, kcmp, vcmp, gates, ov, et):
    b, s, _ = q.shape
    nc = kcmp.shape[1]
    tq = ATT_TQ
    n_sel = s // SEL_BLOCK
    k_sel = min(SEL_TOP, n_sel)
    qspec = pl.BlockSpec((1, tq, NSA_Q_COLS), lambda i, j, g: (i, j, 0))
    kv = pl.BlockSpec((1, s, NSA_KV_COLS), lambda i, j, g: (i, 0, 0))
    cmp_spec = pl.BlockSpec((1, nc, NSA_KV_COLS), lambda i, j, g: (i, 0, 0))
    gspec = pl.BlockSpec((1, tq, LANES), lambda i, j, g: (i, j, g))
    return pl.pallas_call(
        functools.partial(_nsa_attn_kernel, n_sel=n_sel, k_sel=k_sel),
        grid=(b, s // tq, NSA_GROUPS),
        in_specs=[qspec, kv, kv, kv, kv, cmp_spec, cmp_spec, gspec,
                  _const_spec(ov.shape), _const_spec(et.shape)],
        out_specs=qspec,
        out_shape=jax.ShapeDtypeStruct((b, s, NSA_Q_COLS), MXU_DTYPE),
        scratch_shapes=[pltpu.VMEM((2, NSA_HPG, tq, LANES), F32),
                        pltpu.VMEM((2, NSA_HPG, tq, NSA_KV_COLS), F32),
                        pltpu.VMEM((tq, LANES), MXU_DTYPE)],
        compiler_params=_cparams(("arbitrary", "arbitrary", "arbitrary")),
        name="nsa_attn",
    )(q, ks, vs, kw, vw, kcmp, vcmp, gates, ov, et)


def _mm_res_ln_kernel(a_ref, w_ref, x_ref, g_ref, b_ref, o_ref):
    m = _dot(a_ref[...].astype(MXU_DTYPE), w_ref[...])
    o_ref[...] = _layer_norm_rows(DN_ALPHA * x_ref[...] + m, g_ref[...], b_ref[...])


def _mm_res_ln(a, w, x2, g, b):
    t, k = a.shape
    tm = ROW_TILE
    return pl.pallas_call(
        _mm_res_ln_kernel,
        grid=(t // tm,),
        in_specs=[pl.BlockSpec((tm, k), lambda i: (i, 0)), _const_spec(w.shape),
                  pl.BlockSpec((tm, D_MODEL), lambda i: (i, 0)),
                  _const_spec((1, D_MODEL)), _const_spec((1, D_MODEL))],
        out_specs=pl.BlockSpec((tm, D_MODEL), lambda i: (i, 0)),
        out_shape=jax.ShapeDtypeStruct((t, D_MODEL), F32),
        compiler_params=_cparams(("arbitrary",)),
        name="mm_res_ln",
    )(a, w, x2, g, b)


def _shift_rows(v, prev, k):
    out = pltpu.roll(v, k, 0)
    rows = lax.broadcasted_iota(jnp.int32, (v.shape[0], 1), 0)
    np_rows = prev.shape[0]
    for i in range(k):
        out = jnp.where(rows == i, prev[np_rows - k + i:np_rows - k + i + 1, :], out)
    return out


def _shortconv_kernel(x_ref, win_ref, cw_ref, cb_ref, wout_ref, g_ref, b_ref, o_ref, carry_ref):
    @pl.when(pl.program_id(1) == 0)
    def _():
        carry_ref[...] = jnp.zeros(carry_ref.shape, F32)

    x = x_ref[0]
    xb = x.astype(MXU_DTYPE)
    d = D_MODEL
    u = _dot(xb, win_ref[:, 0:d])
    bg = _dot(xb, win_ref[:, d:2 * d])
    cg = _dot(xb, win_ref[:, 2 * d:3 * d])
    v = cg * u
    prev = carry_ref[...]
    z = cb_ref[...] + cw_ref[SC_WIDTH - 1:SC_WIDTH, :] * v
    for k in range(1, SC_WIDTH):
        z = z + cw_ref[SC_WIDTH - 1 - k:SC_WIDTH - k, :] * _shift_rows(v, prev, k)
    carry_ref[...] = v[v.shape[0] - SUBLANES:, :]
    m = _dot((bg * z).astype(MXU_DTYPE), wout_ref[...])
    o_ref[0] = _layer_norm_rows(DN_ALPHA * x + m, g_ref[...], b_ref[...])


def _shortconv_layer(x3, win, cw, cb, wout, g, b):
    bsz, s, d = x3.shape
    tm = ROW_TILE
    xspec = pl.BlockSpec((1, tm, d), lambda i, j: (i, j, 0))
    return pl.pallas_call(
        _shortconv_kernel,
        grid=(bsz, s // tm),
        in_specs=[xspec, _const_spec(win.shape), _const_spec(cw.shape), _const_spec(cb.shape),
                  _const_spec(wout.shape), _const_spec(g.shape), _const_spec(b.shape)],
        out_specs=xspec,
        out_shape=jax.ShapeDtypeStruct(x3.shape, F32),
        scratch_shapes=[pltpu.VMEM((SUBLANES, d), F32)],
        compiler_params=_cparams(("arbitrary", "arbitrary")),
        name="shortconv_layer",
    )(x3, win, cw, cb, wout, g, b)


def _round_down(x, m):
    return x // m * m


def _round_up(x, m):
    return -(-x // m) * m


def _block_diag_dot(u, w_ref):
    tile = 2 * LANES
    outs = []
    for c0 in range(0, LRU_WIDTH, tile):
        c1 = min(c0 + tile, LRU_WIDTH)
        lo = _round_down(_round_down(c0, LRU_BW), LANES)
        hi = min(_round_up(_round_up(c1, LRU_BW), LANES), LRU_WIDTH)
        outs.append(_dot(u[:, lo:hi], w_ref[lo:hi, c0:c1]))
    return jnp.concatenate(outs, axis=1)


def _rglru_kernel(x_ref, win_ref, cw_ref, cb_ref, wa_ref, ba_ref, wx_ref, bx_ref, lam_ref,
                  wout_ref, g_ref, b_ref, o_ref, carry_ref, h_ref, a_s, b_s):
    @pl.when(pl.program_id(1) == 0)
    def _():
        carry_ref[...] = jnp.zeros(carry_ref.shape, F32)
        h_ref[...] = jnp.zeros(h_ref.shape, F32)

    x = x_ref[0]
    tm = x.shape[0]
    xb = x.astype(MXU_DTYPE)
    w = LRU_WIDTH
    gate_br = _dot(xb, win_ref[:, 0:w])
    rec = _dot(xb, win_ref[:, w:2 * w])
    prev = carry_ref[...]
    u = cb_ref[...] + cw_ref[LRU_CONV - 1:LRU_CONV, :] * rec
    for k in range(1, LRU_CONV):
        u = u + cw_ref[LRU_CONV - 1 - k:LRU_CONV - k, :] * _shift_rows(rec, prev, k)
    carry_ref[...] = rec[tm - SUBLANES:, :]
    ub = u.astype(MXU_DTYPE)
    r = jax.nn.sigmoid(_block_diag_dot(ub, wa_ref) + ba_ref[...])
    ig = jax.nn.sigmoid(_block_diag_dot(ub, wx_ref) + bx_ref[...])
    log_a = LRU_C * r * jax.nn.log_sigmoid(lam_ref[...])
    a_s[...] = jnp.exp(log_a)
    th = jnp.tanh(log_a)
    b_s[...] = jnp.sqrt(-2.0 * th / (1.0 - th)) * (ig * u)

    rows = lax.broadcasted_iota(jnp.int32, (SUBLANES, 1), 0)

    def group(i, hc):
        r0 = pl.multiple_of(i * SUBLANES, SUBLANES)
        a = a_s[pl.ds(r0, SUBLANES), :]
        bv = b_s[pl.ds(r0, SUBLANES), :]
        for dlt in (1, 2, 4):
            a_sh = jnp.where(rows >= dlt, pltpu.roll(a, dlt, 0), 1.0)
            b_sh = jnp.where(rows >= dlt, pltpu.roll(bv, dlt, 0), 0.0)
            bv = a * b_sh + bv
            a = a * a_sh
        h = a * hc + bv
        b_s[pl.ds(r0, SUBLANES), :] = h
        return h[SUBLANES - 1:SUBLANES, :]

    h_ref[...] = lax.fori_loop(0, tm // SUBLANES, group, h_ref[...])
    y = jax.nn.gelu(gate_br) * b_s[...]
    m = _dot(y.astype(MXU_DTYPE), wout_ref[...])
    o_ref[0] = _layer_norm_rows(DN_ALPHA * x + m, g_ref[...], b_ref[...])


def _rglru_layer(x3, win, cw, cb, wa, ba, wx, bx, lam, wout, g, b):
    bsz, s, d = x3.shape
    tm = LRU_ROW_TILE
    xspec = pl.BlockSpec((1, tm, d), lambda i, j: (i, j, 0))
    consts = [win, cw, cb, wa, ba, wx, bx, lam, wout, g, b]
    return pl.pallas_call(
        _rglru_kernel,
        grid=(bsz, s // tm),
        in_specs=[xspec] + [_const_spec(c.shape) for c in consts],
        out_specs=xspec,
        out_shape=jax.ShapeDtypeStruct(x3.shape, F32),
        scratch_shapes=[pltpu.VMEM((SUBLANES, LRU_WIDTH), F32), pltpu.VMEM((1, LRU_WIDTH), F32),
                        pltpu.VMEM((tm, LRU_WIDTH), F32), pltpu.VMEM((tm, LRU_WIDTH), F32)],
        compiler_params=_cparams(("arbitrary", "arbitrary")),
        name="rglru_layer",
    )(x3, win, cw, cb, wa, ba, wx, bx, lam, wout, g, b)


def _moe_router_kernel(x_ref, rwt_ref, rb_ref, tri_ref, pos_ref, wts_ref, meta_ref):
    tt = x_ref.shape[0]
    x = x_ref[...]
    w = rwt_ref[...]
    xh = x.astype(MXU_DTYPE)
    wh = w.astype(MXU_DTYPE)
    xl = (x - xh.astype(F32)).astype(MXU_DTYPE)
    wl = (w - wh.astype(F32)).astype(MXU_DTYPE)
    logits = _dot_nt(wh, xh) + _dot_nt(wh, xl) + _dot_nt(wl, xh)
    scores = jax.nn.sigmoid(logits)
    biased = scores + rb_ref[...]
    rows = [biased[e:e + 1, :] for e in range(N_EXPERTS)]

    gscore = []
    for gi in range(N_GROUPS):
        a, b, c, d = rows[gi * EXPERTS_PER_GROUP:(gi + 1) * EXPERTS_PER_GROUP]
        hi1, lo1 = jnp.maximum(a, b), jnp.minimum(a, b)
        hi2, lo2 = jnp.maximum(c, d), jnp.minimum(c, d)
        gscore.append(jnp.maximum(hi1, hi2) + jnp.maximum(jnp.minimum(hi1, hi2),
                                                          jnp.maximum(lo1, lo2)))
    best = gscore[0]
    top_g = jnp.zeros((1, tt), jnp.int32)
    for gi in range(1, N_GROUPS):
        better = gscore[gi] > best
        best = jnp.where(better, gscore[gi], best)
        top_g = jnp.where(better, gi, top_g)

    def argmax_in_group(exclude):
        bv = jnp.full((1, tt), -jnp.inf, F32)
        bi = jnp.full((1, tt), -1, jnp.int32)
        for e in range(N_EXPERTS):
            ok = top_g == (e // EXPERTS_PER_GROUP)
            if exclude is not None:
                ok = ok & (exclude != e)
            better = ok & (rows[e] > bv)
            bv = jnp.where(better, rows[e], bv)
            bi = jnp.where(better, e, bi)
        return bi

    e1 = argmax_in_group(None)
    e2 = argmax_in_group(e1)
    eidx = lax.broadcasted_iota(jnp.int32, (N_EXPERTS, 1), 0)
    oh1 = eidx == e1
    oh2 = eidx == e2
    s1 = jnp.sum(jnp.where(oh1, scores, 0.0), axis=0, keepdims=True)
    s2 = jnp.sum(jnp.where(oh2, scores, 0.0), axis=0, keepdims=True)
    wsum = s1 + s2
    wts_ref[0, 0:1, :] = s1 / wsum
    wts_ref[0, 1:2, :] = s2 / wsum

    oh = jnp.where(oh1 | oh2, 1.0, 0.0)
    csum = _dot(oh.astype(MXU_DTYPE), tri_ref[...])
    excl = csum - oh
    cnt = csum[:, tt - 1:tt]
    padded = jnp.ceil(cnt * (1.0 / MOE_ALIGN)) * MOE_ALIGN
    run = jnp.zeros((1, 1), F32)
    off_rows = []
    for e in range(N_EXPERTS):
        off_rows.append(run)
        run = run + padded[e:e + 1, :]
    off = jnp.concatenate(off_rows, axis=0)
    slot = off + excl

    def flat_row_of(onehot):
        p = jnp.sum(jnp.where(onehot, slot, 0.0), axis=0, keepdims=True).astype(jnp.int32)
        return _flat_row(p)

    pos_ref[0, 0:1, :] = flat_row_of(oh1)
    pos_ref[0, 1:2, :] = flat_row_of(oh2)
    lane = lax.broadcasted_iota(jnp.int32, (1, LANES), 1)
    meta = jnp.zeros((1, LANES), F32)
    for e in range(N_EXPERTS):
        meta = jnp.where(lane == e, off[e:e + 1, :], meta)
        meta = jnp.where(lane == N_EXPERTS + e, cnt[e:e + 1, :], meta)
    meta_ref[0] = meta.astype(jnp.int32)


def _moe_router(x2, rwt, rb, tri):
    t = x2.shape[0]
    tt = tri.shape[0]
    nt = t // tt
    return pl.pallas_call(
        _moe_router_kernel,
        grid=(nt,),
        in_specs=[pl.BlockSpec((tt, D_MODEL), lambda i: (i, 0)), _const_spec(rwt.shape),
                  _const_spec(rb.shape), _const_spec(tri.shape)],
        out_specs=[pl.BlockSpec((1, TOP_K, tt), lambda i: (i, 0, 0)),
                   pl.BlockSpec((1, TOP_K, tt), lambda i: (i, 0, 0)),
                   pl.BlockSpec((1, 1, LANES), lambda i: (i, 0, 0))],
        out_shape=[jax.ShapeDtypeStruct((nt, TOP_K, tt), jnp.int32),
                   jax.ShapeDtypeStruct((nt, TOP_K, tt), F32),
                   jax.ShapeDtypeStruct((nt, 1, LANES), jnp.int32)],
        compiler_params=_cparams(("arbitrary",)),
        name="moe_router",
    )(x2, rwt, rb, tri)


def _flat_row(r):
    return (r >> _SUBLANE_SHIFT) * _TILE_ROW + (r & (SUBLANES - 1))


def _row_slice(q):
    return pl.ds(q, _LANE_TILES, stride=SUBLANES)


def _load_rows_flat(ref, r0, n):
    base = pl.multiple_of(r0 * _LANE_TILES, _TILE_ROW)
    cols = [jnp.concatenate([ref[pl.ds(base + i * _TILE_ROW + c * SUBLANES, SUBLANES), :]
                             for i in range(n // SUBLANES)], axis=0) for c in range(_LANE_TILES)]
    return jnp.concatenate(cols, axis=1)


def _store_rows_flat(ref, r0, val):
    base = pl.multiple_of(r0 * _LANE_TILES, _TILE_ROW)
    for c in range(_LANE_TILES):
        for i in range(val.shape[0] // SUBLANES):
            ref[pl.ds(base + i * _TILE_ROW + c * SUBLANES, SUBLANES), :] = (
                val[i * SUBLANES:(i + 1) * SUBLANES, c * LANES:(c + 1) * LANES])


def _moe_ffn_kernel(pos_ref, wts_ref, meta_ref, x_ref, wg_ref, wu_ref, wd_ref, g_ref, b_ref,
                    o_ref, xs_ref, xf_ref):
    tt = x_ref.shape[0]
    ch = MOE_CHUNK
    e = pl.program_id(1)

    def token_group(body):
        def group(i, carry):
            base = pl.multiple_of(i * SUBLANES, SUBLANES)
            qbase = pl.multiple_of(i * _TILE_ROW, _TILE_ROW)
            for k in range(SUBLANES):
                body(base + k, qbase + k)
            return carry

        lax.fori_loop(0, tt // SUBLANES, group, 0)

    @pl.when(e == 0)
    def _dispatch():
        def to_flat(i, carry):
            r0 = pl.multiple_of(i * SUBLANES, SUBLANES)
            _store_rows_flat(xf_ref, r0, x_ref[pl.ds(r0, SUBLANES), :])
            return carry

        lax.fori_loop(0, tt // SUBLANES, to_flat, 0)

        def zero_gap(ei, carry):
            start = meta_ref[ei] + meta_ref[N_EXPERTS + ei]
            stop = jnp.where(ei == N_EXPERTS - 1, start + ch + MOE_ALIGN,
                             meta_ref[jnp.minimum(ei + 1, N_EXPERTS - 1)])

            def zrow(r, c):
                xs_ref[_row_slice(_flat_row(r)), :] = jnp.zeros((_LANE_TILES, LANES), F32)
                return c

            return lax.fori_loop(start, stop, zrow, carry)

        lax.fori_loop(0, N_EXPERTS, zero_gap, 0)

        def scatter(t, q):
            row = xf_ref[_row_slice(q), :]
            xs_ref[_row_slice(pos_ref[t]), :] = row
            xs_ref[_row_slice(pos_ref[tt + t]), :] = row

        token_group(scatter)

    off = meta_ref[e]
    cnt = meta_ref[N_EXPERTS + e]
    rid = lax.broadcasted_iota(jnp.int32, (ch, 1), 0)

    def chunk(c, carry):
        r0 = pl.multiple_of(off + c * ch, MOE_ALIGN)
        xs = _load_rows_flat(xs_ref, r0, ch)
        xb = xs.astype(MXU_DTYPE)
        hid = jax.nn.silu(_dot(xb, wg_ref[0, 0])) * _dot(xb, wu_ref[0, 0])
        out = _dot(hid.astype(MXU_DTYPE), wd_ref[0, 0])
        _store_rows_flat(xs_ref, r0, jnp.where(rid + c * ch < cnt, out, xs))
        return carry

    lax.fori_loop(0, (cnt + ch - 1) // ch, chunk, 0)

    @pl.when(e == N_EXPERTS - 1)
    def _combine():
        def gather(t, q):
            xf_ref[_row_slice(q), :] = (wts_ref[t] * xs_ref[_row_slice(pos_ref[t]), :]
                                        + wts_ref[tt + t] * xs_ref[_row_slice(pos_ref[tt + t]), :])

        token_group(gather)

        def norm(i, carry):
            r0 = pl.multiple_of(i * MOE_NORM_ROWS, MOE_NORM_ROWS)
            y = (DN_ALPHA * x_ref[pl.ds(r0, MOE_NORM_ROWS), :]
                 + _load_rows_flat(xf_ref, r0, MOE_NORM_ROWS))
            o_ref[pl.ds(r0, MOE_NORM_ROWS), :] = _layer_norm_rows(y, g_ref[...], b_ref[...])
            return carry

        lax.fori_loop(0, tt // MOE_NORM_ROWS, norm, 0)


def _moe_ffn(pos, wts, meta, x2, layer, wg, wu, wd, g, b):
    t = x2.shape[0]
    tt = pos.shape[2]
    nt = t // tt
    smem = lambda n: pl.BlockSpec((n,), lambda i, e: (i,), memory_space=pltpu.SMEM)
    wspec = lambda w: pl.BlockSpec((1, 1) + w.shape[2:], lambda i, e: (layer, e, 0, 0))
    xs_rows = TOP_K * tt + N_EXPERTS * MOE_ALIGN + MOE_CHUNK + MOE_ALIGN
    pos, wts, meta = pos.reshape(-1), wts.reshape(-1), meta.reshape(-1)
    return pl.pallas_call(
        _moe_ffn_kernel,
        grid=(nt, N_EXPERTS),
        in_specs=[smem(TOP_K * tt), smem(TOP_K * tt), smem(LANES),
                  pl.BlockSpec((tt, D_MODEL), lambda i, e: (i, 0), pipeline_mode=pl.Buffered(1)),
                  wspec(wg), wspec(wu), wspec(wd),
                  pl.BlockSpec((1, D_MODEL), lambda i, e: (0, 0)),
                  pl.BlockSpec((1, D_MODEL), lambda i, e: (0, 0))],
        out_specs=pl.BlockSpec((tt, D_MODEL), lambda i, e: (i, 0), pipeline_mode=pl.Buffered(1)),
        out_shape=jax.ShapeDtypeStruct((t, D_MODEL), F32),
        scratch_shapes=[pltpu.VMEM((xs_rows * _LANE_TILES, LANES), F32),
                        pltpu.VMEM((tt * _LANE_TILES, LANES), F32)],
        compiler_params=_cparams(("arbitrary", "arbitrary")),
        name="moe_ffn",
    )(pos, wts, meta, x2, wg, wu, wd, g, b)


def _nsa_weight_layout(w_in, b_gate):
    G, R, Dh = NSA_GROUPS, NSA_HPG, HEAD_DIM
    d_in = w_in.shape[0]
    kvw = NSA_KV_COLS
    wq = w_in[:, :NSA_Q_COLS].reshape(d_in, G, R, Dh).transpose(0, 2, 1, 3).reshape(d_in, NSA_Q_COLS)
    kc, vc, ks, vs, kw, vw = (w_in[:, NSA_Q_COLS + i * kvw:NSA_Q_COLS + (i + 1) * kvw] for i in range(6))
    gate_base = NSA_Q_COLS + 6 * kvw

    def gate_layout(v):
        lead = v.shape[:-1]
        v = jnp.swapaxes(v.reshape(lead + (G, R, 3)), -1, -2).reshape(lead + (G, 3 * R))
        v = jnp.pad(v, [(0, 0)] * (len(lead) + 1) + [(0, LANES - 3 * R)])
        return v.reshape(lead + (_NSA_GATE_COLS,))

    wg = gate_layout(w_in[:, gate_base:gate_base + 3 * NSA_HEADS])
    w = jnp.concatenate([wq, ks, kw, kc, vc, vs, vw, wg], axis=1).astype(MXU_DTYPE)
    return w, gate_layout(b_gate)[None, :]


def _nsa_out_layout(w_out):
    G, R, Dh = NSA_GROUPS, NSA_HPG, HEAD_DIM
    d_out = w_out.shape[1]
    w = w_out.reshape(G, R, Dh, d_out).transpose(1, 0, 2, 3).reshape(NSA_Q_COLS, d_out)
    return w.astype(MXU_DTYPE)


def _compress_weight_layout(pe, w1, w2):
    G, Dh, L = NSA_GROUPS, HEAD_DIM, CMP_BLOCK
    eye = jnp.eye(G, dtype=w1.dtype)
    w1r = w1.reshape(L, Dh, Dh)
    w1big = jnp.einsum("ldj,gh->lgdhj", w1r, eye).reshape(L * G * Dh, G * Dh)
    w2big = jnp.einsum("dj,gh->gdhj", w2, eye).reshape(G * Dh, G * Dh)
    pebig = jnp.broadcast_to(pe[:, None, :], (L, G, Dh)).reshape(1, L * G * Dh)
    return pebig, w1big.astype(MXU_DTYPE), w2big.astype(MXU_DTYPE)


def _block_diag(w):
    h, bi, bj = w.shape
    eye = jnp.eye(h, dtype=w.dtype)
    return jnp.einsum("hij,hk->hikj", w, eye).reshape(h * bi, h * bj).astype(MXU_DTYPE)


def _nsa_constants(s):
    nc = s // CMP_STRIDE
    n_cmp = (s - CMP_BLOCK) // CMP_STRIDE + 1
    n_sel = s // SEL_BLOCK
    cmp_start = np.arange(nc) * CMP_STRIDE
    cmp_end = cmp_start + CMP_BLOCK - 1
    sel_start = np.arange(n_sel) * SEL_BLOCK
    ov = ((cmp_start[:, None] <= sel_start[None, :] + SEL_BLOCK - 1)
          & (cmp_end[:, None] >= sel_start[None, :])
          & (np.arange(nc)[:, None] < n_cmp)).astype(np.float32)
    ov = np.pad(ov, ((0, 0), (0, LANES - n_sel)))
    et = (np.arange(s)[:, None] // SEL_BLOCK == np.arange(LANES)[None, :]).astype(np.float32)
    return jnp.asarray(ov), jnp.asarray(et, dtype=MXU_DTYPE)


def _nsa_layer(x2, bsz, s, tables, cmp_tables, w_in, b_gate, pos_k, w1_k, w2_k, pos_v, w1_v, w2_v,
               w_out, g, b):
    w_perm, bg_perm = _nsa_weight_layout(w_in, b_gate)
    q, ks, kw, kc, vc, vs, vw, gates = _nsa_proj(x2, w_perm, bg_perm, *tables)
    nc = s // CMP_STRIDE
    feat = CMP_STRIDE * NSA_KV_COLS
    pek, w1k, w2k = _compress_weight_layout(pos_k, w1_k, w2_k)
    pev, w1v, w2v = _compress_weight_layout(pos_v, w1_v, w2_v)
    kcmp, vcmp = _nsa_compress(kc.reshape(bsz, nc, feat), vc.reshape(bsz, nc, feat),
                               pek, pev, w1k, w1v, w2k, w2v, *cmp_tables)
    ov, et = _nsa_constants(s)
    r3 = lambda a: a.reshape(bsz, s, a.shape[-1])
    o = _nsa_attn(r3(q), r3(ks), r3(vs), r3(kw), r3(vw), kcmp, vcmp, r3(gates), ov, et)
    return _mm_res_ln(o.reshape(bsz * s, NSA_Q_COLS), _nsa_out_layout(w_out), x2, g, b)


def _moe_layer(x2, rwt, rb, tri, layer, wg, wu, wd, g, b):
    pos, wts, meta = _moe_router(x2, rwt, rb, tri)
    return _moe_ffn(pos, wts, meta, x2, layer, wg, wu, wd, g, b)


def kernel(x, positions, nsa_w_in, nsa_b_gate, nsa_cmp_pos_k, nsa_cmp_w1_k, nsa_cmp_w2_k, nsa_cmp_pos_v, nsa_cmp_w1_v, nsa_cmp_w2_v, nsa_w_out, sc_w_in, sc_conv_w, sc_conv_b, sc_w_out, lru_w_in, lru_conv_w, lru_conv_b, lru_wa, lru_ba, lru_wx, lru_bx, lru_lambda, lru_w_out, router_w, router_b, moe_w_gate, moe_w_up, moe_w_down, ln_g, ln_b):
    bsz, s, d = x.shape
    t = bsz * s
    x2 = x.reshape(t, d)
    row = lambda v: v.reshape(1, -1)

    tables = _rope_tables(positions.reshape(t))
    nc = s // CMP_STRIDE
    cmp_idx = np.minimum(np.arange(nc) * CMP_STRIDE + CMP_BLOCK - 1, s - 1)
    cmp_tables = tuple(tb.reshape(bsz, nc, LANES)
                       for tb in _rope_tables(positions[:, cmp_idx].reshape(bsz * nc)))

    tt = min(MOE_TOKENS, t)
    tri = jnp.asarray(np.triu(np.ones((tt, tt), np.float32)), dtype=MXU_DTYPE)
    rwt = router_w.T
    rb = router_b.reshape(N_EXPERTS, 1)
    moe_wg, moe_wu, moe_wd = (w.astype(MXU_DTYPE) for w in (moe_w_gate, moe_w_up, moe_w_down))

    for i in range(DEPTH):
        kind, j = i % N_MIXERS, i // N_MIXERS
        g0, b0 = row(ln_g[i, 0]), row(ln_b[i, 0])
        if kind == 0:
            x2 = _nsa_layer(x2, bsz, s, tables, cmp_tables, nsa_w_in[j], nsa_b_gate[j],
                            nsa_cmp_pos_k[j], nsa_cmp_w1_k[j], nsa_cmp_w2_k[j],
                            nsa_cmp_pos_v[j], nsa_cmp_w1_v[j], nsa_cmp_w2_v[j], nsa_w_out[j], g0, b0)
        elif kind == 1:
            x2 = _shortconv_layer(x2.reshape(bsz, s, d), sc_w_in[j].astype(MXU_DTYPE), sc_conv_w[j],
                                  row(sc_conv_b[j]), sc_w_out[j].astype(MXU_DTYPE), g0, b0).reshape(t, d)
        else:
            x2 = _rglru_layer(x2.reshape(bsz, s, d), lru_w_in[j].astype(MXU_DTYPE), lru_conv_w[j],
                              row(lru_conv_b[j]), _block_diag(lru_wa[j]), row(lru_ba[j]),
                              _block_diag(lru_wx[j]), row(lru_bx[j]), row(lru_lambda[j]),
                              lru_w_out[j].astype(MXU_DTYPE), g0, b0).reshape(t, d)
        x2 = _moe_layer(x2, rwt, rb, tri, i, moe_wg, moe_wu, moe_wd, row(ln_g[i, 1]), row(ln_b[i, 1]))
    return x2.reshape(bsz, s, d)
```

```python
import functools

import numpy as np
import jax
import jax.numpy as jnp
from jax import lax
from jax.experimental import pallas as pl
from jax.experimental.pallas import tpu as pltpu

F32 = jnp.float32
BF16 = jnp.bfloat16
MXU_DTYPE = BF16

D_MODEL = 1024
DEPTH = 4
N_MIXERS = 3
NSA_HEADS = 16
NSA_GROUPS = 4
NSA_HPG = NSA_HEADS // NSA_GROUPS
HEAD_DIM = D_MODEL // NSA_HEADS
CMP_BLOCK = 32
CMP_STRIDE = 16
SEL_BLOCK = 64
SEL_TOP = 16
WINDOW = 512
ROPE_THETA = 10000.0
FORCE_SCORE = 1e6
NSA_Q_COLS = NSA_HEADS * HEAD_DIM
NSA_KV_COLS = NSA_GROUPS * HEAD_DIM
SC_WIDTH = 3
LRU_WIDTH = 1280
LRU_BLOCKS = 16
LRU_BW = LRU_WIDTH // LRU_BLOCKS
LRU_CONV = 4
LRU_C = 8.0
N_EXPERTS = 16
N_GROUPS = 4
EXPERTS_PER_GROUP = N_EXPERTS // N_GROUPS
TOP_K = 2
D_FF = 1024
DN_ALPHA = (2.0 * DEPTH) ** 0.25
LN_EPS = 1e-5
NEG_INF = -1e30
LOG2_E = 1.4426950408889634

LANES = 128
SUBLANES = 8
VMEM_LIMIT_BYTES = 56 * 1024 * 1024
ROW_TILE = 512
LRU_ROW_TILE = 256
ATT_TQ = 512
ATT_TK = 256
MOE_TOKENS = 2048
MOE_CHUNK = 288
MOE_NORM_ROWS = 128
MOE_ALIGN = SUBLANES
_SUBLANE_SHIFT = 3
_LANE_TILES = D_MODEL // LANES
_TILE_ROW = SUBLANES * _LANE_TILES


def _cparams(sem):
    return pltpu.CompilerParams(dimension_semantics=sem, vmem_limit_bytes=VMEM_LIMIT_BYTES)


def _dot(a, b):
    return jnp.dot(a, b, preferred_element_type=F32)


def _dot_nt(a, b, precision=None):
    return lax.dot_general(a, b, (((1,), (1,)), ((), ())), preferred_element_type=F32,
                           precision=precision)


def _layer_norm_rows(y, g, b):
    mu = jnp.mean(y, axis=-1, keepdims=True)
    d = y - mu
    var = jnp.mean(d * d, axis=-1, keepdims=True)
    return d * lax.rsqrt(var + LN_EPS) * g + b


def _const_spec(shape):
    nd = len(shape)
    return pl.BlockSpec(shape, lambda *_: (0,) * nd)


def _rope_tables_kernel(pos_ref, inv_ref, cos_ref, sina_ref, sinb_ref):
    ang = pos_ref[...] * inv_ref[...]
    c = jnp.cos(ang)
    s = jnp.sin(ang)
    lane = lax.broadcasted_iota(jnp.int32, ang.shape, 1)
    first = (lane % HEAD_DIM) < (HEAD_DIM // 2)
    cos_ref[...] = c
    sina_ref[...] = jnp.where(first, -s, 0.0)
    sinb_ref[...] = jnp.where(first, 0.0, s)


def _rope_tables(pos_flat):
    n = pos_flat.shape[0]
    tm = min(n, 1024)
    half = HEAD_DIM // 2
    inv = ROPE_THETA ** (-jnp.arange(half, dtype=F32) / half)
    inv_lane = jnp.tile(inv, LANES // half)[None, :]
    posb = jnp.broadcast_to(pos_flat.astype(F32)[:, None], (n, LANES))
    spec = pl.BlockSpec((tm, LANES), lambda i: (i, 0))
    out = jax.ShapeDtypeStruct((n, LANES), F32)
    return pl.pallas_call(
        _rope_tables_kernel,
        grid=(n // tm,),
        in_specs=[spec, _const_spec((1, LANES))],
        out_specs=[spec, spec, spec],
        out_shape=[out, out, out],
        compiler_params=_cparams(("arbitrary",)),
        name="rope_tables",
    )(posb, inv_lane)


def _rope128(v, cos, sina, sinb):
    half = HEAD_DIM // 2
    return (v * cos + pltpu.roll(v, LANES - half, 1) * sina + pltpu.roll(v, half, 1) * sinb)


_NSA_SEG = {
    "q": 0, "ks": 1024, "kw": 1280, "kc": 1536, "vc": 1792, "vs": 2048, "vw": 2304, "gate": 2560,
}
_NSA_GATE_COLS = NSA_GROUPS * LANES
_NSA_PROJ_COLS = 2560 + _NSA_GATE_COLS


def _nsa_proj_kernel(x_ref, w_ref, bg_ref, cos_ref, sina_ref, sinb_ref,
                     q_ref, ks_ref, kw_ref, kc_ref, vc_ref, vs_ref, vw_ref, g_ref, chunk_ref):
    xb = x_ref[...].astype(MXU_DTYPE)
    cos = cos_ref[...]
    sina = sina_ref[...]
    sinb = sinb_ref[...]
    scale = HEAD_DIM ** -0.5 * LOG2_E

    def seg(name, width=256, c=0):
        o = _NSA_SEG[name] + c * 256
        return _dot(xb, w_ref[:, o:o + width])

    for c in range(NSA_Q_COLS // 256):
        acc = seg("q", c=c)
        for h in range(2):
            r = _rope128(acc[:, h * LANES:(h + 1) * LANES], cos, sina, sinb) * scale
            q_ref[:, c * 256 + h * LANES:c * 256 + (h + 1) * LANES] = r.astype(q_ref.dtype)
    for name, ref in (("ks", ks_ref), ("kw", kw_ref)):
        acc = seg(name)
        for h in range(2):
            r = _rope128(acc[:, h * LANES:(h + 1) * LANES], cos, sina, sinb)
            ref[:, h * LANES:(h + 1) * LANES] = r.astype(ref.dtype)
    n_chunks = kc_ref.shape[0]
    for name, ref in (("kc", kc_ref), ("vc", vc_ref)):
        acc = seg(name)
        for h in range(2):
            chunk_ref[h] = acc[:, h * LANES:(h + 1) * LANES]
        for l in range(CMP_STRIDE):
            for h in range(2):
                lo = l * NSA_KV_COLS + h * LANES
                ref[:, lo:lo + LANES] = chunk_ref[h, pl.ds(l, n_chunks, stride=CMP_STRIDE), :]
    vs_ref[...] = seg("vs").astype(vs_ref.dtype)
    vw_ref[...] = seg("vw").astype(vw_ref.dtype)
    g_ref[...] = jax.nn.sigmoid(seg("gate", width=_NSA_GATE_COLS) + bg_ref[...])


def _nsa_proj(x2, w_perm, bg_perm, cos_t, sina_t, sinb_t):
    t = x2.shape[0]
    tm = ROW_TILE
    row = lambda w: pl.BlockSpec((tm, w), lambda i: (i, 0))
    chunk_cols = CMP_STRIDE * NSA_KV_COLS
    chunk = pl.BlockSpec((tm // CMP_STRIDE, chunk_cols), lambda i: (i, 0))
    outs = [
        jax.ShapeDtypeStruct((t, NSA_Q_COLS), MXU_DTYPE),
        jax.ShapeDtypeStruct((t, NSA_KV_COLS), MXU_DTYPE),
        jax.ShapeDtypeStruct((t, NSA_KV_COLS), MXU_DTYPE),
        jax.ShapeDtypeStruct((t // CMP_STRIDE, chunk_cols), F32),
        jax.ShapeDtypeStruct((t // CMP_STRIDE, chunk_cols), F32),
        jax.ShapeDtypeStruct((t, NSA_KV_COLS), MXU_DTYPE),
        jax.ShapeDtypeStruct((t, NSA_KV_COLS), MXU_DTYPE),
        jax.ShapeDtypeStruct((t, _NSA_GATE_COLS), F32),
    ]
    return pl.pallas_call(
        _nsa_proj_kernel,
        grid=(t // tm,),
        in_specs=[row(D_MODEL), _const_spec((D_MODEL, _NSA_PROJ_COLS)),
                  _const_spec((1, _NSA_GATE_COLS)), row(LANES), row(LANES), row(LANES)],
        out_specs=[row(NSA_Q_COLS), row(NSA_KV_COLS), row(NSA_KV_COLS), chunk, chunk,
                   row(NSA_KV_COLS), row(NSA_KV_COLS), row(_NSA_GATE_COLS)],
        out_shape=outs,
        scratch_shapes=[pltpu.VMEM((2, tm, LANES), F32)],
        compiler_params=_cparams(("arbitrary",)),
        name="nsa_proj",
    )(x2, w_perm, bg_perm, cos_t, sina_t, sinb_t)


def _nsa_compress_kernel(kc_ref, vc_ref, pek_ref, pev_ref, w1k_ref, w1v_ref, w2k_ref, w2v_ref,
                         cos_ref, sina_ref, sinb_ref, ko_ref, vo_ref):
    nc = kc_ref.shape[1]
    half_feat = kc_ref.shape[2]
    rows = lax.broadcasted_iota(jnp.int32, (nc, 1), 0)
    valid = rows < nc - 1

    def compress(tok, pe_ref, w1_ref, w2_ref):
        a = (tok + pe_ref[:, :half_feat]).astype(MXU_DTYPE)
        b = (tok + pe_ref[:, half_feat:]).astype(MXU_DTYPE)
        pa = _dot(a, w1_ref[:half_feat, :])
        pb = _dot(b, w1_ref[half_feat:, :])
        pre = pa + pltpu.roll(pb, nc - 1, 0)
        h = jax.nn.gelu(pre)
        return _dot(h.astype(MXU_DTYPE), w2_ref[...])

    kcmp = compress(kc_ref[0], pek_ref, w1k_ref, w2k_ref)
    vcmp = compress(vc_ref[0], pev_ref, w1v_ref, w2v_ref)
    cos = cos_ref[0]
    sina = sina_ref[0]
    sinb = sinb_ref[0]
    for h in range(NSA_KV_COLS // LANES):
        r = _rope128(kcmp[:, h * LANES:(h + 1) * LANES], cos, sina, sinb)
        ko_ref[0, :, h * LANES:(h + 1) * LANES] = jnp.where(valid, r, 0.0).astype(ko_ref.dtype)
    vo_ref[0] = jnp.where(valid, vcmp, 0.0).astype(vo_ref.dtype)


def _nsa_compress(kc3, vc3, pek, pev, w1k, w1v, w2k, w2v, cos_c, sina_c, sinb_c):
    b, nc, feat = kc3.shape
    tok = pl.BlockSpec((1, nc, feat), lambda i: (i, 0, 0))
    tab = pl.BlockSpec((1, nc, LANES), lambda i: (i, 0, 0))
    out = pl.BlockSpec((1, nc, NSA_KV_COLS), lambda i: (i, 0, 0))
    oshape = jax.ShapeDtypeStruct((b, nc, NSA_KV_COLS), MXU_DTYPE)
    return pl.pallas_call(
        _nsa_compress_kernel,
        grid=(b,),
        in_specs=[tok, tok, _const_spec(pek.shape), _const_spec(pev.shape),
                  _const_spec(w1k.shape), _const_spec(w1v.shape),
                  _const_spec(w2k.shape), _const_spec(w2v.shape), tab, tab, tab],
        out_specs=[out, out],
        out_shape=[oshape, oshape],
        compiler_params=_cparams(("arbitrary",)),
        name="nsa_compress",
    )(kc3, vc3, pek, pev, w1k, w1v, w2k, w2v, cos_c, sina_c, sinb_c)


def _nsa_attn_kernel(q_ref, ks_ref, vs_ref, kw_ref, vw_ref, kc_ref, vc_ref, g_ref, ov_ref, et_ref,
                     o_ref, m_ref, acc_ref, sel_ref, *, n_sel, k_sel):
    tq = q_ref.shape[1]
    tk = ATT_TK
    nc = kc_ref.shape[1]
    R = NSA_HPG
    qi = pl.program_id(1)
    g = pl.program_id(2)
    t0 = qi * tq
    trow = t0 + lax.broadcasted_iota(jnp.int32, (tq, 1), 0)
    gmask = lax.broadcasted_iota(jnp.int32, (1, NSA_KV_COLS), 1) // HEAD_DIM == g
    gates = g_ref[0]
    low_half = g < NSA_GROUPS // 2
    in_group = (lax.broadcasted_iota(jnp.int32, (1, LANES), 1) // HEAD_DIM
                == g % (LANES // HEAD_DIM))

    def value_tile(v):
        return jnp.where(in_group, jnp.where(low_half, v[:, :LANES], v[:, LANES:]), 1)

    qs = jnp.concatenate(
        [jnp.where(gmask, q_ref[0, :, r * NSA_KV_COLS:(r + 1) * NSA_KV_COLS], 0)
         for r in range(R)], axis=0)
    rows = lambda a, r: a[r * tq:(r + 1) * tq]

    def tile_update(c, k_ref, v_ref, k0, bias):
        kt = k_ref[0, pl.ds(k0, tk), :]
        vt = value_tile(v_ref[0, pl.ds(k0, tk), :])
        s_all = _dot_nt(qs, kt)
        ps, alphas = [], []
        for r in range(R):
            s = rows(s_all, r) + bias
            m_old = m_ref[c, r]
            m_new = jnp.maximum(m_old, jnp.max(s, axis=-1, keepdims=True))
            m_ref[c, r] = m_new
            alphas.append(jnp.exp2(m_old - m_new))
            m_wide = jnp.concatenate([m_new] * (tk // LANES), axis=1)
            ps.append(jnp.exp2(s - m_wide).astype(MXU_DTYPE))
        pv = _dot(jnp.concatenate(ps, axis=0), vt)
        for r in range(R):
            acc_ref[c, r] = alphas[r] * acc_ref[c, r] + rows(pv, r)

    n_idx = lax.broadcasted_iota(jnp.int32, (1, nc), 1)
    cmp_ok = (n_idx * CMP_STRIDE + CMP_BLOCK - 1) <= trow
    s_all = _dot_nt(qs, kc_ref[0])
    psum = jnp.zeros((tq, nc), F32)
    ps = []
    for r in range(R):
        s = jnp.where(cmp_ok, rows(s_all, r), NEG_INF)
        e = jnp.where(cmp_ok, jnp.exp2(s - jnp.max(s, axis=-1, keepdims=True)), 0.0)
        p = e * (1.0 / jnp.maximum(jnp.sum(e, axis=-1, keepdims=True), 1e-30))
        psum = psum + p
        ps.append(p.astype(MXU_DTYPE))
    o_cmp = _dot(jnp.concatenate(ps, axis=0), value_tile(vc_ref[0]))

    all_selected = t0 + tq <= k_sel * SEL_BLOCK

    @pl.when(all_selected)
    def _():
        sel_ref[...] = jnp.ones(sel_ref.shape, sel_ref.dtype)

    @pl.when(jnp.logical_not(all_selected))
    def _():
        imp = jnp.dot(psum, ov_ref[...], preferred_element_type=F32,
                      precision=lax.Precision.HIGHEST)
        blk = lax.broadcasted_iota(jnp.int32, (1, LANES), 1)
        rel = trow // SEL_BLOCK - blk
        imp = jnp.where(rel >= 0, imp, -1.0)
        forced = (blk == 0) | (rel == 0) | (rel == 1)
        imp = jnp.where(forced, FORCE_SCORE, imp)
        imp_t = jnp.transpose(imp)[:n_sel, :]
        midx = lax.broadcasted_iota(jnp.int32, (n_sel, 1), 0)
        cnt = jnp.zeros((n_sel, tq), F32)
        for mp in range(n_sel):
            row = imp_t[mp:mp + 1, :]
            beats = (row > imp_t) | ((row == imp_t) & (midx > mp))
            cnt = cnt + jnp.where(beats, 1.0, 0.0)
        sel_t = jnp.where(cnt < k_sel, 1.0, 0.0)
        if n_sel < LANES:
            sel_t = jnp.concatenate([sel_t, jnp.zeros((LANES - n_sel, tq), F32)], axis=0)
        sel_ref[...] = jnp.transpose(sel_t).astype(sel_ref.dtype)

    sel = sel_ref[...]

    SEL, WIN = 0, 1
    m_ref[...] = jnp.full(m_ref.shape, NEG_INF, F32)
    acc_ref[...] = jnp.zeros(acc_ref.shape, F32)

    def sel_update(k0, kpos):
        selm = _dot_nt(sel, et_ref[pl.ds(k0, tk), :])
        bias = jnp.where((selm > 0.5) & (kpos <= trow), 0.0, NEG_INF)
        tile_update(SEL, ks_ref, vs_ref, k0, bias)

    def win_update(k0, kpos):
        d = trow - kpos
        bias = jnp.where((d >= 0) & (d < WINDOW), 0.0, NEG_INF)
        tile_update(WIN, kw_ref, vw_ref, k0, bias)

    def key_tile(j):
        k0 = pl.multiple_of(j * tk, tk)
        return k0, k0 + lax.broadcasted_iota(jnp.int32, (1, tk), 1)

    def sel_only(j, carry):
        sel_update(*key_tile(j))
        return carry

    def sel_and_win(j, carry):
        k0, kpos = key_tile(j)
        sel_update(k0, kpos)
        win_update(k0, kpos)
        return carry

    j_win = jnp.maximum(t0 - WINDOW, 0) // tk
    lax.fori_loop(0, j_win, sel_only, 0)
    lax.fori_loop(j_win, (t0 + tq) // tk, sel_and_win, 0)

    def normalized(c, r):
        acc = acc_ref[c, r]
        return acc * (1.0 / pltpu.roll(acc, HEAD_DIM, 1))

    o_heads = [gates[:, r:r + 1] * rows(o_cmp, r)
               + gates[:, R + r:R + r + 1] * normalized(SEL, r)
               + gates[:, 2 * R + r:2 * R + r + 1] * normalized(WIN, r) for r in range(R)]

    for gg in range(NSA_GROUPS):
        @pl.when(g == gg)
        def _(gg=gg):
            src = gg % (LANES // HEAD_DIM) * HEAD_DIM
            for r in range(R):
                lo = r * NSA_KV_COLS + gg * HEAD_DIM
                o_ref[0, :, lo:lo + HEAD_DIM] = o_heads[r][:, src:src + HEAD_DIM].astype(o_ref.dtype)


def _nsa_attn(q, ks, vs, kw, vw, kcmp, vcmp, gates, ov, et):
    b, s, _ = q.shape
    nc = kcmp.shape[1]
    tq = ATT_TQ
    n_sel = s // SEL_BLOCK
    k_sel = min(SEL_TOP, n_sel)
    qspec = pl.BlockSpec((1, tq, NSA_Q_COLS), lambda i, j, g: (i, j, 0))
    kv = pl.BlockSpec((1, s, NSA_KV_COLS), lambda i, j, g: (i, 0, 0))
    cmp_spec = pl.BlockSpec((1, nc, NSA_KV_COLS), lambda i, j, g: (i, 0, 0))
    gspec = pl.BlockSpec((1, tq, LANES), lambda i, j, g: (i, j, g))
    return pl.pallas_call(
        functools.partial(_nsa_attn_kernel, n_sel=n_sel, k_sel=k_sel),
        grid=(b, s // tq, NSA_GROUPS),
        in_specs=[qspec, kv, kv, kv, kv, cmp_spec, cmp_spec, gspec,
                  _const_spec(ov.shape), _const_spec(et.shape)],
        out_specs=qspec,
        out_shape=jax.ShapeDtypeStruct((b, s, NSA_Q_COLS), MXU_DTYPE),
        scratch_shapes=[pltpu.VMEM((2, NSA_HPG, tq, LANES), F32),
                        pltpu.VMEM((2, NSA_HPG, tq, LANES), F32),
                        pltpu.VMEM((tq, LANES), MXU_DTYPE)],
        compiler_params=_cparams(("arbitrary", "arbitrary", "arbitrary")),
        name="nsa_attn",
    )(q, ks, vs, kw, vw, kcmp, vcmp, gates, ov, et)


def _mm_res_ln_kernel(a_ref, w_ref, x_ref, g_ref, b_ref, o_ref):
    m = _dot(a_ref[...].astype(MXU_DTYPE), w_ref[...])
    o_ref[...] = _layer_norm_rows(DN_ALPHA * x_ref[...] + m, g_ref[...], b_ref[...])


def _mm_res_ln(a, w, x2, g, b):
    t, k = a.shape
    tm = ROW_TILE
    return pl.pallas_call(
        _mm_res_ln_kernel,
        grid=(t // tm,),
        in_specs=[pl.BlockSpec((tm, k), lambda i: (i, 0)), _const_spec(w.shape),
                  pl.BlockSpec((tm, D_MODEL), lambda i: (i, 0)),
                  _const_spec((1, D_MODEL)), _const_spec((1, D_MODEL))],
        out_specs=pl.BlockSpec((tm, D_MODEL), lambda i: (i, 0)),
        out_shape=jax.ShapeDtypeStruct((t, D_MODEL), F32),
        compiler_params=_cparams(("arbitrary",)),
        name="mm_res_ln",
    )(a, w, x2, g, b)


def _shift_rows(v, prev, k):
    out = pltpu.roll(v, k, 0)
    rows = lax.broadcasted_iota(jnp.int32, (v.shape[0], 1), 0)
    np_rows = prev.shape[0]
    for i in range(k):
        out = jnp.where(rows == i, prev[np_rows - k + i:np_rows - k + i + 1, :], out)
    return out


def _shortconv_kernel(x_ref, win_ref, cw_ref, cb_ref, wout_ref, g_ref, b_ref, o_ref, carry_ref):
    @pl.when(pl.program_id(1) == 0)
    def _():
        carry_ref[...] = jnp.zeros(carry_ref.shape, F32)

    x = x_ref[0]
    xb = x.astype(MXU_DTYPE)
    d = D_MODEL
    u = _dot(xb, win_ref[:, 0:d])
    bg = _dot(xb, win_ref[:, d:2 * d])
    cg = _dot(xb, win_ref[:, 2 * d:3 * d])
    v = cg * u
    prev = carry_ref[...]
    z = cb_ref[...] + cw_ref[SC_WIDTH - 1:SC_WIDTH, :] * v
    for k in range(1, SC_WIDTH):
        z = z + cw_ref[SC_WIDTH - 1 - k:SC_WIDTH - k, :] * _shift_rows(v, prev, k)
    carry_ref[...] = v[v.shape[0] - SUBLANES:, :]
    m = _dot((bg * z).astype(MXU_DTYPE), wout_ref[...])
    o_ref[0] = _layer_norm_rows(DN_ALPHA * x + m, g_ref[...], b_ref[...])


def _shortconv_layer(x3, win, cw, cb, wout, g, b):
    bsz, s, d = x3.shape
    tm = ROW_TILE
    xspec = pl.BlockSpec((1, tm, d), lambda i, j: (i, j, 0))
    return pl.pallas_call(
        _shortconv_kernel,
        grid=(bsz, s // tm),
        in_specs=[xspec, _const_spec(win.shape), _const_spec(cw.shape), _const_spec(cb.shape),
                  _const_spec(wout.shape), _const_spec(g.shape), _const_spec(b.shape)],
        out_specs=xspec,
        out_shape=jax.ShapeDtypeStruct(x3.shape, F32),
        scratch_shapes=[pltpu.VMEM((SUBLANES, d), F32)],
        compiler_params=_cparams(("arbitrary", "arbitrary")),
        name="shortconv_layer",
    )(x3, win, cw, cb, wout, g, b)


def _round_down(x, m):
    return x // m * m


def _round_up(x, m):
    return -(-x // m) * m


def _block_diag_dot(u, w_ref):
    tile = 2 * LANES
    outs = []
    for c0 in range(0, LRU_WIDTH, tile):
        c1 = min(c0 + tile, LRU_WIDTH)
        lo = _round_down(_round_down(c0, LRU_BW), LANES)
        hi = min(_round_up(_round_up(c1, LRU_BW), LANES), LRU_WIDTH)
        outs.append(_dot(u[:, lo:hi], w_ref[lo:hi, c0:c1]))
    return jnp.concatenate(outs, axis=1)


def _rglru_kernel(x_ref, win_ref, cw_ref, cb_ref, wa_ref, ba_ref, wx_ref, bx_ref, lam_ref,
                  wout_ref, g_ref, b_ref, o_ref, carry_ref, h_ref, a_s, b_s):
    @pl.when(pl.program_id(1) == 0)
    def _():
        carry_ref[...] = jnp.zeros(carry_ref.shape, F32)
        h_ref[...] = jnp.zeros(h_ref.shape, F32)

    x = x_ref[0]
    tm = x.shape[0]
    xb = x.astype(MXU_DTYPE)
    w = LRU_WIDTH
    gate_br = _dot(xb, win_ref[:, 0:w])
    rec = _dot(xb, win_ref[:, w:2 * w])
    prev = carry_ref[...]
    u = cb_ref[...] + cw_ref[LRU_CONV - 1:LRU_CONV, :] * rec
    for k in range(1, LRU_CONV):
        u = u + cw_ref[LRU_CONV - 1 - k:LRU_CONV - k, :] * _shift_rows(rec, prev, k)
    carry_ref[...] = rec[tm - SUBLANES:, :]
    ub = u.astype(MXU_DTYPE)
    r = jax.nn.sigmoid(_block_diag_dot(ub, wa_ref) + ba_ref[...])
    ig = jax.nn.sigmoid(_block_diag_dot(ub, wx_ref) + bx_ref[...])
    log_a = LRU_C * r * jax.nn.log_sigmoid(lam_ref[...])
    a_s[...] = jnp.exp(log_a)
    th = jnp.tanh(log_a)
    b_s[...] = jnp.sqrt(-2.0 * th / (1.0 - th)) * (ig * u)

    rows = lax.broadcasted_iota(jnp.int32, (SUBLANES, 1), 0)

    def group(i, hc):
        r0 = pl.multiple_of(i * SUBLANES, SUBLANES)
        a = a_s[pl.ds(r0, SUBLANES), :]
        bv = b_s[pl.ds(r0, SUBLANES), :]
        for dlt in (1, 2, 4):
            a_sh = jnp.where(rows >= dlt, pltpu.roll(a, dlt, 0), 1.0)
            b_sh = jnp.where(rows >= dlt, pltpu.roll(bv, dlt, 0), 0.0)
            bv = a * b_sh + bv
            a = a * a_sh
        h = a * hc + bv
        b_s[pl.ds(r0, SUBLANES), :] = h
        return h[SUBLANES - 1:SUBLANES, :]

    h_ref[...] = lax.fori_loop(0, tm // SUBLANES, group, h_ref[...])
    y = jax.nn.gelu(gate_br) * b_s[...]
    m = _dot(y.astype(MXU_DTYPE), wout_ref[...])
    o_ref[0] = _layer_norm_rows(DN_ALPHA * x + m, g_ref[...], b_ref[...])


def _rglru_layer(x3, win, cw, cb, wa, ba, wx, bx, lam, wout, g, b):
    bsz, s, d = x3.shape
    tm = LRU_ROW_TILE
    xspec = pl.BlockSpec((1, tm, d), lambda i, j: (i, j, 0))
    consts = [win, cw, cb, wa, ba, wx, bx, lam, wout, g, b]
    return pl.pallas_call(
        _rglru_kernel,
        grid=(bsz, s // tm),
        in_specs=[xspec] + [_const_spec(c.shape) for c in consts],
        out_specs=xspec,
        out_shape=jax.ShapeDtypeStruct(x3.shape, F32),
        scratch_shapes=[pltpu.VMEM((SUBLANES, LRU_WIDTH), F32), pltpu.VMEM((1, LRU_WIDTH), F32),
                        pltpu.VMEM((tm, LRU_WIDTH), F32), pltpu.VMEM((tm, LRU_WIDTH), F32)],
        compiler_params=_cparams(("arbitrary", "arbitrary")),
        name="rglru_layer",
    )(x3, win, cw, cb, wa, ba, wx, bx, lam, wout, g, b)


def _moe_router_kernel(x_ref, rwt_ref, rb_ref, tri_ref, pos_ref, wts_ref, meta_ref):
    tt = x_ref.shape[0]
    x = x_ref[...]
    w = rwt_ref[...]
    xh = x.astype(MXU_DTYPE)
    wh = w.astype(MXU_DTYPE)
    xl = (x - xh.astype(F32)).astype(MXU_DTYPE)
    wl = (w - wh.astype(F32)).astype(MXU_DTYPE)
    logits = _dot_nt(wh, xh) + _dot_nt(wh, xl) + _dot_nt(wl, xh)
    scores = jax.nn.sigmoid(logits)
    biased = scores + rb_ref[...]
    rows = [biased[e:e + 1, :] for e in range(N_EXPERTS)]

    gscore = []
    for gi in range(N_GROUPS):
        a, b, c, d = rows[gi * EXPERTS_PER_GROUP:(gi + 1) * EXPERTS_PER_GROUP]
        hi1, lo1 = jnp.maximum(a, b), jnp.minimum(a, b)
        hi2, lo2 = jnp.maximum(c, d), jnp.minimum(c, d)
        gscore.append(jnp.maximum(hi1, hi2) + jnp.maximum(jnp.minimum(hi1, hi2),
                                                          jnp.maximum(lo1, lo2)))
    best = gscore[0]
    top_g = jnp.zeros((1, tt), jnp.int32)
    for gi in range(1, N_GROUPS):
        better = gscore[gi] > best
        best = jnp.where(better, gscore[gi], best)
        top_g = jnp.where(better, gi, top_g)

    def argmax_in_group(exclude):
        bv = jnp.full((1, tt), -jnp.inf, F32)
        bi = jnp.full((1, tt), -1, jnp.int32)
        for e in range(N_EXPERTS):
            ok = top_g == (e // EXPERTS_PER_GROUP)
            if exclude is not None:
                ok = ok & (exclude != e)
            better = ok & (rows[e] > bv)
            bv = jnp.where(better, rows[e], bv)
            bi = jnp.where(better, e, bi)
        return bi

    e1 = argmax_in_group(None)
    e2 = argmax_in_group(e1)
    eidx = lax.broadcasted_iota(jnp.int32, (N_EXPERTS, 1), 0)
    oh1 = eidx == e1
    oh2 = eidx == e2
    s1 = jnp.sum(jnp.where(oh1, scores, 0.0), axis=0, keepdims=True)
    s2 = jnp.sum(jnp.where(oh2, scores, 0.0), axis=0, keepdims=True)
    wsum = s1 + s2
    wts_ref[0, 0:1, :] = s1 / wsum
    wts_ref[0, 1:2, :] = s2 / wsum

    oh = jnp.where(oh1 | oh2, 1.0, 0.0)
    csum = _dot(oh.astype(MXU_DTYPE), tri_ref[...])
    excl = csum - oh
    cnt = csum[:, tt - 1:tt]
    padded = jnp.ceil(cnt * (1.0 / MOE_ALIGN)) * MOE_ALIGN
    run = jnp.zeros((1, 1), F32)
    off_rows = []
    for e in range(N_EXPERTS):
        off_rows.append(run)
        run = run + padded[e:e + 1, :]
    off = jnp.concatenate(off_rows, axis=0)
    slot = off + excl

    def flat_row_of(onehot):
        p = jnp.sum(jnp.where(onehot, slot, 0.0), axis=0, keepdims=True).astype(jnp.int32)
        return _flat_row(p)

    pos_ref[0, 0:1, :] = flat_row_of(oh1)
    pos_ref[0, 1:2, :] = flat_row_of(oh2)
    lane = lax.broadcasted_iota(jnp.int32, (1, LANES), 1)
    meta = jnp.zeros((1, LANES), F32)
    for e in range(N_EXPERTS):
        meta = jnp.where(lane == e, off[e:e + 1, :], meta)
        meta = jnp.where(lane == N_EXPERTS + e, cnt[e:e + 1, :], meta)
    meta_ref[0] = meta.astype(jnp.int32)


def _moe_router(x2, rwt, rb, tri):
    t = x2.shape[0]
    tt = tri.shape[0]
    nt = t // tt
    return pl.pallas_call(
        _moe_router_kernel,
        grid=(nt,),
        in_specs=[pl.BlockSpec((tt, D_MODEL), lambda i: (i, 0)), _const_spec(rwt.shape),
                  _const_spec(rb.shape), _const_spec(tri.shape)],
        out_specs=[pl.BlockSpec((1, TOP_K, tt), lambda i: (i, 0, 0)),
                   pl.BlockSpec((1, TOP_K, tt), lambda i: (i, 0, 0)),
                   pl.BlockSpec((1, 1, LANES), lambda i: (i, 0, 0))],
        out_shape=[jax.ShapeDtypeStruct((nt, TOP_K, tt), jnp.int32),
                   jax.ShapeDtypeStruct((nt, TOP_K, tt), F32),
                   jax.ShapeDtypeStruct((nt, 1, LANES), jnp.int32)],
        compiler_params=_cparams(("arbitrary",)),
        name="moe_router",
    )(x2, rwt, rb, tri)


def _flat_row(r):
    return (r >> _SUBLANE_SHIFT) * _TILE_ROW + (r & (SUBLANES - 1))


def _row_slice(q):
    return pl.ds(q, _LANE_TILES, stride=SUBLANES)


def _load_rows_flat(ref, r0, n):
    base = pl.multiple_of(r0 * _LANE_TILES, _TILE_ROW)
    cols = [jnp.concatenate([ref[pl.ds(base + i * _TILE_ROW + c * SUBLANES, SUBLANES), :]
                             for i in range(n // SUBLANES)], axis=0) for c in range(_LANE_TILES)]
    return jnp.concatenate(cols, axis=1)


def _store_rows_flat(ref, r0, val):
    base = pl.multiple_of(r0 * _LANE_TILES, _TILE_ROW)
    for c in range(_LANE_TILES):
        for i in range(val.shape[0] // SUBLANES):
            ref[pl.ds(base + i * _TILE_ROW + c * SUBLANES, SUBLANES), :] = (
                val[i * SUBLANES:(i + 1) * SUBLANES, c * LANES:(c + 1) * LANES])


def _moe_ffn_kernel(pos_ref, wts_ref, meta_ref, x_ref, wg_ref, wu_ref, wd_ref, g_ref, b_ref,
                    o_ref, xs_ref, xf_ref):
    tt = x_ref.shape[0]
    ch = MOE_CHUNK
    e = pl.program_id(1)

    def token_group(body):
        def group(i, carry):
            base = pl.multiple_of(i * SUBLANES, SUBLANES)
            qbase = pl.multiple_of(i * _TILE_ROW, _TILE_ROW)
            for k in range(SUBLANES):
                body(base + k, qbase + k)
            return carry

        lax.fori_loop(0, tt // SUBLANES, group, 0)

    @pl.when(e == 0)
    def _dispatch():
        def to_flat(i, carry):
            r0 = pl.multiple_of(i * SUBLANES, SUBLANES)
            _store_rows_flat(xf_ref, r0, x_ref[pl.ds(r0, SUBLANES), :])
            return carry

        lax.fori_loop(0, tt // SUBLANES, to_flat, 0)

        def zero_gap(ei, carry):
            start = meta_ref[ei] + meta_ref[N_EXPERTS + ei]
            stop = jnp.where(ei == N_EXPERTS - 1, start + ch + MOE_ALIGN,
                             meta_ref[jnp.minimum(ei + 1, N_EXPERTS - 1)])

            def zrow(r, c):
                xs_ref[_row_slice(_flat_row(r)), :] = jnp.zeros((_LANE_TILES, LANES), F32)
                return c

            return lax.fori_loop(start, stop, zrow, carry)

        lax.fori_loop(0, N_EXPERTS, zero_gap, 0)

        def scatter(t, q):
            row = xf_ref[_row_slice(q), :]
            xs_ref[_row_slice(pos_ref[t]), :] = row
            xs_ref[_row_slice(pos_ref[tt + t]), :] = row

        token_group(scatter)

    off = meta_ref[e]
    cnt = meta_ref[N_EXPERTS + e]
    rid = lax.broadcasted_iota(jnp.int32, (ch, 1), 0)

    def chunk(c, carry):
        r0 = pl.multiple_of(off + c * ch, MOE_ALIGN)
        xs = _load_rows_flat(xs_ref, r0, ch)
        xb = xs.astype(MXU_DTYPE)
        hid = jax.nn.silu(_dot(xb, wg_ref[0, 0])) * _dot(xb, wu_ref[0, 0])
        out = _dot(hid.astype(MXU_DTYPE), wd_ref[0, 0])
        _store_rows_flat(xs_ref, r0, jnp.where(rid + c * ch < cnt, out, xs))
        return carry

    lax.fori_loop(0, (cnt + ch - 1) // ch, chunk, 0)

    @pl.when(e == N_EXPERTS - 1)
    def _combine():
        def gather(t, q):
            xf_ref[_row_slice(q), :] = (wts_ref[t] * xs_ref[_row_slice(pos_ref[t]), :]
                                        + wts_ref[tt + t] * xs_ref[_row_slice(pos_ref[tt + t]), :])

        token_group(gather)

        def norm(i, carry):
            r0 = pl.multiple_of(i * MOE_NORM_ROWS, MOE_NORM_ROWS)
            y = (DN_ALPHA * x_ref[pl.ds(r0, MOE_NORM_ROWS), :]
                 + _load_rows_flat(xf_ref, r0, MOE_NORM_ROWS))
            o_ref[pl.ds(r0, MOE_NORM_ROWS), :] = _layer_norm_rows(y, g_ref[...], b_ref[...])
            return carry

        lax.fori_loop(0, tt // MOE_NORM_ROWS, norm, 0)


def _moe_ffn(pos, wts, meta, x2, layer, wg, wu, wd, g, b):
    t = x2.shape[0]
    tt = pos.shape[2]
    nt = t // tt
    smem = lambda n: pl.BlockSpec((n,), lambda i, e: (i,), memory_space=pltpu.SMEM)
    wspec = lambda w: pl.BlockSpec((1, 1) + w.shape[2:], lambda i, e: (layer, e, 0, 0))
    xs_rows = TOP_K * tt + N_EXPERTS * MOE_ALIGN + MOE_CHUNK + MOE_ALIGN
    pos, wts, meta = pos.reshape(-1), wts.reshape(-1), meta.reshape(-1)
    return pl.pallas_call(
        _moe_ffn_kernel,
        grid=(nt, N_EXPERTS),
        in_specs=[smem(TOP_K * tt), smem(TOP_K * tt), smem(LANES),
                  pl.BlockSpec((tt, D_MODEL), lambda i, e: (i, 0), pipeline_mode=pl.Buffered(1)),
                  wspec(wg), wspec(wu), wspec(wd),
                  pl.BlockSpec((1, D_MODEL), lambda i, e: (0, 0)),
                  pl.BlockSpec((1, D_MODEL), lambda i, e: (0, 0))],
        out_specs=pl.BlockSpec((tt, D_MODEL), lambda i, e: (i, 0), pipeline_mode=pl.Buffered(1)),
        out_shape=jax.ShapeDtypeStruct((t, D_MODEL), F32),
        scratch_shapes=[pltpu.VMEM((xs_rows * _LANE_TILES, LANES), F32),
                        pltpu.VMEM((tt * _LANE_TILES, LANES), F32)],
        compiler_params=_cparams(("arbitrary", "arbitrary")),
        name="moe_ffn",
    )(pos, wts, meta, x2, wg, wu, wd, g, b)


def _nsa_weight_layout(w_in, b_gate):
    G, R, Dh = NSA_GROUPS, NSA_HPG, HEAD_DIM
    d_in = w_in.shape[0]
    kvw = NSA_KV_COLS
    wq = w_in[:, :NSA_Q_COLS].reshape(d_in, G, R, Dh).transpose(0, 2, 1, 3).reshape(d_in, NSA_Q_COLS)
    kc, vc, ks, vs, kw, vw = (w_in[:, NSA_Q_COLS + i * kvw:NSA_Q_COLS + (i + 1) * kvw] for i in range(6))
    gate_base = NSA_Q_COLS + 6 * kvw

    def gate_layout(v):
        lead = v.shape[:-1]
        v = jnp.swapaxes(v.reshape(lead + (G, R, 3)), -1, -2).reshape(lead + (G, 3 * R))
        v = jnp.pad(v, [(0, 0)] * (len(lead) + 1) + [(0, LANES - 3 * R)])
        return v.reshape(lead + (_NSA_GATE_COLS,))

    wg = gate_layout(w_in[:, gate_base:gate_base + 3 * NSA_HEADS])
    w = jnp.concatenate([wq, ks, kw, kc, vc, vs, vw, wg], axis=1).astype(MXU_DTYPE)
    return w, gate_layout(b_gate)[None, :]


def _nsa_out_layout(w_out):
    G, R, Dh = NSA_GROUPS, NSA_HPG, HEAD_DIM
    d_out = w_out.shape[1]
    w = w_out.reshape(G, R, Dh, d_out).transpose(1, 0, 2, 3).reshape(NSA_Q_COLS, d_out)
    return w.astype(MXU_DTYPE)


def _compress_weight_layout(pe, w1, w2):
    G, Dh, L = NSA_GROUPS, HEAD_DIM, CMP_BLOCK
    eye = jnp.eye(G, dtype=w1.dtype)
    w1r = w1.reshape(L, Dh, Dh)
    w1big = jnp.einsum("ldj,gh->lgdhj", w1r, eye).reshape(L * G * Dh, G * Dh)
    w2big = jnp.einsum("dj,gh->gdhj", w2, eye).reshape(G * Dh, G * Dh)
    pebig = jnp.broadcast_to(pe[:, None, :], (L, G, Dh)).reshape(1, L * G * Dh)
    return pebig, w1big.astype(MXU_DTYPE), w2big.astype(MXU_DTYPE)


def _block_diag(w):
    h, bi, bj = w.shape
    eye = jnp.eye(h, dtype=w.dtype)
    return jnp.einsum("hij,hk->hikj", w, eye).reshape(h * bi, h * bj).astype(MXU_DTYPE)


def _nsa_constants(s):
    nc = s // CMP_STRIDE
    n_cmp = (s - CMP_BLOCK) // CMP_STRIDE + 1
    n_sel = s // SEL_BLOCK
    cmp_start = np.arange(nc) * CMP_STRIDE
    cmp_end = cmp_start + CMP_BLOCK - 1
    sel_start = np.arange(n_sel) * SEL_BLOCK
    ov = ((cmp_start[:, None] <= sel_start[None, :] + SEL_BLOCK - 1)
          & (cmp_end[:, None] >= sel_start[None, :])
          & (np.arange(nc)[:, None] < n_cmp)).astype(np.float32)
    ov = np.pad(ov, ((0, 0), (0, LANES - n_sel)))
    et = (np.arange(s)[:, None] // SEL_BLOCK == np.arange(LANES)[None, :]).astype(np.float32)
    return jnp.asarray(ov), jnp.asarray(et, dtype=MXU_DTYPE)


def _nsa_layer(x2, bsz, s, tables, cmp_tables, w_in, b_gate, pos_k, w1_k, w2_k, pos_v, w1_v, w2_v,
               w_out, g, b):
    w_perm, bg_perm = _nsa_weight_layout(w_in, b_gate)
    q, ks, kw, kc, vc, vs, vw, gates = _nsa_proj(x2, w_perm, bg_perm, *tables)
    nc = s // CMP_STRIDE
    feat = CMP_STRIDE * NSA_KV_COLS
    pek, w1k, w2k = _compress_weight_layout(pos_k, w1_k, w2_k)
    pev, w1v, w2v = _compress_weight_layout(pos_v, w1_v, w2_v)
    kcmp, vcmp = _nsa_compress(kc.reshape(bsz, nc, feat), vc.reshape(bsz, nc, feat),
                               pek, pev, w1k, w1v, w2k, w2v, *cmp_tables)
    ov, et = _nsa_constants(s)
    r3 = lambda a: a.reshape(bsz, s, a.shape[-1])
    o = _nsa_attn(r3(q), r3(ks), r3(vs), r3(kw), r3(vw), kcmp, vcmp, r3(gates), ov, et)
    return _mm_res_ln(o.reshape(bsz * s, NSA_Q_COLS), _nsa_out_layout(w_out), x2, g, b)


def _moe_layer(x2, rwt, rb, tri, layer, wg, wu, wd, g, b):
    pos, wts, meta = _moe_router(x2, rwt, rb, tri)
    return _moe_ffn(pos, wts, meta, x2, layer, wg, wu, wd, g, b)


def kernel(x, positions, nsa_w_in, nsa_b_gate, nsa_cmp_pos_k, nsa_cmp_w1_k, nsa_cmp_w2_k, nsa_cmp_pos_v, nsa_cmp_w1_v, nsa_cmp_w2_v, nsa_w_out, sc_w_in, sc_conv_w, sc_conv_b, sc_w_out, lru_w_in, lru_conv_w, lru_conv_b, lru_wa, lru_ba, lru_wx, lru_bx, lru_lambda, lru_w_out, router_w, router_b, moe_w_gate, moe_w_up, moe_w_down, ln_g, ln_b):
    bsz, s, d = x.shape
    t = bsz * s
    x2 = x.reshape(t, d)
    row = lambda v: v.reshape(1, -1)

    tables = _rope_tables(positions.reshape(t))
    nc = s // CMP_STRIDE
    cmp_idx = np.minimum(np.arange(nc) * CMP_STRIDE + CMP_BLOCK - 1, s - 1)
    cmp_tables = tuple(tb.reshape(bsz, nc, LANES)
                       for tb in _rope_tables(positions[:, cmp_idx].reshape(bsz * nc)))

    tt = min(MOE_TOKENS, t)
    tri = jnp.asarray(np.triu(np.ones((tt, tt), np.float32)), dtype=MXU_DTYPE)
    rwt = router_w.T
    rb = router_b.reshape(N_EXPERTS, 1)
    moe_wg, moe_wu, moe_wd = (w.astype(MXU_DTYPE) for w in (moe_w_gate, moe_w_up, moe_w_down))

    for i in range(DEPTH):
        kind, j = i % N_MIXERS, i // N_MIXERS
        g0, b0 = row(ln_g[i, 0]), row(ln_b[i, 0])
        if kind == 0:
            x2 = _nsa_layer(x2, bsz, s, tables, cmp_tables, nsa_w_in[j], nsa_b_gate[j],
                            nsa_cmp_pos_k[j], nsa_cmp_w1_k[j], nsa_cmp_w2_k[j],
                            nsa_cmp_pos_v[j], nsa_cmp_w1_v[j], nsa_cmp_w2_v[j], nsa_w_out[j], g0, b0)
        elif kind == 1:
            x2 = _shortconv_layer(x2.reshape(bsz, s, d), sc_w_in[j].astype(MXU_DTYPE), sc_conv_w[j],
                                  row(sc_conv_b[j]), sc_w_out[j].astype(MXU_DTYPE), g0, b0).reshape(t, d)
        else:
            x2 = _rglru_layer(x2.reshape(bsz, s, d), lru_w_in[j].astype(MXU_DTYPE), lru_conv_w[j],
                              row(lru_conv_b[j]), _block_diag(lru_wa[j]), row(lru_ba[j]),
                              _block_diag(lru_wx[j]), row(lru_bx[j]), row(lru_lambda[j]),
                              lru_w_out[j].astype(MXU_DTYPE), g0, b0).reshape(t, d)
        x2 = _moe_layer(x2, rwt, rb, tri, i, moe_wg, moe_wu, moe_wd, row(ln_g[i, 1]), row(ln_b[i, 1]))
    return x2.reshape(bsz, s, d)
```

```python
import functools

import numpy as np
import jax
import jax.numpy as jnp
from jax import lax
from jax.experimental import pallas as pl
from jax.experimental.pallas import tpu as pltpu

F32 = jnp.float32
BF16 = jnp.bfloat16
MXU_DTYPE = BF16

D_MODEL = 1024
DEPTH = 4
N_MIXERS = 3
NSA_HEADS = 16
NSA_GROUPS = 4
NSA_HPG = NSA_HEADS // NSA_GROUPS
HEAD_DIM = D_MODEL // NSA_HEADS
CMP_BLOCK = 32
CMP_STRIDE = 16
SEL_BLOCK = 64
SEL_TOP = 16
WINDOW = 512
ROPE_THETA = 10000.0
FORCE_SCORE = 1e6
NSA_Q_COLS = NSA_HEADS * HEAD_DIM
NSA_KV_COLS = NSA_GROUPS * HEAD_DIM
SC_WIDTH = 3
LRU_WIDTH = 1280
LRU_BLOCKS = 16
LRU_BW = LRU_WIDTH // LRU_BLOCKS
LRU_CONV = 4
LRU_C = 8.0
N_EXPERTS = 16
N_GROUPS = 4
EXPERTS_PER_GROUP = N_EXPERTS // N_GROUPS
TOP_K = 2
D_FF = 1024
DN_ALPHA = (2.0 * DEPTH) ** 0.25
LN_EPS = 1e-5
NEG_INF = -1e30
LOG2_E = 1.4426950408889634

LANES = 128
SUBLANES = 8
VMEM_LIMIT_BYTES = 56 * 1024 * 1024
ROW_TILE = 512
LRU_ROW_TILE = 256
ATT_TQ = 512
ATT_TK = 512
MOE_TOKENS = 2048
MOE_CHUNK = 288
MOE_NORM_ROWS = 128
MOE_ALIGN = SUBLANES
_SUBLANE_SHIFT = 3
_LANE_TILES = D_MODEL // LANES
_TILE_ROW = SUBLANES * _LANE_TILES


def _cparams(sem):
    return pltpu.CompilerParams(dimension_semantics=sem, vmem_limit_bytes=VMEM_LIMIT_BYTES)


def _dot(a, b):
    return jnp.dot(a, b, preferred_element_type=F32)


def _dot_nt(a, b, precision=None):
    return lax.dot_general(a, b, (((1,), (1,)), ((), ())), preferred_element_type=F32,
                           precision=precision)


def _layer_norm_rows(y, g, b):
    mu = jnp.mean(y, axis=-1, keepdims=True)
    d = y - mu
    var = jnp.mean(d * d, axis=-1, keepdims=True)
    return d * lax.rsqrt(var + LN_EPS) * g + b


def _const_spec(shape):
    nd = len(shape)
    return pl.BlockSpec(shape, lambda *_: (0,) * nd)


def _rope_tables_kernel(pos_ref, inv_ref, cos_ref, sina_ref, sinb_ref):
    ang = pos_ref[...] * inv_ref[...]
    c = jnp.cos(ang)
    s = jnp.sin(ang)
    lane = lax.broadcasted_iota(jnp.int32, ang.shape, 1)
    first = (lane % HEAD_DIM) < (HEAD_DIM // 2)
    cos_ref[...] = c
    sina_ref[...] = jnp.where(first, -s, 0.0)
    sinb_ref[...] = jnp.where(first, 0.0, s)


def _rope_tables(pos_flat):
    n = pos_flat.shape[0]
    tm = min(n, 1024)
    half = HEAD_DIM // 2
    inv = ROPE_THETA ** (-jnp.arange(half, dtype=F32) / half)
    inv_lane = jnp.tile(inv, LANES // half)[None, :]
    posb = jnp.broadcast_to(pos_flat.astype(F32)[:, None], (n, LANES))
    spec = pl.BlockSpec((tm, LANES), lambda i: (i, 0))
    out = jax.ShapeDtypeStruct((n, LANES), F32)
    return pl.pallas_call(
        _rope_tables_kernel,
        grid=(n // tm,),
        in_specs=[spec, _const_spec((1, LANES))],
        out_specs=[spec, spec, spec],
        out_shape=[out, out, out],
        compiler_params=_cparams(("arbitrary",)),
        name="rope_tables",
    )(posb, inv_lane)


def _rope128(v, cos, sina, sinb):
    half = HEAD_DIM // 2
    return (v * cos + pltpu.roll(v, LANES - half, 1) * sina + pltpu.roll(v, half, 1) * sinb)


_NSA_SEG = {
    "q": 0, "ks": 1024, "kw": 1280, "kc": 1536, "vc": 1792, "vs": 2048, "vw": 2304, "gate": 2560,
}
_NSA_GATE_COLS = NSA_GROUPS * LANES
_NSA_PROJ_COLS = 2560 + _NSA_GATE_COLS


def _nsa_proj_kernel(x_ref, w_ref, bg_ref, cos_ref, sina_ref, sinb_ref,
                     q_ref, ks_ref, kw_ref, kc_ref, vc_ref, vs_ref, vw_ref, g_ref, chunk_ref):
    xb = x_ref[...].astype(MXU_DTYPE)
    cos = cos_ref[...]
    sina = sina_ref[...]
    sinb = sinb_ref[...]
    scale = HEAD_DIM ** -0.5 * LOG2_E

    def seg(name, width=256, c=0):
        o = _NSA_SEG[name] + c * 256
        return _dot(xb, w_ref[:, o:o + width])

    for c in range(NSA_Q_COLS // 256):
        acc = seg("q", c=c)
        for h in range(2):
            r = _rope128(acc[:, h * LANES:(h + 1) * LANES], cos, sina, sinb) * scale
            q_ref[:, c * 256 + h * LANES:c * 256 + (h + 1) * LANES] = r.astype(q_ref.dtype)
    for name, ref in (("ks", ks_ref), ("kw", kw_ref)):
        acc = seg(name)
        for h in range(2):
            r = _rope128(acc[:, h * LANES:(h + 1) * LANES], cos, sina, sinb)
            ref[:, h * LANES:(h + 1) * LANES] = r.astype(ref.dtype)
    n_chunks = kc_ref.shape[0]
    for name, ref in (("kc", kc_ref), ("vc", vc_ref)):
        acc = seg(name)
        for h in range(2):
            chunk_ref[h] = acc[:, h * LANES:(h + 1) * LANES]
        for l in range(CMP_STRIDE):
            for h in range(2):
                lo = l * NSA_KV_COLS + h * LANES
                ref[:, lo:lo + LANES] = chunk_ref[h, pl.ds(l, n_chunks, stride=CMP_STRIDE), :]
    vs_ref[...] = seg("vs").astype(vs_ref.dtype)
    vw_ref[...] = seg("vw").astype(vw_ref.dtype)
    g_ref[...] = jax.nn.sigmoid(seg("gate", width=_NSA_GATE_COLS) + bg_ref[...])


def _nsa_proj(x2, w_perm, bg_perm, cos_t, sina_t, sinb_t):
    t = x2.shape[0]
    tm = ROW_TILE
    row = lambda w: pl.BlockSpec((tm, w), lambda i: (i, 0))
    chunk_cols = CMP_STRIDE * NSA_KV_COLS
    chunk = pl.BlockSpec((tm // CMP_STRIDE, chunk_cols), lambda i: (i, 0))
    outs = [
        jax.ShapeDtypeStruct((t, NSA_Q_COLS), MXU_DTYPE),
        jax.ShapeDtypeStruct((t, NSA_KV_COLS), MXU_DTYPE),
        jax.ShapeDtypeStruct((t, NSA_KV_COLS), MXU_DTYPE),
        jax.ShapeDtypeStruct((t // CMP_STRIDE, chunk_cols), F32),
        jax.ShapeDtypeStruct((t // CMP_STRIDE, chunk_cols), F32),
        jax.ShapeDtypeStruct((t, NSA_KV_COLS), MXU_DTYPE),
        jax.ShapeDtypeStruct((t, NSA_KV_COLS), MXU_DTYPE),
        jax.ShapeDtypeStruct((t, _NSA_GATE_COLS), F32),
    ]
    return pl.pallas_call(
        _nsa_proj_kernel,
        grid=(t // tm,),
        in_specs=[row(D_MODEL), _const_spec((D_MODEL, _NSA_PROJ_COLS)),
                  _const_spec((1, _NSA_GATE_COLS)), row(LANES), row(LANES), row(LANES)],
        out_specs=[row(NSA_Q_COLS), row(NSA_KV_COLS), row(NSA_KV_COLS), chunk, chunk,
                   row(NSA_KV_COLS), row(NSA_KV_COLS), row(_NSA_GATE_COLS)],
        out_shape=outs,
        scratch_shapes=[pltpu.VMEM((2, tm, LANES), F32)],
        compiler_params=_cparams(("arbitrary",)),
        name="nsa_proj",
    )(x2, w_perm, bg_perm, cos_t, sina_t, sinb_t)


def _nsa_compress_kernel(kc_ref, vc_ref, pek_ref, pev_ref, w1k_ref, w1v_ref, w2k_ref, w2v_ref,
                         cos_ref, sina_ref, sinb_ref, ko_ref, vo_ref):
    nc = kc_ref.shape[1]
    half_feat = kc_ref.shape[2]
    rows = lax.broadcasted_iota(jnp.int32, (nc, 1), 0)
    valid = rows < nc - 1

    def compress(tok, pe_ref, w1_ref, w2_ref):
        a = (tok + pe_ref[:, :half_feat]).astype(MXU_DTYPE)
        b = (tok + pe_ref[:, half_feat:]).astype(MXU_DTYPE)
        pa = _dot(a, w1_ref[:half_feat, :])
        pb = _dot(b, w1_ref[half_feat:, :])
        pre = pa + pltpu.roll(pb, nc - 1, 0)
        h = jax.nn.gelu(pre)
        return _dot(h.astype(MXU_DTYPE), w2_ref[...])

    kcmp = compress(kc_ref[0], pek_ref, w1k_ref, w2k_ref)
    vcmp = compress(vc_ref[0], pev_ref, w1v_ref, w2v_ref)
    cos = cos_ref[0]
    sina = sina_ref[0]
    sinb = sinb_ref[0]
    for h in range(NSA_KV_COLS // LANES):
        r = _rope128(kcmp[:, h * LANES:(h + 1) * LANES], cos, sina, sinb)
        ko_ref[0, :, h * LANES:(h + 1) * LANES] = jnp.where(valid, r, 0.0).astype(ko_ref.dtype)
    vo_ref[0] = jnp.where(valid, vcmp, 0.0).astype(vo_ref.dtype)


def _nsa_compress(kc3, vc3, pek, pev, w1k, w1v, w2k, w2v, cos_c, sina_c, sinb_c):
    b, nc, feat = kc3.shape
    tok = pl.BlockSpec((1, nc, feat), lambda i: (i, 0, 0))
    tab = pl.BlockSpec((1, nc, LANES), lambda i: (i, 0, 0))
    out = pl.BlockSpec((1, nc, NSA_KV_COLS), lambda i: (i, 0, 0))
    oshape = jax.ShapeDtypeStruct((b, nc, NSA_KV_COLS), MXU_DTYPE)
    return pl.pallas_call(
        _nsa_compress_kernel,
        grid=(b,),
        in_specs=[tok, tok, _const_spec(pek.shape), _const_spec(pev.shape),
                  _const_spec(w1k.shape), _const_spec(w1v.shape),
                  _const_spec(w2k.shape), _const_spec(w2v.shape), tab, tab, tab],
        out_specs=[out, out],
        out_shape=[oshape, oshape],
        compiler_params=_cparams(("arbitrary",)),
        name="nsa_compress",
    )(kc3, vc3, pek, pev, w1k, w1v, w2k, w2v, cos_c, sina_c, sinb_c)


def _nsa_attn_kernel(q_ref, ks_ref, vs_ref, kw_ref, vw_ref, kc_ref, vc_ref, g_ref, ov_ref, et_ref,
                     o_ref, m_ref, acc_ref, sel_ref, *, n_sel, k_sel):
    tq = q_ref.shape[1]
    tk = ATT_TK
    nc = kc_ref.shape[1]
    R = NSA_HPG
    qi = pl.program_id(1)
    g = pl.program_id(2)
    t0 = qi * tq
    trow = t0 + lax.broadcasted_iota(jnp.int32, (tq, 1), 0)
    gmask = lax.broadcasted_iota(jnp.int32, (1, NSA_KV_COLS), 1) // HEAD_DIM == g
    gates = g_ref[0]
    low_half = g < NSA_GROUPS // 2
    in_group = (lax.broadcasted_iota(jnp.int32, (1, LANES), 1) // HEAD_DIM
                == g % (LANES // HEAD_DIM))

    def value_tile(v):
        return jnp.where(in_group, jnp.where(low_half, v[:, :LANES], v[:, LANES:]), 1)

    qs = jnp.concatenate(
        [jnp.where(gmask, q_ref[0, :, r * NSA_KV_COLS:(r + 1) * NSA_KV_COLS], 0)
         for r in range(R)], axis=0)
    rows = lambda a, r: a[r * tq:(r + 1) * tq]

    def tile_update(c, k_ref, v_ref, k0, bias):
        kt = k_ref[0, pl.ds(k0, tk), :]
        vt = value_tile(v_ref[0, pl.ds(k0, tk), :])
        s_all = _dot_nt(qs, kt)
        ps, alphas = [], []
        for r in range(R):
            s = rows(s_all, r) + bias
            m_old = m_ref[c, r]
            m_new = jnp.maximum(m_old, jnp.max(s, axis=-1, keepdims=True))
            m_ref[c, r] = m_new
            alphas.append(jnp.exp2(m_old - m_new))
            m_wide = jnp.concatenate([m_new] * (tk // LANES), axis=1)
            ps.append(jnp.exp2(s - m_wide).astype(MXU_DTYPE))
        pv = _dot(jnp.concatenate(ps, axis=0), vt)
        for r in range(R):
            acc_ref[c, r] = alphas[r] * acc_ref[c, r] + rows(pv, r)

    n_idx = lax.broadcasted_iota(jnp.int32, (1, nc), 1)
    cmp_ok = (n_idx * CMP_STRIDE + CMP_BLOCK - 1) <= trow
    s_all = _dot_nt(qs, kc_ref[0])
    psum = jnp.zeros((tq, nc), F32)
    ps = []
    for r in range(R):
        s = jnp.where(cmp_ok, rows(s_all, r), NEG_INF)
        e = jnp.where(cmp_ok, jnp.exp2(s - jnp.max(s, axis=-1, keepdims=True)), 0.0)
        p = e * (1.0 / jnp.maximum(jnp.sum(e, axis=-1, keepdims=True), 1e-30))
        psum = psum + p
        ps.append(p.astype(MXU_DTYPE))
    o_cmp = _dot(jnp.concatenate(ps, axis=0), value_tile(vc_ref[0]))

    all_selected = t0 + tq <= k_sel * SEL_BLOCK

    @pl.when(all_selected)
    def _():
        sel_ref[...] = jnp.ones(sel_ref.shape, sel_ref.dtype)

    @pl.when(jnp.logical_not(all_selected))
    def _():
        imp = jnp.dot(psum, ov_ref[...], preferred_element_type=F32,
                      precision=lax.Precision.HIGHEST)
        blk = lax.broadcasted_iota(jnp.int32, (1, LANES), 1)
        rel = trow // SEL_BLOCK - blk
        imp = jnp.where(rel >= 0, imp, -1.0)
        forced = (blk == 0) | (rel == 0) | (rel == 1)
        imp = jnp.where(forced, FORCE_SCORE, imp)
        imp_t = jnp.transpose(imp)[:n_sel, :]
        midx = lax.broadcasted_iota(jnp.int32, (n_sel, 1), 0)
        cnt = jnp.zeros((n_sel, tq), F32)
        for mp in range(n_sel):
            row = imp_t[mp:mp + 1, :]
            beats = (row > imp_t) | ((row == imp_t) & (midx > mp))
            cnt = cnt + jnp.where(beats, 1.0, 0.0)
        sel_t = jnp.where(cnt < k_sel, 1.0, 0.0)
        if n_sel < LANES:
            sel_t = jnp.concatenate([sel_t, jnp.zeros((LANES - n_sel, tq), F32)], axis=0)
        sel_ref[...] = jnp.transpose(sel_t).astype(sel_ref.dtype)

    sel = sel_ref[...]

    SEL, WIN = 0, 1
    m_ref[...] = jnp.full(m_ref.shape, NEG_INF, F32)
    acc_ref[...] = jnp.zeros(acc_ref.shape, F32)

    def sel_update(k0, kpos):
        selm = _dot_nt(sel, et_ref[pl.ds(k0, tk), :])
        bias = jnp.where((selm > 0.5) & (kpos <= trow), 0.0, NEG_INF)
        tile_update(SEL, ks_ref, vs_ref, k0, bias)

    def win_update(k0, kpos):
        d = trow - kpos
        bias = jnp.where((d >= 0) & (d < WINDOW), 0.0, NEG_INF)
        tile_update(WIN, kw_ref, vw_ref, k0, bias)

    def key_tile(j):
        k0 = pl.multiple_of(j * tk, tk)
        return k0, k0 + lax.broadcasted_iota(jnp.int32, (1, tk), 1)

    def sel_only(j, carry):
        sel_update(*key_tile(j))
        return carry

    def sel_and_win(j, carry):
        k0, kpos = key_tile(j)
        sel_update(k0, kpos)
        win_update(k0, kpos)
        return carry

    j_win = jnp.maximum(t0 - WINDOW, 0) // tk
    lax.fori_loop(0, j_win, sel_only, 0)
    lax.fori_loop(j_win, (t0 + tq) // tk, sel_and_win, 0)

    def normalized(c, r):
        acc = acc_ref[c, r]
        return acc * (1.0 / pltpu.roll(acc, HEAD_DIM, 1))

    o_heads = [gates[:, r:r + 1] * rows(o_cmp, r)
               + gates[:, R + r:R + r + 1] * normalized(SEL, r)
               + gates[:, 2 * R + r:2 * R + r + 1] * normalized(WIN, r) for r in range(R)]

    for gg in range(NSA_GROUPS):
        @pl.when(g == gg)
        def _(gg=gg):
            src = gg % (LANES // HEAD_DIM) * HEAD_DIM
            for r in range(R):
                lo = r * NSA_KV_COLS + gg * HEAD_DIM
                o_ref[0, :, lo:lo + HEAD_DIM] = o_heads[r][:, src:src + HEAD_DIM].astype(o_ref.dtype)


def _nsa_attn(q, ks, vs, kw, vw, kcmp, vcmp, gates, ov, et):
    b, s, _ = q.shape
    nc = kcmp.shape[1]
    tq = ATT_TQ
    n_sel = s // SEL_BLOCK
    k_sel = min(SEL_TOP, n_sel)
    qspec = pl.BlockSpec((1, tq, NSA_Q_COLS), lambda i, j, g: (i, j, 0))
    kv = pl.BlockSpec((1, s, NSA_KV_COLS), lambda i, j, g: (i, 0, 0))
    cmp_spec = pl.BlockSpec((1, nc, NSA_KV_COLS), lambda i, j, g: (i, 0, 0))
    gspec = pl.BlockSpec((1, tq, LANES), lambda i, j, g: (i, j, g))
    return pl.pallas_call(
        functools.partial(_nsa_attn_kernel, n_sel=n_sel, k_sel=k_sel),
        grid=(b, s // tq, NSA_GROUPS),
        in_specs=[qspec, kv, kv, kv, kv, cmp_spec, cmp_spec, gspec,
                  _const_spec(ov.shape), _const_spec(et.shape)],
        out_specs=qspec,
        out_shape=jax.ShapeDtypeStruct((b, s, NSA_Q_COLS), MXU_DTYPE),
        scratch_shapes=[pltpu.VMEM((2, NSA_HPG, tq, LANES), F32),
                        pltpu.VMEM((2, NSA_HPG, tq, LANES), F32),
                        pltpu.VMEM((tq, LANES), MXU_DTYPE)],
        compiler_params=_cparams(("arbitrary", "arbitrary", "arbitrary")),
        name="nsa_attn",
    )(q, ks, vs, kw, vw, kcmp, vcmp, gates, ov, et)


def _mm_res_ln_kernel(a_ref, w_ref, x_ref, g_ref, b_ref, o_ref):
    m = _dot(a_ref[...].astype(MXU_DTYPE), w_ref[...])
    o_ref[...] = _layer_norm_rows(DN_ALPHA * x_ref[...] + m, g_ref[...], b_ref[...])


def _mm_res_ln(a, w, x2, g, b):
    t, k = a.shape
    tm = ROW_TILE
    return pl.pallas_call(
        _mm_res_ln_kernel,
        grid=(t // tm,),
        in_specs=[pl.BlockSpec((tm, k), lambda i: (i, 0)), _const_spec(w.shape),
                  pl.BlockSpec((tm, D_MODEL), lambda i: (i, 0)),
                  _const_spec((1, D_MODEL)), _const_spec((1, D_MODEL))],
        out_specs=pl.BlockSpec((tm, D_MODEL), lambda i: (i, 0)),
        out_shape=jax.ShapeDtypeStruct((t, D_MODEL), F32),
        compiler_params=_cparams(("arbitrary",)),
        name="mm_res_ln",
    )(a, w, x2, g, b)


def _shift_rows(v, prev, k):
    out = pltpu.roll(v, k, 0)
    rows = lax.broadcasted_iota(jnp.int32, (v.shape[0], 1), 0)
    np_rows = prev.shape[0]
    for i in range(k):
        out = jnp.where(rows == i, prev[np_rows - k + i:np_rows - k + i + 1, :], out)
    return out


def _shortconv_kernel(x_ref, win_ref, cw_ref, cb_ref, wout_ref, g_ref, b_ref, o_ref, carry_ref):
    @pl.when(pl.program_id(1) == 0)
    def _():
        carry_ref[...] = jnp.zeros(carry_ref.shape, F32)

    x = x_ref[0]
    xb = x.astype(MXU_DTYPE)
    d = D_MODEL
    u = _dot(xb, win_ref[:, 0:d])
    bg = _dot(xb, win_ref[:, d:2 * d])
    cg = _dot(xb, win_ref[:, 2 * d:3 * d])
    v = cg * u
    prev = carry_ref[...]
    z = cb_ref[...] + cw_ref[SC_WIDTH - 1:SC_WIDTH, :] * v
    for k in range(1, SC_WIDTH):
        z = z + cw_ref[SC_WIDTH - 1 - k:SC_WIDTH - k, :] * _shift_rows(v, prev, k)
    carry_ref[...] = v[v.shape[0] - SUBLANES:, :]
    m = _dot((bg * z).astype(MXU_DTYPE), wout_ref[...])
    o_ref[0] = _layer_norm_rows(DN_ALPHA * x + m, g_ref[...], b_ref[...])


def _shortconv_layer(x3, win, cw, cb, wout, g, b):
    bsz, s, d = x3.shape
    tm = ROW_TILE
    xspec = pl.BlockSpec((1, tm, d), lambda i, j: (i, j, 0))
    return pl.pallas_call(
        _shortconv_kernel,
        grid=(bsz, s // tm),
        in_specs=[xspec, _const_spec(win.shape), _const_spec(cw.shape), _const_spec(cb.shape),
                  _const_spec(wout.shape), _const_spec(g.shape), _const_spec(b.shape)],
        out_specs=xspec,
        out_shape=jax.ShapeDtypeStruct(x3.shape, F32),
        scratch_shapes=[pltpu.VMEM((SUBLANES, d), F32)],
        compiler_params=_cparams(("arbitrary", "arbitrary")),
        name="shortconv_layer",
    )(x3, win, cw, cb, wout, g, b)


def _round_down(x, m):
    return x // m * m


def _round_up(x, m):
    return -(-x // m) * m


def _block_diag_dot(u, w_ref):
    tile = 2 * LANES
    outs = []
    for c0 in range(0, LRU_WIDTH, tile):
        c1 = min(c0 + tile, LRU_WIDTH)
        lo = _round_down(_round_down(c0, LRU_BW), LANES)
        hi = min(_round_up(_round_up(c1, LRU_BW), LANES), LRU_WIDTH)
        outs.append(_dot(u[:, lo:hi], w_ref[lo:hi, c0:c1]))
    return jnp.concatenate(outs, axis=1)


def _rglru_kernel(x_ref, win_ref, cw_ref, cb_ref, wa_ref, ba_ref, wx_ref, bx_ref, lam_ref,
                  wout_ref, g_ref, b_ref, o_ref, carry_ref, h_ref, a_s, b_s):
    @pl.when(pl.program_id(1) == 0)
    def _():
        carry_ref[...] = jnp.zeros(carry_ref.shape, F32)
        h_ref[...] = jnp.zeros(h_ref.shape, F32)

    x = x_ref[0]
    tm = x.shape[0]
    xb = x.astype(MXU_DTYPE)
    w = LRU_WIDTH
    gate_br = _dot(xb, win_ref[:, 0:w])
    rec = _dot(xb, win_ref[:, w:2 * w])
    prev = carry_ref[...]
    u = cb_ref[...] + cw_ref[LRU_CONV - 1:LRU_CONV, :] * rec
    for k in range(1, LRU_CONV):
        u = u + cw_ref[LRU_CONV - 1 - k:LRU_CONV - k, :] * _shift_rows(rec, prev, k)
    carry_ref[...] = rec[tm - SUBLANES:, :]
    ub = u.astype(MXU_DTYPE)
    r = jax.nn.sigmoid(_block_diag_dot(ub, wa_ref) + ba_ref[...])
    ig = jax.nn.sigmoid(_block_diag_dot(ub, wx_ref) + bx_ref[...])
    log_a = LRU_C * r * jax.nn.log_sigmoid(lam_ref[...])
    a_s[...] = jnp.exp(log_a)
    th = jnp.tanh(log_a)
    b_s[...] = jnp.sqrt(-2.0 * th / (1.0 - th)) * (ig * u)

    rows = lax.broadcasted_iota(jnp.int32, (SUBLANES, 1), 0)

    def group(i, hc):
        r0 = pl.multiple_of(i * SUBLANES, SUBLANES)
        a = a_s[pl.ds(r0, SUBLANES), :]
        bv = b_s[pl.ds(r0, SUBLANES), :]
        for dlt in (1, 2, 4):
            a_sh = jnp.where(rows >= dlt, pltpu.roll(a, dlt, 0), 1.0)
            b_sh = jnp.where(rows >= dlt, pltpu.roll(bv, dlt, 0), 0.0)
            bv = a * b_sh + bv
            a = a * a_sh
        h = a * hc + bv
        b_s[pl.ds(r0, SUBLANES), :] = h
        return h[SUBLANES - 1:SUBLANES, :]

    h_ref[...] = lax.fori_loop(0, tm // SUBLANES, group, h_ref[...])
    y = jax.nn.gelu(gate_br) * b_s[...]
    m = _dot(y.astype(MXU_DTYPE), wout_ref[...])
    o_ref[0] = _layer_norm_rows(DN_ALPHA * x + m, g_ref[...], b_ref[...])


def _rglru_layer(x3, win, cw, cb, wa, ba, wx, bx, lam, wout, g, b):
    bsz, s, d = x3.shape
    tm = LRU_ROW_TILE
    xspec = pl.BlockSpec((1, tm, d), lambda i, j: (i, j, 0))
    consts = [win, cw, cb, wa, ba, wx, bx, lam, wout, g, b]
    return pl.pallas_call(
        _rglru_kernel,
        grid=(bsz, s // tm),
        in_specs=[xspec] + [_const_spec(c.shape) for c in consts],
        out_specs=xspec,
        out_shape=jax.ShapeDtypeStruct(x3.shape, F32),
        scratch_shapes=[pltpu.VMEM((SUBLANES, LRU_WIDTH), F32), pltpu.VMEM((1, LRU_WIDTH), F32),
                        pltpu.VMEM((tm, LRU_WIDTH), F32), pltpu.VMEM((tm, LRU_WIDTH), F32)],
        compiler_params=_cparams(("arbitrary", "arbitrary")),
        name="rglru_layer",
    )(x3, win, cw, cb, wa, ba, wx, bx, lam, wout, g, b)


def _moe_router_kernel(x_ref, rwt_ref, rb_ref, tri_ref, pos_ref, wts_ref, meta_ref):
    tt = x_ref.shape[0]
    x = x_ref[...]
    w = rwt_ref[...]
    xh = x.astype(MXU_DTYPE)
    wh = w.astype(MXU_DTYPE)
    xl = (x - xh.astype(F32)).astype(MXU_DTYPE)
    wl = (w - wh.astype(F32)).astype(MXU_DTYPE)
    logits = _dot_nt(wh, xh) + _dot_nt(wh, xl) + _dot_nt(wl, xh)
    scores = jax.nn.sigmoid(logits)
    biased = scores + rb_ref[...]
    rows = [biased[e:e + 1, :] for e in range(N_EXPERTS)]

    gscore = []
    for gi in range(N_GROUPS):
        a, b, c, d = rows[gi * EXPERTS_PER_GROUP:(gi + 1) * EXPERTS_PER_GROUP]
        hi1, lo1 = jnp.maximum(a, b), jnp.minimum(a, b)
        hi2, lo2 = jnp.maximum(c, d), jnp.minimum(c, d)
        gscore.append(jnp.maximum(hi1, hi2) + jnp.maximum(jnp.minimum(hi1, hi2),
                                                          jnp.maximum(lo1, lo2)))
    best = gscore[0]
    top_g = jnp.zeros((1, tt), jnp.int32)
    for gi in range(1, N_GROUPS):
        better = gscore[gi] > best
        best = jnp.where(better, gscore[gi], best)
        top_g = jnp.where(better, gi, top_g)

    def argmax_in_group(exclude):
        bv = jnp.full((1, tt), -jnp.inf, F32)
        bi = jnp.full((1, tt), -1, jnp.int32)
        for e in range(N_EXPERTS):
            ok = top_g == (e // EXPERTS_PER_GROUP)
            if exclude is not None:
                ok = ok & (exclude != e)
            better = ok & (rows[e] > bv)
            bv = jnp.where(better, rows[e], bv)
            bi = jnp.where(better, e, bi)
        return bi

    e1 = argmax_in_group(None)
    e2 = argmax_in_group(e1)
    eidx = lax.broadcasted_iota(jnp.int32, (N_EXPERTS, 1), 0)
    oh1 = eidx == e1
    oh2 = eidx == e2
    s1 = jnp.sum(jnp.where(oh1, scores, 0.0), axis=0, keepdims=True)
    s2 = jnp.sum(jnp.where(oh2, scores, 0.0), axis=0, keepdims=True)
    wsum = s1 + s2
    wts_ref[0, 0:1, :] = s1 / wsum
    wts_ref[0, 1:2, :] = s2 / wsum

    oh = jnp.where(oh1 | oh2, 1.0, 0.0)
    csum = _dot(oh.astype(MXU_DTYPE), tri_ref[...])
    excl = csum - oh
    cnt = csum[:, tt - 1:tt]
    padded = jnp.ceil(cnt * (1.0 / MOE_ALIGN)) * MOE_ALIGN
    run = jnp.zeros((1, 1), F32)
    off_rows = []
    for e in range(N_EXPERTS):
        off_rows.append(run)
        run = run + padded[e:e + 1, :]
    off = jnp.concatenate(off_rows, axis=0)
    slot = off + excl

    def flat_row_of(onehot):
        p = jnp.sum(jnp.where(onehot, slot, 0.0), axis=0, keepdims=True).astype(jnp.int32)
        return _flat_row(p)

    pos_ref[0, 0:1, :] = flat_row_of(oh1)
    pos_ref[0, 1:2, :] = flat_row_of(oh2)
    lane = lax.broadcasted_iota(jnp.int32, (1, LANES), 1)
    meta = jnp.zeros((1, LANES), F32)
    for e in range(N_EXPERTS):
        meta = jnp.where(lane == e, off[e:e + 1, :], meta)
        meta = jnp.where(lane == N_EXPERTS + e, cnt[e:e + 1, :], meta)
    meta_ref[0] = meta.astype(jnp.int32)


def _moe_router(x2, rwt, rb, tri):
    t = x2.shape[0]
    tt = tri.shape[0]
    nt = t // tt
    return pl.pallas_call(
        _moe_router_kernel,
        grid=(nt,),
        in_specs=[pl.BlockSpec((tt, D_MODEL), lambda i: (i, 0)), _const_spec(rwt.shape),
                  _const_spec(rb.shape), _const_spec(tri.shape)],
        out_specs=[pl.BlockSpec((1, TOP_K, tt), lambda i: (i, 0, 0)),
                   pl.BlockSpec((1, TOP_K, tt), lambda i: (i, 0, 0)),
                   pl.BlockSpec((1, 1, LANES), lambda i: (i, 0, 0))],
        out_shape=[jax.ShapeDtypeStruct((nt, TOP_K, tt), jnp.int32),
                   jax.ShapeDtypeStruct((nt, TOP_K, tt), F32),
                   jax.ShapeDtypeStruct((nt, 1, LANES), jnp.int32)],
        compiler_params=_cparams(("arbitrary",)),
        name="moe_router",
    )(x2, rwt, rb, tri)


def _flat_row(r):
    return (r >> _SUBLANE_SHIFT) * _TILE_ROW + (r & (SUBLANES - 1))


def _row_slice(q):
    return pl.ds(q, _LANE_TILES, stride=SUBLANES)


def _load_rows_flat(ref, r0, n):
    base = pl.multiple_of(r0 * _LANE_TILES, _TILE_ROW)
    cols = [jnp.concatenate([ref[pl.ds(base + i * _TILE_ROW + c * SUBLANES, SUBLANES), :]
                             for i in range(n // SUBLANES)], axis=0) for c in range(_LANE_TILES)]
    return jnp.concatenate(cols, axis=1)


def _store_rows_flat(ref, r0, val):
    base = pl.multiple_of(r0 * _LANE_TILES, _TILE_ROW)
    for c in range(_LANE_TILES):
        for i in range(val.shape[0] // SUBLANES):
            ref[pl.ds(base + i * _TILE_ROW + c * SUBLANES, SUBLANES), :] = (
                val[i * SUBLANES:(i + 1) * SUBLANES, c * LANES:(c + 1) * LANES])


def _moe_ffn_kernel(pos_ref, wts_ref, meta_ref, x_ref, wg_ref, wu_ref, wd_ref, g_ref, b_ref,
                    o_ref, xs_ref, xf_ref):
    tt = x_ref.shape[0]
    ch = MOE_CHUNK
    e = pl.program_id(1)

    def token_group(body):
        def group(i, carry):
            base = pl.multiple_of(i * SUBLANES, SUBLANES)
            qbase = pl.multiple_of(i * _TILE_ROW, _TILE_ROW)
            for k in range(SUBLANES):
                body(base + k, qbase + k)
            return carry

        lax.fori_loop(0, tt // SUBLANES, group, 0)

    @pl.when(e == 0)
    def _dispatch():
        def to_flat(i, carry):
            r0 = pl.multiple_of(i * SUBLANES, SUBLANES)
            _store_rows_flat(xf_ref, r0, x_ref[pl.ds(r0, SUBLANES), :])
            return carry

        lax.fori_loop(0, tt // SUBLANES, to_flat, 0)

        def zero_gap(ei, carry):
            start = meta_ref[ei] + meta_ref[N_EXPERTS + ei]
            stop = jnp.where(ei == N_EXPERTS - 1, start + ch + MOE_ALIGN,
                             meta_ref[jnp.minimum(ei + 1, N_EXPERTS - 1)])

            def zrow(r, c):
                xs_ref[_row_slice(_flat_row(r)), :] = jnp.zeros((_LANE_TILES, LANES), F32)
                return c

            return lax.fori_loop(start, stop, zrow, carry)

        lax.fori_loop(0, N_EXPERTS, zero_gap, 0)

        def scatter(t, q):
            row = xf_ref[_row_slice(q), :]
            xs_ref[_row_slice(pos_ref[t]), :] = row
            xs_ref[_row_slice(pos_ref[tt + t]), :] = row

        token_group(scatter)

    off = meta_ref[e]
    cnt = meta_ref[N_EXPERTS + e]
    rid = lax.broadcasted_iota(jnp.int32, (ch, 1), 0)

    def chunk(c, carry):
        r0 = pl.multiple_of(off + c * ch, MOE_ALIGN)
        xs = _load_rows_flat(xs_ref, r0, ch)
        xb = xs.astype(MXU_DTYPE)
        hid = jax.nn.silu(_dot(xb, wg_ref[0, 0])) * _dot(xb, wu_ref[0, 0])
        out = _dot(hid.astype(MXU_DTYPE), wd_ref[0, 0])
        _store_rows_flat(xs_ref, r0, jnp.where(rid + c * ch < cnt, out, xs))
        return carry

    lax.fori_loop(0, (cnt + ch - 1) // ch, chunk, 0)

    @pl.when(e == N_EXPERTS - 1)
    def _combine():
        def gather(t, q):
            xf_ref[_row_slice(q), :] = (wts_ref[t] * xs_ref[_row_slice(pos_ref[t]), :]
                                        + wts_ref[tt + t] * xs_ref[_row_slice(pos_ref[tt + t]), :])

        token_group(gather)

        def norm(i, carry):
            r0 = pl.multiple_of(i * MOE_NORM_ROWS, MOE_NORM_ROWS)
            y = (DN_ALPHA * x_ref[pl.ds(r0, MOE_NORM_ROWS), :]
                 + _load_rows_flat(xf_ref, r0, MOE_NORM_ROWS))
            o_ref[pl.ds(r0, MOE_NORM_ROWS), :] = _layer_norm_rows(y, g_ref[...], b_ref[...])
            return carry

        lax.fori_loop(0, tt // MOE_NORM_ROWS, norm, 0)


def _moe_ffn(pos, wts, meta, x2, layer, wg, wu, wd, g, b):
    t = x2.shape[0]
    tt = pos.shape[2]
    nt = t // tt
    smem = lambda n: pl.BlockSpec((n,), lambda i, e: (i,), memory_space=pltpu.SMEM)
    wspec = lambda w: pl.BlockSpec((1, 1) + w.shape[2:], lambda i, e: (layer, e, 0, 0))
    xs_rows = TOP_K * tt + N_EXPERTS * MOE_ALIGN + MOE_CHUNK + MOE_ALIGN
    pos, wts, meta = pos.reshape(-1), wts.reshape(-1), meta.reshape(-1)
    return pl.pallas_call(
        _moe_ffn_kernel,
        grid=(nt, N_EXPERTS),
        in_specs=[smem(TOP_K * tt), smem(TOP_K * tt), smem(LANES),
                  pl.BlockSpec((tt, D_MODEL), lambda i, e: (i, 0), pipeline_mode=pl.Buffered(1)),
                  wspec(wg), wspec(wu), wspec(wd),
                  pl.BlockSpec((1, D_MODEL), lambda i, e: (0, 0)),
                  pl.BlockSpec((1, D_MODEL), lambda i, e: (0, 0))],
        out_specs=pl.BlockSpec((tt, D_MODEL), lambda i, e: (i, 0), pipeline_mode=pl.Buffered(1)),
        out_shape=jax.ShapeDtypeStruct((t, D_MODEL), F32),
        scratch_shapes=[pltpu.VMEM((xs_rows * _LANE_TILES, LANES), F32),
                        pltpu.VMEM((tt * _LANE_TILES, LANES), F32)],
        compiler_params=_cparams(("arbitrary", "arbitrary")),
        name="moe_ffn",
    )(pos, wts, meta, x2, wg, wu, wd, g, b)


def _nsa_weight_layout(w_in, b_gate):
    G, R, Dh = NSA_GROUPS, NSA_HPG, HEAD_DIM
    d_in = w_in.shape[0]
    kvw = NSA_KV_COLS
    wq = w_in[:, :NSA_Q_COLS].reshape(d_in, G, R, Dh).transpose(0, 2, 1, 3).reshape(d_in, NSA_Q_COLS)
    kc, vc, ks, vs, kw, vw = (w_in[:, NSA_Q_COLS + i * kvw:NSA_Q_COLS + (i + 1) * kvw] for i in range(6))
    gate_base = NSA_Q_COLS + 6 * kvw

    def gate_layout(v):
        lead = v.shape[:-1]
        v = jnp.swapaxes(v.reshape(lead + (G, R, 3)), -1, -2).reshape(lead + (G, 3 * R))
        v = jnp.pad(v, [(0, 0)] * (len(lead) + 1) + [(0, LANES - 3 * R)])
        return v.reshape(lead + (_NSA_GATE_COLS,))

    wg = gate_layout(w_in[:, gate_base:gate_base + 3 * NSA_HEADS])
    w = jnp.concatenate([wq, ks, kw, kc, vc, vs, vw, wg], axis=1).astype(MXU_DTYPE)
    return w, gate_layout(b_gate)[None, :]


def _nsa_out_layout(w_out):
    G, R, Dh = NSA_GROUPS, NSA_HPG, HEAD_DIM
    d_out = w_out.shape[1]
    w = w_out.reshape(G, R, Dh, d_out).transpose(1, 0, 2, 3).reshape(NSA_Q_COLS, d_out)
    return w.astype(MXU_DTYPE)


def _compress_weight_layout(pe, w1, w2):
    G, Dh, L = NSA_GROUPS, HEAD_DIM, CMP_BLOCK
    eye = jnp.eye(G, dtype=w1.dtype)
    w1r = w1.reshape(L, Dh, Dh)
    w1big = jnp.einsum("ldj,gh->lgdhj", w1r, eye).reshape(L * G * Dh, G * Dh)
    w2big = jnp.einsum("dj,gh->gdhj", w2, eye).reshape(G * Dh, G * Dh)
    pebig = jnp.broadcast_to(pe[:, None, :], (L, G, Dh)).reshape(1, L * G * Dh)
    return pebig, w1big.astype(MXU_DTYPE), w2big.astype(MXU_DTYPE)


def _block_diag(w):
    h, bi, bj = w.shape
    eye = jnp.eye(h, dtype=w.dtype)
    return jnp.einsum("hij,hk->hikj", w, eye).reshape(h * bi, h * bj).astype(MXU_DTYPE)


def _nsa_constants(s):
    nc = s // CMP_STRIDE
    n_cmp = (s - CMP_BLOCK) // CMP_STRIDE + 1
    n_sel = s // SEL_BLOCK
    cmp_start = np.arange(nc) * CMP_STRIDE
    cmp_end = cmp_start + CMP_BLOCK - 1
    sel_start = np.arange(n_sel) * SEL_BLOCK
    ov = ((cmp_start[:, None] <= sel_start[None, :] + SEL_BLOCK - 1)
          & (cmp_end[:, None] >= sel_start[None, :])
          & (np.arange(nc)[:, None] < n_cmp)).astype(np.float32)
    ov = np.pad(ov, ((0, 0), (0, LANES - n_sel)))
    et = (np.arange(s)[:, None] // SEL_BLOCK == np.arange(LANES)[None, :]).astype(np.float32)
    return jnp.asarray(ov), jnp.asarray(et, dtype=MXU_DTYPE)


def _nsa_layer(x2, bsz, s, tables, cmp_tables, w_in, b_gate, pos_k, w1_k, w2_k, pos_v, w1_v, w2_v,
               w_out, g, b):
    w_perm, bg_perm = _nsa_weight_layout(w_in, b_gate)
    q, ks, kw, kc, vc, vs, vw, gates = _nsa_proj(x2, w_perm, bg_perm, *tables)
    nc = s // CMP_STRIDE
    feat = CMP_STRIDE * NSA_KV_COLS
    pek, w1k, w2k = _compress_weight_layout(pos_k, w1_k, w2_k)
    pev, w1v, w2v = _compress_weight_layout(pos_v, w1_v, w2_v)
    kcmp, vcmp = _nsa_compress(kc.reshape(bsz, nc, feat), vc.reshape(bsz, nc, feat),
                               pek, pev, w1k, w1v, w2k, w2v, *cmp_tables)
    ov, et = _nsa_constants(s)
    r3 = lambda a: a.reshape(bsz, s, a.shape[-1])
    o = _nsa_attn(r3(q), r3(ks), r3(vs), r3(kw), r3(vw), kcmp, vcmp, r3(gates), ov, et)
    return _mm_res_ln(o.reshape(bsz * s, NSA_Q_COLS), _nsa_out_layout(w_out), x2, g, b)


def _moe_layer(x2, rwt, rb, tri, layer, wg, wu, wd, g, b):
    pos, wts, meta = _moe_router(x2, rwt, rb, tri)
    return _moe_ffn(pos, wts, meta, x2, layer, wg, wu, wd, g, b)


def kernel(x, positions, nsa_w_in, nsa_b_gate, nsa_cmp_pos_k, nsa_cmp_w1_k, nsa_cmp_w2_k, nsa_cmp_pos_v, nsa_cmp_w1_v, nsa_cmp_w2_v, nsa_w_out, sc_w_in, sc_conv_w, sc_conv_b, sc_w_out, lru_w_in, lru_conv_w, lru_conv_b, lru_wa, lru_ba, lru_wx, lru_bx, lru_lambda, lru_w_out, router_w, router_b, moe_w_gate, moe_w_up, moe_w_down, ln_g, ln_b):
    bsz, s, d = x.shape
    t = bsz * s
    x2 = x.reshape(t, d)
    row = lambda v: v.reshape(1, -1)

    tables = _rope_tables(positions.reshape(t))
    nc = s // CMP_STRIDE
    cmp_idx = np.minimum(np.arange(nc) * CMP_STRIDE + CMP_BLOCK - 1, s - 1)
    cmp_tables = tuple(tb.reshape(bsz, nc, LANES)
                       for tb in _rope_tables(positions[:, cmp_idx].reshape(bsz * nc)))

    tt = min(MOE_TOKENS, t)
    tri = jnp.asarray(np.triu(np.ones((tt, tt), np.float32)), dtype=MXU_DTYPE)
    rwt = router_w.T
    rb = router_b.reshape(N_EXPERTS, 1)
    moe_wg, moe_wu, moe_wd = (w.astype(MXU_DTYPE) for w in (moe_w_gate, moe_w_up, moe_w_down))

    for i in range(DEPTH):
        kind, j = i % N_MIXERS, i // N_MIXERS
        g0, b0 = row(ln_g[i, 0]), row(ln_b[i, 0])
        if kind == 0:
            x2 = _nsa_layer(x2, bsz, s, tables, cmp_tables, nsa_w_in[j], nsa_b_gate[j],
                            nsa_cmp_pos_k[j], nsa_cmp_w1_k[j], nsa_cmp_w2_k[j],
                            nsa_cmp_pos_v[j], nsa_cmp_w1_v[j], nsa_cmp_w2_v[j], nsa_w_out[j], g0, b0)
        elif kind == 1:
            x2 = _shortconv_layer(x2.reshape(bsz, s, d), sc_w_in[j].astype(MXU_DTYPE), sc_conv_w[j],
                                  row(sc_conv_b[j]), sc_w_out[j].astype(MXU_DTYPE), g0, b0).reshape(t, d)
        else:
            x2 = _rglru_layer(x2.reshape(bsz, s, d), lru_w_in[j].astype(MXU_DTYPE), lru_conv_w[j],
                              row(lru_conv_b[j]), _block_diag(lru_wa[j]), row(lru_ba[j]),
                              _block_diag(lru_wx[j]), row(lru_bx[j]), row(lru_lambda[j]),
                              lru_w_out[j].astype(MXU_DTYPE), g0, b0).reshape(t, d)
        x2 = _moe_layer(x2, rwt, rb, tri, i, moe_wg, moe_wu, moe_wd, row(ln_g[i, 1]), row(ln_b[i, 1]))
    return x2.reshape(bsz, s, d)
```

```python
import functools

import numpy as np
import jax
import jax.numpy as jnp
from jax import lax
from jax.experimental import pallas as pl
from jax.experimental.pallas import tpu as pltpu

F32 = jnp.float32
BF16 = jnp.bfloat16
MXU_DTYPE = BF16

D_MODEL = 1024
DEPTH = 4
N_MIXERS = 3
NSA_HEADS = 16
NSA_GROUPS = 4
NSA_HPG = NSA_HEADS // NSA_GROUPS
HEAD_DIM = D_MODEL // NSA_HEADS
CMP_BLOCK = 32
CMP_STRIDE = 16
SEL_BLOCK = 64
SEL_TOP = 16
WINDOW = 512
ROPE_THETA = 10000.0
FORCE_SCORE = 1e6
NSA_Q_COLS = NSA_HEADS * HEAD_DIM
NSA_KV_COLS = NSA_GROUPS * HEAD_DIM
SC_WIDTH = 3
LRU_WIDTH = 1280
LRU_BLOCKS = 16
LRU_BW = LRU_WIDTH // LRU_BLOCKS
LRU_CONV = 4
LRU_C = 8.0
N_EXPERTS = 16
N_GROUPS = 4
EXPERTS_PER_GROUP = N_EXPERTS // N_GROUPS
TOP_K = 2
D_FF = 1024
DN_ALPHA = (2.0 * DEPTH) ** 0.25
LN_EPS = 1e-5
NEG_INF = -1e30
LOG2_E = 1.4426950408889634

LANES = 128
SUBLANES = 8
VMEM_LIMIT_BYTES = 56 * 1024 * 1024
ROW_TILE = 512
LRU_ROW_TILE = 256
ATT_TQ = 512
ATT_TK = 512
MOE_TOKENS = 2048
MOE_CHUNK = 288
MOE_NORM_ROWS = 128
MOE_ALIGN = SUBLANES
_SUBLANE_SHIFT = 3
_LANE_TILES = D_MODEL // LANES
_TILE_ROW = SUBLANES * _LANE_TILES


def _cparams(sem):
    return pltpu.CompilerParams(dimension_semantics=sem, vmem_limit_bytes=VMEM_LIMIT_BYTES)


def _dot(a, b):
    return jnp.dot(a, b, preferred_element_type=F32)


def _dot_nt(a, b, precision=None):
    return lax.dot_general(a, b, (((1,), (1,)), ((), ())), preferred_element_type=F32,
                           precision=precision)


def _layer_norm_rows(y, g, b):
    mu = jnp.mean(y, axis=-1, keepdims=True)
    d = y - mu
    var = jnp.mean(d * d, axis=-1, keepdims=True)
    return d * lax.rsqrt(var + LN_EPS) * g + b


def _const_spec(shape):
    nd = len(shape)
    return pl.BlockSpec(shape, lambda *_: (0,) * nd)


def _rope_tables_kernel(pos_ref, inv_ref, cos_ref, sina_ref, sinb_ref):
    ang = pos_ref[...] * inv_ref[...]
    c = jnp.cos(ang)
    s = jnp.sin(ang)
    lane = lax.broadcasted_iota(jnp.int32, ang.shape, 1)
    first = (lane % HEAD_DIM) < (HEAD_DIM // 2)
    cos_ref[...] = c
    sina_ref[...] = jnp.where(first, -s, 0.0)
    sinb_ref[...] = jnp.where(first, 0.0, s)


def _rope_tables(pos_flat):
    n = pos_flat.shape[0]
    tm = min(n, 1024)
    half = HEAD_DIM // 2
    inv = ROPE_THETA ** (-jnp.arange(half, dtype=F32) / half)
    inv_lane = jnp.tile(inv, LANES // half)[None, :]
    posb = jnp.broadcast_to(pos_flat.astype(F32)[:, None], (n, LANES))
    spec = pl.BlockSpec((tm, LANES), lambda i: (i, 0))
    out = jax.ShapeDtypeStruct((n, LANES), F32)
    return pl.pallas_call(
        _rope_tables_kernel,
        grid=(n // tm,),
        in_specs=[spec, _const_spec((1, LANES))],
        out_specs=[spec, spec, spec],
        out_shape=[out, out, out],
        compiler_params=_cparams(("arbitrary",)),
        name="rope_tables",
    )(posb, inv_lane)


def _rope128(v, cos, sina, sinb):
    half = HEAD_DIM // 2
    return (v * cos + pltpu.roll(v, LANES - half, 1) * sina + pltpu.roll(v, half, 1) * sinb)


_NSA_SEG = {
    "q": 0, "ks": 1024, "kw": 1280, "kc": 1536, "vc": 1792, "vs": 2048, "vw": 2304, "gate": 2560,
}
_NSA_GATE_COLS = NSA_GROUPS * LANES
_NSA_PROJ_COLS = 2560 + _NSA_GATE_COLS


def _nsa_proj_kernel(x_ref, w_ref, bg_ref, cos_ref, sina_ref, sinb_ref,
                     q_ref, ks_ref, kw_ref, kc_ref, vc_ref, vs_ref, vw_ref, g_ref, chunk_ref):
    xb = x_ref[...].astype(MXU_DTYPE)
    cos = cos_ref[...]
    sina = sina_ref[...]
    sinb = sinb_ref[...]
    scale = HEAD_DIM ** -0.5 * LOG2_E

    def seg(name, width=256, c=0):
        o = _NSA_SEG[name] + c * 256
        return _dot(xb, w_ref[:, o:o + width])

    for c in range(NSA_Q_COLS // 256):
        acc = seg("q", c=c)
        for h in range(2):
            r = _rope128(acc[:, h * LANES:(h + 1) * LANES], cos, sina, sinb) * scale
            q_ref[:, c * 256 + h * LANES:c * 256 + (h + 1) * LANES] = r.astype(q_ref.dtype)
    for name, ref in (("ks", ks_ref), ("kw", kw_ref)):
        acc = seg(name)
        for h in range(2):
            r = _rope128(acc[:, h * LANES:(h + 1) * LANES], cos, sina, sinb)
            ref[:, h * LANES:(h + 1) * LANES] = r.astype(ref.dtype)
    n_chunks = kc_ref.shape[0]
    for name, ref in (("kc", kc_ref), ("vc", vc_ref)):
        acc = seg(name)
        for h in range(2):
            chunk_ref[h] = acc[:, h * LANES:(h + 1) * LANES]
        for l in range(CMP_STRIDE):
            for h in range(2):
                lo = l * NSA_KV_COLS + h * LANES
                ref[:, lo:lo + LANES] = chunk_ref[h, pl.ds(l, n_chunks, stride=CMP_STRIDE), :]
    vs_ref[...] = seg("vs").astype(vs_ref.dtype)
    vw_ref[...] = seg("vw").astype(vw_ref.dtype)
    g_ref[...] = jax.nn.sigmoid(seg("gate", width=_NSA_GATE_COLS) + bg_ref[...])


def _nsa_proj(x2, w_perm, bg_perm, cos_t, sina_t, sinb_t):
    t = x2.shape[0]
    tm = ROW_TILE
    row = lambda w: pl.BlockSpec((tm, w), lambda i: (i, 0))
    chunk_cols = CMP_STRIDE * NSA_KV_COLS
    chunk = pl.BlockSpec((tm // CMP_STRIDE, chunk_cols), lambda i: (i, 0))
    outs = [
        jax.ShapeDtypeStruct((t, NSA_Q_COLS), MXU_DTYPE),
        jax.ShapeDtypeStruct((t, NSA_KV_COLS), MXU_DTYPE),
        jax.ShapeDtypeStruct((t, NSA_KV_COLS), MXU_DTYPE),
        jax.ShapeDtypeStruct((t // CMP_STRIDE, chunk_cols), F32),
        jax.ShapeDtypeStruct((t // CMP_STRIDE, chunk_cols), F32),
        jax.ShapeDtypeStruct((t, NSA_KV_COLS), MXU_DTYPE),
        jax.ShapeDtypeStruct((t, NSA_KV_COLS), MXU_DTYPE),
        jax.ShapeDtypeStruct((t, _NSA_GATE_COLS), F32),
    ]
    return pl.pallas_call(
        _nsa_proj_kernel,
        grid=(t // tm,),
        in_specs=[row(D_MODEL), _const_spec((D_MODEL, _NSA_PROJ_COLS)),
                  _const_spec((1, _NSA_GATE_COLS)), row(LANES), row(LANES), row(LANES)],
        out_specs=[row(NSA_Q_COLS), row(NSA_KV_COLS), row(NSA_KV_COLS), chunk, chunk,
                   row(NSA_KV_COLS), row(NSA_KV_COLS), row(_NSA_GATE_COLS)],
        out_shape=outs,
        scratch_shapes=[pltpu.VMEM((2, tm, LANES), F32)],
        compiler_params=_cparams(("arbitrary",)),
        name="nsa_proj",
    )(x2, w_perm, bg_perm, cos_t, sina_t, sinb_t)


def _nsa_compress_kernel(kc_ref, vc_ref, pek_ref, pev_ref, w1k_ref, w1v_ref, w2k_ref, w2v_ref,
                         cos_ref, sina_ref, sinb_ref, ko_ref, vo_ref):
    nc = kc_ref.shape[1]
    half_feat = kc_ref.shape[2]
    rows = lax.broadcasted_iota(jnp.int32, (nc, 1), 0)
    valid = rows < nc - 1

    def compress(tok, pe_ref, w1_ref, w2_ref):
        a = (tok + pe_ref[:, :half_feat]).astype(MXU_DTYPE)
        b = (tok + pe_ref[:, half_feat:]).astype(MXU_DTYPE)
        pa = _dot(a, w1_ref[:half_feat, :])
        pb = _dot(b, w1_ref[half_feat:, :])
        pre = pa + pltpu.roll(pb, nc - 1, 0)
        h = jax.nn.gelu(pre)
        return _dot(h.astype(MXU_DTYPE), w2_ref[...])

    kcmp = compress(kc_ref[0], pek_ref, w1k_ref, w2k_ref)
    vcmp = compress(vc_ref[0], pev_ref, w1v_ref, w2v_ref)
    cos = cos_ref[0]
    sina = sina_ref[0]
    sinb = sinb_ref[0]
    for h in range(NSA_KV_COLS // LANES):
        r = _rope128(kcmp[:, h * LANES:(h + 1) * LANES], cos, sina, sinb)
        ko_ref[0, :, h * LANES:(h + 1) * LANES] = jnp.where(valid, r, 0.0).astype(ko_ref.dtype)
    vo_ref[0] = jnp.where(valid, vcmp, 0.0).astype(vo_ref.dtype)


def _nsa_compress(kc3, vc3, pek, pev, w1k, w1v, w2k, w2v, cos_c, sina_c, sinb_c):
    b, nc, feat = kc3.shape
    tok = pl.BlockSpec((1, nc, feat), lambda i: (i, 0, 0))
    tab = pl.BlockSpec((1, nc, LANES), lambda i: (i, 0, 0))
    out = pl.BlockSpec((1, nc, NSA_KV_COLS), lambda i: (i, 0, 0))
    oshape = jax.ShapeDtypeStruct((b, nc, NSA_KV_COLS), MXU_DTYPE)
    return pl.pallas_call(
        _nsa_compress_kernel,
        grid=(b,),
        in_specs=[tok, tok, _const_spec(pek.shape), _const_spec(pev.shape),
                  _const_spec(w1k.shape), _const_spec(w1v.shape),
                  _const_spec(w2k.shape), _const_spec(w2v.shape), tab, tab, tab],
        out_specs=[out, out],
        out_shape=[oshape, oshape],
        compiler_params=_cparams(("arbitrary",)),
        name="nsa_compress",
    )(kc3, vc3, pek, pev, w1k, w1v, w2k, w2v, cos_c, sina_c, sinb_c)


def _nsa_attn_kernel(q_ref, ks_ref, vs_ref, kw_ref, vw_ref, kc_ref, vc_ref, g_ref, ov_ref, et_ref,
                     o_ref, m_ref, acc_ref, sel_ref, *, n_sel, k_sel):
    tq = q_ref.shape[1]
    tk = ATT_TK
    nc = kc_ref.shape[1]
    R = NSA_HPG
    qi = pl.program_id(1)
    g = pl.program_id(2)
    t0 = qi * tq
    trow = t0 + lax.broadcasted_iota(jnp.int32, (tq, 1), 0)
    gmask = lax.broadcasted_iota(jnp.int32, (1, NSA_KV_COLS), 1) // HEAD_DIM == g
    gates = g_ref[0]
    low_half = g < NSA_GROUPS // 2
    in_group = (lax.broadcasted_iota(jnp.int32, (1, LANES), 1) // HEAD_DIM
                == g % (LANES // HEAD_DIM))

    def value_tile(v):
        return jnp.where(in_group, jnp.where(low_half, v[:, :LANES], v[:, LANES:]), 1)

    qs = jnp.concatenate(
        [jnp.where(gmask, q_ref[0, :, r * NSA_KV_COLS:(r + 1) * NSA_KV_COLS], 0)
         for r in range(R)], axis=0)
    rows = lambda a, r: a[r * tq:(r + 1) * tq]

    def tile_update(c, k_ref, v_ref, k0, bias):
        kt = k_ref[0, pl.ds(k0, tk), :]
        vt = value_tile(v_ref[0, pl.ds(k0, tk), :])
        s_all = _dot_nt(qs, kt)
        ps, alphas = [], []
        for r in range(R):
            s = rows(s_all, r) + bias
            m_old = m_ref[c, r]
            m_new = jnp.maximum(m_old, jnp.max(s, axis=-1, keepdims=True))
            m_ref[c, r] = m_new
            alphas.append(jnp.exp2(m_old - m_new))
            m_wide = jnp.concatenate([m_new] * (tk // LANES), axis=1)
            ps.append(jnp.exp2(s - m_wide).astype(MXU_DTYPE))
        pv = _dot(jnp.concatenate(ps, axis=0), vt)
        for r in range(R):
            acc_ref[c, r] = alphas[r] * acc_ref[c, r] + rows(pv, r)

    n_idx = lax.broadcasted_iota(jnp.int32, (1, nc), 1)
    cmp_ok = (n_idx * CMP_STRIDE + CMP_BLOCK - 1) <= trow
    s_all = _dot_nt(qs, kc_ref[0])
    psum = jnp.zeros((tq, nc), F32)
    ps = []
    for r in range(R):
        s = jnp.where(cmp_ok, rows(s_all, r), NEG_INF)
        e = jnp.where(cmp_ok, jnp.exp2(s - jnp.max(s, axis=-1, keepdims=True)), 0.0)
        p = e * (1.0 / jnp.maximum(jnp.sum(e, axis=-1, keepdims=True), 1e-30))
        psum = psum + p
        ps.append(p.astype(MXU_DTYPE))
    o_cmp = _dot(jnp.concatenate(ps, axis=0), value_tile(vc_ref[0]))

    all_selected = t0 + tq <= k_sel * SEL_BLOCK

    @pl.when(all_selected)
    def _():
        sel_ref[...] = jnp.ones(sel_ref.shape, sel_ref.dtype)

    @pl.when(jnp.logical_not(all_selected))
    def _():
        imp = jnp.dot(psum, ov_ref[...], preferred_element_type=F32,
                      precision=lax.Precision.HIGHEST)
        blk = lax.broadcasted_iota(jnp.int32, (1, LANES), 1)
        rel = trow // SEL_BLOCK - blk
        imp = jnp.where(rel >= 0, imp, -1.0)
        forced = (blk == 0) | (rel == 0) | (rel == 1)
        imp = jnp.where(forced, FORCE_SCORE, imp)
        imp_t = jnp.transpose(imp)[:n_sel, :]
        midx = lax.broadcasted_iota(jnp.int32, (n_sel, 1), 0)
        cnt = jnp.zeros((n_sel, tq), F32)
        for mp in range(n_sel):
            row = imp_t[mp:mp + 1, :]
            beats = (row > imp_t) | ((row == imp_t) & (midx > mp))
            cnt = cnt + jnp.where(beats, 1.0, 0.0)
        sel_t = jnp.where(cnt < k_sel, 1.0, 0.0)
        if n_sel < LANES:
            sel_t = jnp.concatenate([sel_t, jnp.zeros((LANES - n_sel, tq), F32)], axis=0)
        sel_ref[...] = jnp.transpose(sel_t).astype(sel_ref.dtype)

    sel = sel_ref[...]

    SEL, WIN = 0, 1
    m_ref[...] = jnp.full(m_ref.shape, NEG_INF, F32)
    acc_ref[...] = jnp.zeros(acc_ref.shape, F32)

    def sel_update(k0, kpos):
        selm = _dot_nt(sel, et_ref[pl.ds(k0, tk), :])
        bias = jnp.where((selm > 0.5) & (kpos <= trow), 0.0, NEG_INF)
        tile_update(SEL, ks_ref, vs_ref, k0, bias)

    def win_update(k0, kpos):
        d = trow - kpos
        bias = jnp.where((d >= 0) & (d < WINDOW), 0.0, NEG_INF)
        tile_update(WIN, kw_ref, vw_ref, k0, bias)

    def key_tile(j):
        k0 = pl.multiple_of(j * tk, tk)
        return k0, k0 + lax.broadcasted_iota(jnp.int32, (1, tk), 1)

    def sel_only(j, carry):
        sel_update(*key_tile(j))
        return carry

    def sel_and_win(j, carry):
        k0, kpos = key_tile(j)
        sel_update(k0, kpos)
        win_update(k0, kpos)
        return carry

    j_win = jnp.maximum(t0 - WINDOW, 0) // tk
    lax.fori_loop(0, j_win, sel_only, 0)
    lax.fori_loop(j_win, (t0 + tq) // tk, sel_and_win, 0)

    def normalized(c, r):
        acc = acc_ref[c, r]
        return acc * (1.0 / pltpu.roll(acc, HEAD_DIM, 1))

    o_heads = [gates[:, r:r + 1] * rows(o_cmp, r)
               + gates[:, R + r:R + r + 1] * normalized(SEL, r)
               + gates[:, 2 * R + r:2 * R + r + 1] * normalized(WIN, r) for r in range(R)]

    for gg in range(NSA_GROUPS):
        @pl.when(g == gg)
        def _(gg=gg):
            src = gg % (LANES // HEAD_DIM) * HEAD_DIM
            for r in range(R):
                lo = r * NSA_KV_COLS + gg * HEAD_DIM
                o_ref[0, :, lo:lo + HEAD_DIM] = o_heads[r][:, src:src + HEAD_DIM].astype(o_ref.dtype)


def _nsa_attn(q, ks, vs, kw, vw, kcmp, vcmp, gates, ov, et):
    b, s, _ = q.shape
    nc = kcmp.shape[1]
    tq = ATT_TQ
    n_sel = s // SEL_BLOCK
    k_sel = min(SEL_TOP, n_sel)
    qspec = pl.BlockSpec((1, tq, NSA_Q_COLS), lambda i, j, g: (i, j, 0))
    kv = pl.BlockSpec((1, s, NSA_KV_COLS), lambda i, j, g: (i, 0, 0))
    cmp_spec = pl.BlockSpec((1, nc, NSA_KV_COLS), lambda i, j, g: (i, 0, 0))
    gspec = pl.BlockSpec((1, tq, LANES), lambda i, j, g: (i, j, g))
    return pl.pallas_call(
        functools.partial(_nsa_attn_kernel, n_sel=n_sel, k_sel=k_sel),
        grid=(b, s // tq, NSA_GROUPS),
        in_specs=[qspec, kv, kv, kv, kv, cmp_spec, cmp_spec, gspec,
                  _const_spec(ov.shape), _const_spec(et.shape)],
        out_specs=qspec,
        out_shape=jax.ShapeDtypeStruct((b, s, NSA_Q_COLS), MXU_DTYPE),
        scratch_shapes=[pltpu.VMEM((2, NSA_HPG, tq, LANES), F32),
                        pltpu.VMEM((2, NSA_HPG, tq, LANES), F32),
                        pltpu.VMEM((tq, LANES), MXU_DTYPE)],
        compiler_params=_cparams(("arbitrary", "arbitrary", "arbitrary")),
        name="nsa_attn",
    )(q, ks, vs, kw, vw, kcmp, vcmp, gates, ov, et)


def _mm_res_ln_kernel(a_ref, w_ref, x_ref, g_ref, b_ref, o_ref):
    m = _dot(a_ref[...].astype(MXU_DTYPE), w_ref[...])
    o_ref[...] = _layer_norm_rows(DN_ALPHA * x_ref[...] + m, g_ref[...], b_ref[...])


def _mm_res_ln(a, w, x2, g, b):
    t, k = a.shape
    tm = ROW_TILE
    return pl.pallas_call(
        _mm_res_ln_kernel,
        grid=(t // tm,),
        in_specs=[pl.BlockSpec((tm, k), lambda i: (i, 0)), _const_spec(w.shape),
                  pl.BlockSpec((tm, D_MODEL), lambda i: (i, 0)),
                  _const_spec((1, D_MODEL)), _const_spec((1, D_MODEL))],
        out_specs=pl.BlockSpec((tm, D_MODEL), lambda i: (i, 0)),
        out_shape=jax.ShapeDtypeStruct((t, D_MODEL), F32),
        compiler_params=_cparams(("arbitrary",)),
        name="mm_res_ln",
    )(a, w, x2, g, b)


def _shift_rows(v, prev, k):
    out = pltpu.roll(v, k, 0)
    rows = lax.broadcasted_iota(jnp.int32, (v.shape[0], 1), 0)
    np_rows = prev.shape[0]
    for i in range(k):
        out = jnp.where(rows == i, prev[np_rows - k + i:np_rows - k + i + 1, :], out)
    return out


def _shortconv_kernel(x_ref, win_ref, cw_ref, cb_ref, wout_ref, g_ref, b_ref, o_ref, carry_ref):
    @pl.when(pl.program_id(1) == 0)
    def _():
        carry_ref[...] = jnp.zeros(carry_ref.shape, F32)

    x = x_ref[0]
    xb = x.astype(MXU_DTYPE)
    d = D_MODEL
    u = _dot(xb, win_ref[:, 0:d])
    bg = _dot(xb, win_ref[:, d:2 * d])
    cg = _dot(xb, win_ref[:, 2 * d:3 * d])
    v = cg * u
    prev = carry_ref[...]
    z = cb_ref[...] + cw_ref[SC_WIDTH - 1:SC_WIDTH, :] * v
    for k in range(1, SC_WIDTH):
        z = z + cw_ref[SC_WIDTH - 1 - k:SC_WIDTH - k, :] * _shift_rows(v, prev, k)
    carry_ref[...] = v[v.shape[0] - SUBLANES:, :]
    m = _dot((bg * z).astype(MXU_DTYPE), wout_ref[...])
    o_ref[0] = _layer_norm_rows(DN_ALPHA * x + m, g_ref[...], b_ref[...])


def _shortconv_layer(x3, win, cw, cb, wout, g, b):
    bsz, s, d = x3.shape
    tm = ROW_TILE
    xspec = pl.BlockSpec((1, tm, d), lambda i, j: (i, j, 0))
    return pl.pallas_call(
        _shortconv_kernel,
        grid=(bsz, s // tm),
        in_specs=[xspec, _const_spec(win.shape), _const_spec(cw.shape), _const_spec(cb.shape),
                  _const_spec(wout.shape), _const_spec(g.shape), _const_spec(b.shape)],
        out_specs=xspec,
        out_shape=jax.ShapeDtypeStruct(x3.shape, F32),
        scratch_shapes=[pltpu.VMEM((SUBLANES, d), F32)],
        compiler_params=_cparams(("arbitrary", "arbitrary")),
        name="shortconv_layer",
    )(x3, win, cw, cb, wout, g, b)


def _round_down(x, m):
    return x // m * m


def _round_up(x, m):
    return -(-x // m) * m


def _block_diag_dot(u, w_ref):
    tile = 2 * LANES
    outs = []
    for c0 in range(0, LRU_WIDTH, tile):
        c1 = min(c0 + tile, LRU_WIDTH)
        lo = _round_down(_round_down(c0, LRU_BW), LANES)
        hi = min(_round_up(_round_up(c1, LRU_BW), LANES), LRU_WIDTH)
        outs.append(_dot(u[:, lo:hi], w_ref[lo:hi, c0:c1]))
    return jnp.concatenate(outs, axis=1)


def _rglru_kernel(x_ref, win_ref, cw_ref, cb_ref, wa_ref, ba_ref, wx_ref, bx_ref, lam_ref,
                  wout_ref, g_ref, b_ref, o_ref, carry_ref, h_ref, a_s, b_s):
    @pl.when(pl.program_id(1) == 0)
    def _():
        carry_ref[...] = jnp.zeros(carry_ref.shape, F32)
        h_ref[...] = jnp.zeros(h_ref.shape, F32)

    x = x_ref[0]
    tm = x.shape[0]
    xb = x.astype(MXU_DTYPE)
    w = LRU_WIDTH
    gate_br = _dot(xb, win_ref[:, 0:w])
    rec = _dot(xb, win_ref[:, w:2 * w])
    prev = carry_ref[...]
    u = cb_ref[...] + cw_ref[LRU_CONV - 1:LRU_CONV, :] * rec
    for k in range(1, LRU_CONV):
        u = u + cw_ref[LRU_CONV - 1 - k:LRU_CONV - k, :] * _shift_rows(rec, prev, k)
    carry_ref[...] = rec[tm - SUBLANES:, :]
    ub = u.astype(MXU_DTYPE)
    r = jax.nn.sigmoid(_block_diag_dot(ub, wa_ref) + ba_ref[...])
    ig = jax.nn.sigmoid(_block_diag_dot(ub, wx_ref) + bx_ref[...])
    log_a = LRU_C * r * jax.nn.log_sigmoid(lam_ref[...])
    a_s[...] = jnp.exp(log_a)
    th = jnp.tanh(log_a)
    b_s[...] = jnp.sqrt(-2.0 * th / (1.0 - th)) * (ig * u)

    rows = lax.broadcasted_iota(jnp.int32, (SUBLANES, 1), 0)

    def group(i, hc):
        r0 = pl.multiple_of(i * SUBLANES, SUBLANES)
        a = a_s[pl.ds(r0, SUBLANES), :]
        bv = b_s[pl.ds(r0, SUBLANES), :]
        for dlt in (1, 2, 4):
            a_sh = jnp.where(rows >= dlt, pltpu.roll(a, dlt, 0), 1.0)
            b_sh = jnp.where(rows >= dlt, pltpu.roll(bv, dlt, 0), 0.0)
            bv = a * b_sh + bv
            a = a * a_sh
        h = a * hc + bv
        b_s[pl.ds(r0, SUBLANES), :] = h
        return h[SUBLANES - 1:SUBLANES, :]

    h_ref[...] = lax.fori_loop(0, tm // SUBLANES, group, h_ref[...])
    y = jax.nn.gelu(gate_br) * b_s[...]
    m = _dot(y.astype(MXU_DTYPE), wout_ref[...])
    o_ref[0] = _layer_norm_rows(DN_ALPHA * x + m, g_ref[...], b_ref[...])


def _rglru_layer(x3, win, cw, cb, wa, ba, wx, bx, lam, wout, g, b):
    bsz, s, d = x3.shape
    tm = LRU_ROW_TILE
    xspec = pl.BlockSpec((1, tm, d), lambda i, j: (i, j, 0))
    consts = [win, cw, cb, wa, ba, wx, bx, lam, wout, g, b]
    return pl.pallas_call(
        _rglru_kernel,
        grid=(bsz, s // tm),
        in_specs=[xspec] + [_const_spec(c.shape) for c in consts],
        out_specs=xspec,
        out_shape=jax.ShapeDtypeStruct(x3.shape, F32),
        scratch_shapes=[pltpu.VMEM((SUBLANES, LRU_WIDTH), F32), pltpu.VMEM((1, LRU_WIDTH), F32),
                        pltpu.VMEM((tm, LRU_WIDTH), F32), pltpu.VMEM((tm, LRU_WIDTH), F32)],
        compiler_params=_cparams(("arbitrary", "arbitrary")),
        name="rglru_layer",
    )(x3, win, cw, cb, wa, ba, wx, bx, lam, wout, g, b)


def _moe_router_kernel(x_ref, rwt_ref, rb_ref, tri_ref, pos_ref, wts_ref, meta_ref):
    tt = x_ref.shape[0]
    x = x_ref[...]
    w = rwt_ref[...]
    xh = x.astype(MXU_DTYPE)
    wh = w.astype(MXU_DTYPE)
    xl = (x - xh.astype(F32)).astype(MXU_DTYPE)
    wl = (w - wh.astype(F32)).astype(MXU_DTYPE)
    logits = _dot_nt(wh, xh) + _dot_nt(wh, xl) + _dot_nt(wl, xh)
    scores = jax.nn.sigmoid(logits)
    biased = scores + rb_ref[...]
    rows = [biased[e:e + 1, :] for e in range(N_EXPERTS)]

    gscore = []
    for gi in range(N_GROUPS):
        a, b, c, d = rows[gi * EXPERTS_PER_GROUP:(gi + 1) * EXPERTS_PER_GROUP]
        hi1, lo1 = jnp.maximum(a, b), jnp.minimum(a, b)
        hi2, lo2 = jnp.maximum(c, d), jnp.minimum(c, d)
        gscore.append(jnp.maximum(hi1, hi2) + jnp.maximum(jnp.minimum(hi1, hi2),
                                                          jnp.maximum(lo1, lo2)))
    best = gscore[0]
    top_g = jnp.zeros((1, tt), jnp.int32)
    for gi in range(1, N_GROUPS):
        better = gscore[gi] > best
        best = jnp.where(better, gscore[gi], best)
        top_g = jnp.where(better, gi, top_g)

    def argmax_in_group(exclude):
        bv = jnp.full((1, tt), -jnp.inf, F32)
        bi = jnp.full((1, tt), -1, jnp.int32)
        for e in range(N_EXPERTS):
            ok = top_g == (e // EXPERTS_PER_GROUP)
            if exclude is not None:
                ok = ok & (exclude != e)
            better = ok & (rows[e] > bv)
            bv = jnp.where(better, rows[e], bv)
            bi = jnp.where(better, e, bi)
        return bi

    e1 = argmax_in_group(None)
    e2 = argmax_in_group(e1)
    eidx = lax.broadcasted_iota(jnp.int32, (N_EXPERTS, 1), 0)
    oh1 = eidx == e1
    oh2 = eidx == e2
    s1 = jnp.sum(jnp.where(oh1, scores, 0.0), axis=0, keepdims=True)
    s2 = jnp.sum(jnp.where(oh2, scores, 0.0), axis=0, keepdims=True)
    wsum = s1 + s2
    wts_ref[0, 0:1, :] = s1 / wsum
    wts_ref[0, 1:2, :] = s2 / wsum

    oh = jnp.where(oh1 | oh2, 1.0, 0.0)
    csum = _dot(oh.astype(MXU_DTYPE), tri_ref[...])
    excl = csum - oh
    cnt = csum[:, tt - 1:tt]
    padded = jnp.ceil(cnt * (1.0 / MOE_ALIGN)) * MOE_ALIGN
    run = jnp.zeros((1, 1), F32)
    off_rows = []
    for e in range(N_EXPERTS):
        off_rows.append(run)
        run = run + padded[e:e + 1, :]
    off = jnp.concatenate(off_rows, axis=0)
    slot = off + excl

    def flat_row_of(onehot):
        p = jnp.sum(jnp.where(onehot, slot, 0.0), axis=0, keepdims=True).astype(jnp.int32)
        return _flat_row(p)

    pos_ref[0, 0:1, :] = flat_row_of(oh1)
    pos_ref[0, 1:2, :] = flat_row_of(oh2)
    lane = lax.broadcasted_iota(jnp.int32, (1, LANES), 1)
    meta = jnp.zeros((1, LANES), F32)
    for e in range(N_EXPERTS):
        meta = jnp.where(lane == e, off[e:e + 1, :], meta)
        meta = jnp.where(lane == N_EXPERTS + e, cnt[e:e + 1, :], meta)
    meta_ref[0] = meta.astype(jnp.int32)


def _moe_router(x2, rwt, rb, tri):
    t = x2.shape[0]
    tt = tri.shape[0]
    nt = t // tt
    return pl.pallas_call(
        _moe_router_kernel,
        grid=(nt,),
        in_specs=[pl.BlockSpec((tt, D_MODEL), lambda i: (i, 0)), _const_spec(rwt.shape),
                  _const_spec(rb.shape), _const_spec(tri.shape)],
        out_specs=[pl.BlockSpec((1, TOP_K, tt), lambda i: (i, 0, 0)),
                   pl.BlockSpec((1, TOP_K, tt), lambda i: (i, 0, 0)),
                   pl.BlockSpec((1, 1, LANES), lambda i: (i, 0, 0))],
        out_shape=[jax.ShapeDtypeStruct((nt, TOP_K, tt), jnp.int32),
                   jax.ShapeDtypeStruct((nt, TOP_K, tt), F32),
                   jax.ShapeDtypeStruct((nt, 1, LANES), jnp.int32)],
        compiler_params=_cparams(("arbitrary",)),
        name="moe_router",
    )(x2, rwt, rb, tri)


def _flat_row(r):
    return (r >> _SUBLANE_SHIFT) * _TILE_ROW + (r & (SUBLANES - 1))


def _row_slice(q):
    return pl.ds(q, _LANE_TILES, stride=SUBLANES)


def _load_rows_flat(ref, r0, n):
    base = pl.multiple_of(r0 * _LANE_TILES, _TILE_ROW)
    cols = [jnp.concatenate([ref[pl.ds(base + i * _TILE_ROW + c * SUBLANES, SUBLANES), :]
                             for i in range(n // SUBLANES)], axis=0) for c in range(_LANE_TILES)]
    return jnp.concatenate(cols, axis=1)


def _store_rows_flat(ref, r0, val):
    base = pl.multiple_of(r0 * _LANE_TILES, _TILE_ROW)
    for c in range(_LANE_TILES):
        for i in range(val.shape[0] // SUBLANES):
            ref[pl.ds(base + i * _TILE_ROW + c * SUBLANES, SUBLANES), :] = (
                val[i * SUBLANES:(i + 1) * SUBLANES, c * LANES:(c + 1) * LANES])


def _moe_ffn_kernel(pos_ref, wts_ref, meta_ref, x_ref, wg_hbm, wu_hbm, wd_hbm, g_ref, b_ref,
                    o_ref, xs_ref, xf_ref, w_ref, w_sem, *, layer):
    tt = x_ref.shape[0]
    ch = MOE_CHUNK

    def weight_copy(e, slot, k):
        src = (wg_hbm, wu_hbm, wd_hbm)[k]
        return pltpu.make_async_copy(src.at[layer, e], w_ref.at[slot, k], w_sem.at[slot, k])

    def start_weights(e, slot):
        for k in range(3):
            weight_copy(e, slot, k).start()

    def wait_weights(e, slot):
        for k in range(3):
            weight_copy(e, slot, k).wait()

    start_weights(0, 0)

    def token_group(body):
        def group(i, carry):
            base = pl.multiple_of(i * SUBLANES, SUBLANES)
            qbase = pl.multiple_of(i * _TILE_ROW, _TILE_ROW)
            for k in range(SUBLANES):
                body(base + k, qbase + k)
            return carry

        lax.fori_loop(0, tt // SUBLANES, group, 0)

    def to_flat(i, carry):
        r0 = pl.multiple_of(i * SUBLANES, SUBLANES)
        _store_rows_flat(xf_ref, r0, x_ref[pl.ds(r0, SUBLANES), :])
        return carry

    lax.fori_loop(0, tt // SUBLANES, to_flat, 0)

    def zero_gap(ei, carry):
        start = meta_ref[ei] + meta_ref[N_EXPERTS + ei]
        stop = jnp.where(ei == N_EXPERTS - 1, start + ch + MOE_ALIGN,
                         meta_ref[jnp.minimum(ei + 1, N_EXPERTS - 1)])

        def zrow(r, c):
            xs_ref[_row_slice(_flat_row(r)), :] = jnp.zeros((_LANE_TILES, LANES), F32)
            return c

        return lax.fori_loop(start, stop, zrow, carry)

    lax.fori_loop(0, N_EXPERTS, zero_gap, 0)

    def scatter(t, q):
        row = xf_ref[_row_slice(q), :]
        xs_ref[_row_slice(pos_ref[t]), :] = row
        xs_ref[_row_slice(pos_ref[tt + t]), :] = row

    token_group(scatter)

    rid = lax.broadcasted_iota(jnp.int32, (ch, 1), 0)

    def expert(e, carry):
        slot = e % 2

        @pl.when(e + 1 < N_EXPERTS)
        def _():
            start_weights(e + 1, 1 - slot)

        wait_weights(e, slot)
        off = meta_ref[e]
        cnt = meta_ref[N_EXPERTS + e]

        def chunk(c, carry_c):
            r0 = pl.multiple_of(off + c * ch, MOE_ALIGN)
            xs = _load_rows_flat(xs_ref, r0, ch)
            xb = xs.astype(MXU_DTYPE)
            hid = jax.nn.silu(_dot(xb, w_ref[slot, 0])) * _dot(xb, w_ref[slot, 1])
            out = _dot(hid.astype(MXU_DTYPE), w_ref[slot, 2])
            _store_rows_flat(xs_ref, r0, jnp.where(rid + c * ch < cnt, out, xs))
            return carry_c

        return lax.fori_loop(0, (cnt + ch - 1) // ch, chunk, carry)

    lax.fori_loop(0, N_EXPERTS, expert, 0)

    def gather(t, q):
        xf_ref[_row_slice(q), :] = (wts_ref[t] * xs_ref[_row_slice(pos_ref[t]), :]
                                    + wts_ref[tt + t] * xs_ref[_row_slice(pos_ref[tt + t]), :])

    token_group(gather)

    def norm(i, carry):
        r0 = pl.multiple_of(i * MOE_NORM_ROWS, MOE_NORM_ROWS)
        y = (DN_ALPHA * x_ref[pl.ds(r0, MOE_NORM_ROWS), :]
             + _load_rows_flat(xf_ref, r0, MOE_NORM_ROWS))
        o_ref[pl.ds(r0, MOE_NORM_ROWS), :] = _layer_norm_rows(y, g_ref[...], b_ref[...])
        return carry

    lax.fori_loop(0, tt // MOE_NORM_ROWS, norm, 0)


def _moe_ffn(pos, wts, meta, x2, layer, wg, wu, wd, g, b):
    t = x2.shape[0]
    tt = pos.shape[2]
    nt = t // tt
    smem = lambda n: pl.BlockSpec((n,), lambda i: (i,), memory_space=pltpu.SMEM)
    hbm = pl.BlockSpec(memory_space=pl.ANY)
    xs_rows = TOP_K * tt + N_EXPERTS * MOE_ALIGN + MOE_CHUNK + MOE_ALIGN
    pos, wts, meta = pos.reshape(-1), wts.reshape(-1), meta.reshape(-1)
    assert wg.shape[2:] == wu.shape[2:] == (D_MODEL, D_FF) and wd.shape[2:] == (D_FF, D_MODEL)
    assert D_FF == D_MODEL
    return pl.pallas_call(
        functools.partial(_moe_ffn_kernel, layer=layer),
        grid=(nt,),
        in_specs=[smem(TOP_K * tt), smem(TOP_K * tt), smem(LANES),
                  pl.BlockSpec((tt, D_MODEL), lambda i: (i, 0), pipeline_mode=pl.Buffered(1)),
                  hbm, hbm, hbm, _const_spec((1, D_MODEL)), _const_spec((1, D_MODEL))],
        out_specs=pl.BlockSpec((tt, D_MODEL), lambda i: (i, 0), pipeline_mode=pl.Buffered(1)),
        out_shape=jax.ShapeDtypeStruct((t, D_MODEL), F32),
        scratch_shapes=[pltpu.VMEM((xs_rows * _LANE_TILES, LANES), F32),
                        pltpu.VMEM((tt * _LANE_TILES, LANES), F32),
                        pltpu.VMEM((2, 3, D_MODEL, D_FF), MXU_DTYPE),
                        pltpu.SemaphoreType.DMA((2, 3))],
        compiler_params=_cparams(("arbitrary",)),
        name="moe_ffn",
    )(pos, wts, meta, x2, wg, wu, wd, g, b)


def _nsa_weight_layout(w_in, b_gate):
    G, R, Dh = NSA_GROUPS, NSA_HPG, HEAD_DIM
    d_in = w_in.shape[0]
    kvw = NSA_KV_COLS
    wq = w_in[:, :NSA_Q_COLS].reshape(d_in, G, R, Dh).transpose(0, 2, 1, 3).reshape(d_in, NSA_Q_COLS)
    kc, vc, ks, vs, kw, vw = (w_in[:, NSA_Q_COLS + i * kvw:NSA_Q_COLS + (i + 1) * kvw] for i in range(6))
    gate_base = NSA_Q_COLS + 6 * kvw

    def gate_layout(v):
        lead = v.shape[:-1]
        v = jnp.swapaxes(v.reshape(lead + (G, R, 3)), -1, -2).reshape(lead + (G, 3 * R))
        v = jnp.pad(v, [(0, 0)] * (len(lead) + 1) + [(0, LANES - 3 * R)])
        return v.reshape(lead + (_NSA_GATE_COLS,))

    wg = gate_layout(w_in[:, gate_base:gate_base + 3 * NSA_HEADS])
    w = jnp.concatenate([wq, ks, kw, kc, vc, vs, vw, wg], axis=1).astype(MXU_DTYPE)
    return w, gate_layout(b_gate)[None, :]


def _nsa_out_layout(w_out):
    G, R, Dh = NSA_GROUPS, NSA_HPG, HEAD_DIM
    d_out = w_out.shape[1]
    w = w_out.reshape(G, R, Dh, d_out).transpose(1, 0, 2, 3).reshape(NSA_Q_COLS, d_out)
    return w.astype(MXU_DTYPE)


def _compress_weight_layout(pe, w1, w2):
    G, Dh, L = NSA_GROUPS, HEAD_DIM, CMP_BLOCK
    eye = jnp.eye(G, dtype=w1.dtype)
    w1r = w1.reshape(L, Dh, Dh)
    w1big = jnp.einsum("ldj,gh->lgdhj", w1r, eye).reshape(L * G * Dh, G * Dh)
    w2big = jnp.einsum("dj,gh->gdhj", w2, eye).reshape(G * Dh, G * Dh)
    pebig = jnp.broadcast_to(pe[:, None, :], (L, G, Dh)).reshape(1, L * G * Dh)
    return pebig, w1big.astype(MXU_DTYPE), w2big.astype(MXU_DTYPE)


def _block_diag(w):
    h, bi, bj = w.shape
    eye = jnp.eye(h, dtype=w.dtype)
    return jnp.einsum("hij,hk->hikj", w, eye).reshape(h * bi, h * bj).astype(MXU_DTYPE)


def _nsa_constants(s):
    nc = s // CMP_STRIDE
    n_cmp = (s - CMP_BLOCK) // CMP_STRIDE + 1
    n_sel = s // SEL_BLOCK
    cmp_start = np.arange(nc) * CMP_STRIDE
    cmp_end = cmp_start + CMP_BLOCK - 1
    sel_start = np.arange(n_sel) * SEL_BLOCK
    ov = ((cmp_start[:, None] <= sel_start[None, :] + SEL_BLOCK - 1)
          & (cmp_end[:, None] >= sel_start[None, :])
          & (np.arange(nc)[:, None] < n_cmp)).astype(np.float32)
    ov = np.pad(ov, ((0, 0), (0, LANES - n_sel)))
    et = (np.arange(s)[:, None] // SEL_BLOCK == np.arange(LANES)[None, :]).astype(np.float32)
    return jnp.asarray(ov), jnp.asarray(et, dtype=MXU_DTYPE)


def _nsa_layer(x2, bsz, s, tables, cmp_tables, w_in, b_gate, pos_k, w1_k, w2_k, pos_v, w1_v, w2_v,
               w_out, g, b):
    w_perm, bg_perm = _nsa_weight_layout(w_in, b_gate)
    q, ks, kw, kc, vc, vs, vw, gates = _nsa_proj(x2, w_perm, bg_perm, *tables)
    nc = s // CMP_STRIDE
    feat = CMP_STRIDE * NSA_KV_COLS
    pek, w1k, w2k = _compress_weight_layout(pos_k, w1_k, w2_k)
    pev, w1v, w2v = _compress_weight_layout(pos_v, w1_v, w2_v)
    kcmp, vcmp = _nsa_compress(kc.reshape(bsz, nc, feat), vc.reshape(bsz, nc, feat),
                               pek, pev, w1k, w1v, w2k, w2v, *cmp_tables)
    ov, et = _nsa_constants(s)
    r3 = lambda a: a.reshape(bsz, s, a.shape[-1])
    o = _nsa_attn(r3(q), r3(ks), r3(vs), r3(kw), r3(vw), kcmp, vcmp, r3(gates), ov, et)
    return _mm_res_ln(o.reshape(bsz * s, NSA_Q_COLS), _nsa_out_layout(w_out), x2, g, b)


def _moe_layer(x2, rwt, rb, tri, layer, wg, wu, wd, g, b):
    pos, wts, meta = _moe_router(x2, rwt, rb, tri)
    return _moe_ffn(pos, wts, meta, x2, layer, wg, wu, wd, g, b)


def kernel(x, positions, nsa_w_in, nsa_b_gate, nsa_cmp_pos_k, nsa_cmp_w1_k, nsa_cmp_w2_k, nsa_cmp_pos_v, nsa_cmp_w1_v, nsa_cmp_w2_v, nsa_w_out, sc_w_in, sc_conv_w, sc_conv_b, sc_w_out, lru_w_in, lru_conv_w, lru_conv_b, lru_wa, lru_ba, lru_wx, lru_bx, lru_lambda, lru_w_out, router_w, router_b, moe_w_gate, moe_w_up, moe_w_down, ln_g, ln_b):
    bsz, s, d = x.shape
    t = bsz * s
    x2 = x.reshape(t, d)
    row = lambda v: v.reshape(1, -1)

    tables = _rope_tables(positions.reshape(t))
    nc = s // CMP_STRIDE
    cmp_idx = np.minimum(np.arange(nc) * CMP_STRIDE + CMP_BLOCK - 1, s - 1)
    cmp_tables = tuple(tb.reshape(bsz, nc, LANES)
                       for tb in _rope_tables(positions[:, cmp_idx].reshape(bsz * nc)))

    tt = min(MOE_TOKENS, t)
    tri = jnp.asarray(np.triu(np.ones((tt, tt), np.float32)), dtype=MXU_DTYPE)
    rwt = router_w.T
    rb = router_b.reshape(N_EXPERTS, 1)
    moe_wg, moe_wu, moe_wd = (w.astype(MXU_DTYPE) for w in (moe_w_gate, moe_w_up, moe_w_down))

    for i in range(DEPTH):
        kind, j = i % N_MIXERS, i // N_MIXERS
        g0, b0 = row(ln_g[i, 0]), row(ln_b[i, 0])
        if kind == 0:
            x2 = _nsa_layer(x2, bsz, s, tables, cmp_tables, nsa_w_in[j], nsa_b_gate[j],
                            nsa_cmp_pos_k[j], nsa_cmp_w1_k[j], nsa_cmp_w2_k[j],
                            nsa_cmp_pos_v[j], nsa_cmp_w1_v[j], nsa_cmp_w2_v[j], nsa_w_out[j], g0, b0)
        elif kind == 1:
            x2 = _shortconv_layer(x2.reshape(bsz, s, d), sc_w_in[j].astype(MXU_DTYPE), sc_conv_w[j],
                                  row(sc_conv_b[j]), sc_w_out[j].astype(MXU_DTYPE), g0, b0).reshape(t, d)
        else:
            x2 = _rglru_layer(x2.reshape(bsz, s, d), lru_w_in[j].astype(MXU_DTYPE), lru_conv_w[j],
                              row(lru_conv_b[j]), _block_diag(lru_wa[j]), row(lru_ba[j]),
                              _block_diag(lru_wx[j]), row(lru_bx[j]), row(lru_lambda[j]),
                              lru_w_out[j].astype(MXU_DTYPE), g0, b0).reshape(t, d)
        x2 = _moe_layer(x2, rwt, rb, tri, i, moe_wg, moe_wu, moe_wd, row(ln_g[i, 1]), row(ln_b[i, 1]))
    return x2.reshape(bsz, s, d)
```

```python
import functools

import numpy as np
import jax
import jax.numpy as jnp
from jax import lax
from jax.experimental import pallas as pl
from jax.experimental.pallas import tpu as pltpu

F32 = jnp.float32
BF16 = jnp.bfloat16
MXU_DTYPE = BF16

D_MODEL = 1024
DEPTH = 4
N_MIXERS = 3
NSA_HEADS = 16
NSA_GROUPS = 4
NSA_HPG = NSA_HEADS // NSA_GROUPS
HEAD_DIM = D_MODEL // NSA_HEADS
CMP_BLOCK = 32
CMP_STRIDE = 16
SEL_BLOCK = 64
SEL_TOP = 16
WINDOW = 512
ROPE_THETA = 10000.0
FORCE_SCORE = 1e6
NSA_Q_COLS = NSA_HEADS * HEAD_DIM
NSA_KV_COLS = NSA_GROUPS * HEAD_DIM
SC_WIDTH = 3
LRU_WIDTH = 1280
LRU_BLOCKS = 16
LRU_BW = LRU_WIDTH // LRU_BLOCKS
LRU_CONV = 4
LRU_C = 8.0
N_EXPERTS = 16
N_GROUPS = 4
EXPERTS_PER_GROUP = N_EXPERTS // N_GROUPS
TOP_K = 2
D_FF = 1024
DN_ALPHA = (2.0 * DEPTH) ** 0.25
LN_EPS = 1e-5
NEG_INF = -1e30
LOG2_E = 1.4426950408889634

LANES = 128
SUBLANES = 8
VMEM_LIMIT_BYTES = 56 * 1024 * 1024
ROW_TILE = 512
LRU_ROW_TILE = 256
ATT_TQ = 512
ATT_TK = 512
MOE_TOKENS = 2048
MOE_CHUNK = 288
MOE_NORM_ROWS = 128
MOE_IO_ROWS = 256
MOE_ALIGN = SUBLANES
_SUBLANE_SHIFT = 3
_LANE_TILES = D_MODEL // LANES
_TILE_ROW = SUBLANES * _LANE_TILES


def _cparams(sem):
    return pltpu.CompilerParams(dimension_semantics=sem, vmem_limit_bytes=VMEM_LIMIT_BYTES)


def _dot(a, b):
    return jnp.dot(a, b, preferred_element_type=F32)


def _dot_nt(a, b, precision=None):
    return lax.dot_general(a, b, (((1,), (1,)), ((), ())), preferred_element_type=F32,
                           precision=precision)


def _layer_norm_rows(y, g, b):
    mu = jnp.mean(y, axis=-1, keepdims=True)
    d = y - mu
    var = jnp.mean(d * d, axis=-1, keepdims=True)
    return d * lax.rsqrt(var + LN_EPS) * g + b


def _const_spec(shape):
    nd = len(shape)
    return pl.BlockSpec(shape, lambda *_: (0,) * nd)


def _rope_tables_kernel(pos_ref, inv_ref, cos_ref, sina_ref, sinb_ref):
    ang = pos_ref[...] * inv_ref[...]
    c = jnp.cos(ang)
    s = jnp.sin(ang)
    lane = lax.broadcasted_iota(jnp.int32, ang.shape, 1)
    first = (lane % HEAD_DIM) < (HEAD_DIM // 2)
    cos_ref[...] = c
    sina_ref[...] = jnp.where(first, -s, 0.0)
    sinb_ref[...] = jnp.where(first, 0.0, s)


def _rope_tables(pos_flat):
    n = pos_flat.shape[0]
    tm = min(n, 1024)
    half = HEAD_DIM // 2
    inv = ROPE_THETA ** (-jnp.arange(half, dtype=F32) / half)
    inv_lane = jnp.tile(inv, LANES // half)[None, :]
    posb = jnp.broadcast_to(pos_flat.astype(F32)[:, None], (n, LANES))
    spec = pl.BlockSpec((tm, LANES), lambda i: (i, 0))
    out = jax.ShapeDtypeStruct((n, LANES), F32)
    return pl.pallas_call(
        _rope_tables_kernel,
        grid=(n // tm,),
        in_specs=[spec, _const_spec((1, LANES))],
        out_specs=[spec, spec, spec],
        out_shape=[out, out, out],
        compiler_params=_cparams(("arbitrary",)),
        name="rope_tables",
    )(posb, inv_lane)


def _rope128(v, cos, sina, sinb):
    half = HEAD_DIM // 2
    return (v * cos + pltpu.roll(v, LANES - half, 1) * sina + pltpu.roll(v, half, 1) * sinb)


_NSA_SEG = {
    "q": 0, "ks": 1024, "kw": 1280, "kc": 1536, "vc": 1792, "vs": 2048, "vw": 2304, "gate": 2560,
}
_NSA_GATE_COLS = NSA_GROUPS * LANES
_NSA_PROJ_COLS = 2560 + _NSA_GATE_COLS


def _nsa_proj_kernel(x_ref, w_ref, bg_ref, cos_ref, sina_ref, sinb_ref,
                     q_ref, ks_ref, kw_ref, kc_ref, vc_ref, vs_ref, vw_ref, g_ref, chunk_ref):
    xb = x_ref[...].astype(MXU_DTYPE)
    cos = cos_ref[...]
    sina = sina_ref[...]
    sinb = sinb_ref[...]
    scale = HEAD_DIM ** -0.5 * LOG2_E

    def seg(name, width=256, c=0):
        o = _NSA_SEG[name] + c * 256
        return _dot(xb, w_ref[:, o:o + width])

    for c in range(NSA_Q_COLS // 256):
        acc = seg("q", c=c)
        for h in range(2):
            r = _rope128(acc[:, h * LANES:(h + 1) * LANES], cos, sina, sinb) * scale
            q_ref[:, c * 256 + h * LANES:c * 256 + (h + 1) * LANES] = r.astype(q_ref.dtype)
    for name, ref in (("ks", ks_ref), ("kw", kw_ref)):
        acc = seg(name)
        for h in range(2):
            r = _rope128(acc[:, h * LANES:(h + 1) * LANES], cos, sina, sinb)
            ref[:, h * LANES:(h + 1) * LANES] = r.astype(ref.dtype)
    n_chunks = kc_ref.shape[0]
    for name, ref in (("kc", kc_ref), ("vc", vc_ref)):
        acc = seg(name)
        for h in range(2):
            chunk_ref[h] = acc[:, h * LANES:(h + 1) * LANES]
        for l in range(CMP_STRIDE):
            for h in range(2):
                lo = l * NSA_KV_COLS + h * LANES
                ref[:, lo:lo + LANES] = chunk_ref[h, pl.ds(l, n_chunks, stride=CMP_STRIDE), :]
    vs_ref[...] = seg("vs").astype(vs_ref.dtype)
    vw_ref[...] = seg("vw").astype(vw_ref.dtype)
    g_ref[...] = jax.nn.sigmoid(seg("gate", width=_NSA_GATE_COLS) + bg_ref[...])


def _nsa_proj(x2, w_perm, bg_perm, cos_t, sina_t, sinb_t):
    t = x2.shape[0]
    tm = ROW_TILE
    row = lambda w: pl.BlockSpec((tm, w), lambda i: (i, 0))
    chunk_cols = CMP_STRIDE * NSA_KV_COLS
    chunk = pl.BlockSpec((tm // CMP_STRIDE, chunk_cols), lambda i: (i, 0))
    outs = [
        jax.ShapeDtypeStruct((t, NSA_Q_COLS), MXU_DTYPE),
        jax.ShapeDtypeStruct((t, NSA_KV_COLS), MXU_DTYPE),
        jax.ShapeDtypeStruct((t, NSA_KV_COLS), MXU_DTYPE),
        jax.ShapeDtypeStruct((t // CMP_STRIDE, chunk_cols), F32),
        jax.ShapeDtypeStruct((t // CMP_STRIDE, chunk_cols), F32),
        jax.ShapeDtypeStruct((t, NSA_KV_COLS), MXU_DTYPE),
        jax.ShapeDtypeStruct((t, NSA_KV_COLS), MXU_DTYPE),
        jax.ShapeDtypeStruct((t, _NSA_GATE_COLS), F32),
    ]
    return pl.pallas_call(
        _nsa_proj_kernel,
        grid=(t // tm,),
        in_specs=[row(D_MODEL), _const_spec((D_MODEL, _NSA_PROJ_COLS)),
                  _const_spec((1, _NSA_GATE_COLS)), row(LANES), row(LANES), row(LANES)],
        out_specs=[row(NSA_Q_COLS), row(NSA_KV_COLS), row(NSA_KV_COLS), chunk, chunk,
                   row(NSA_KV_COLS), row(NSA_KV_COLS), row(_NSA_GATE_COLS)],
        out_shape=outs,
        scratch_shapes=[pltpu.VMEM((2, tm, LANES), F32)],
        compiler_params=_cparams(("arbitrary",)),
        name="nsa_proj",
    )(x2, w_perm, bg_perm, cos_t, sina_t, sinb_t)


def _nsa_compress_kernel(kc_ref, vc_ref, pek_ref, pev_ref, w1k_ref, w1v_ref, w2k_ref, w2v_ref,
                         cos_ref, sina_ref, sinb_ref, ko_ref, vo_ref):
    nc = kc_ref.shape[1]
    half_feat = kc_ref.shape[2]
    rows = lax.broadcasted_iota(jnp.int32, (nc, 1), 0)
    valid = rows < nc - 1

    def compress(tok, pe_ref, w1_ref, w2_ref):
        a = (tok + pe_ref[:, :half_feat]).astype(MXU_DTYPE)
        b = (tok + pe_ref[:, half_feat:]).astype(MXU_DTYPE)
        pa = _dot(a, w1_ref[:half_feat, :])
        pb = _dot(b, w1_ref[half_feat:, :])
        pre = pa + pltpu.roll(pb, nc - 1, 0)
        h = jax.nn.gelu(pre)
        return _dot(h.astype(MXU_DTYPE), w2_ref[...])

    kcmp = compress(kc_ref[0], pek_ref, w1k_ref, w2k_ref)
    vcmp = compress(vc_ref[0], pev_ref, w1v_ref, w2v_ref)
    cos = cos_ref[0]
    sina = sina_ref[0]
    sinb = sinb_ref[0]
    for h in range(NSA_KV_COLS // LANES):
        r = _rope128(kcmp[:, h * LANES:(h + 1) * LANES], cos, sina, sinb)
        ko_ref[0, :, h * LANES:(h + 1) * LANES] = jnp.where(valid, r, 0.0).astype(ko_ref.dtype)
    vo_ref[0] = jnp.where(valid, vcmp, 0.0).astype(vo_ref.dtype)


def _nsa_compress(kc3, vc3, pek, pev, w1k, w1v, w2k, w2v, cos_c, sina_c, sinb_c):
    b, nc, feat = kc3.shape
    tok = pl.BlockSpec((1, nc, feat), lambda i: (i, 0, 0))
    tab = pl.BlockSpec((1, nc, LANES), lambda i: (i, 0, 0))
    out = pl.BlockSpec((1, nc, NSA_KV_COLS), lambda i: (i, 0, 0))
    oshape = jax.ShapeDtypeStruct((b, nc, NSA_KV_COLS), MXU_DTYPE)
    return pl.pallas_call(
        _nsa_compress_kernel,
        grid=(b,),
        in_specs=[tok, tok, _const_spec(pek.shape), _const_spec(pev.shape),
                  _const_spec(w1k.shape), _const_spec(w1v.shape),
                  _const_spec(w2k.shape), _const_spec(w2v.shape), tab, tab, tab],
        out_specs=[out, out],
        out_shape=[oshape, oshape],
        compiler_params=_cparams(("arbitrary",)),
        name="nsa_compress",
    )(kc3, vc3, pek, pev, w1k, w1v, w2k, w2v, cos_c, sina_c, sinb_c)


def _nsa_attn_kernel(q_ref, ks_ref, vs_ref, kw_ref, vw_ref, kc_ref, vc_ref, g_ref, ov_ref, et_ref,
                     o_ref, m_ref, acc_ref, sel_ref, *, n_sel, k_sel):
    tq = q_ref.shape[1]
    tk = ATT_TK
    nc = kc_ref.shape[1]
    R = NSA_HPG
    qi = pl.program_id(1)
    g = pl.program_id(2)
    t0 = qi * tq
    trow = t0 + lax.broadcasted_iota(jnp.int32, (tq, 1), 0)
    gmask = lax.broadcasted_iota(jnp.int32, (1, NSA_KV_COLS), 1) // HEAD_DIM == g
    gates = g_ref[0]
    low_half = g < NSA_GROUPS // 2
    in_group = (lax.broadcasted_iota(jnp.int32, (1, LANES), 1) // HEAD_DIM
                == g % (LANES // HEAD_DIM))

    def value_tile(v):
        return jnp.where(in_group, jnp.where(low_half, v[:, :LANES], v[:, LANES:]), 1)

    qs = jnp.concatenate(
        [jnp.where(gmask, q_ref[0, :, r * NSA_KV_COLS:(r + 1) * NSA_KV_COLS], 0)
         for r in range(R)], axis=0)
    rows = lambda a, r: a[r * tq:(r + 1) * tq]

    def tile_update(c, k_ref, v_ref, k0, bias):
        kt = k_ref[0, pl.ds(k0, tk), :]
        vt = value_tile(v_ref[0, pl.ds(k0, tk), :])
        s_all = _dot_nt(qs, kt)
        ps, alphas = [], []
        for r in range(R):
            s = rows(s_all, r) + bias
            m_old = m_ref[c, r]
            m_new = jnp.maximum(m_old, jnp.max(s, axis=-1, keepdims=True))
            m_ref[c, r] = m_new
            alphas.append(jnp.exp2(m_old - m_new))
            m_wide = jnp.concatenate([m_new] * (tk // LANES), axis=1)
            ps.append(jnp.exp2(s - m_wide).astype(MXU_DTYPE))
        pv = _dot(jnp.concatenate(ps, axis=0), vt)
        for r in range(R):
            acc_ref[c, r] = alphas[r] * acc_ref[c, r] + rows(pv, r)

    n_idx = lax.broadcasted_iota(jnp.int32, (1, nc), 1)
    cmp_ok = (n_idx * CMP_STRIDE + CMP_BLOCK - 1) <= trow
    s_all = _dot_nt(qs, kc_ref[0])
    psum = jnp.zeros((tq, nc), F32)
    ps = []
    for r in range(R):
        s = jnp.where(cmp_ok, rows(s_all, r), NEG_INF)
        e = jnp.where(cmp_ok, jnp.exp2(s - jnp.max(s, axis=-1, keepdims=True)), 0.0)
        p = e * (1.0 / jnp.maximum(jnp.sum(e, axis=-1, keepdims=True), 1e-30))
        psum = psum + p
        ps.append(p.astype(MXU_DTYPE))
    o_cmp = _dot(jnp.concatenate(ps, axis=0), value_tile(vc_ref[0]))

    all_selected = t0 + tq <= k_sel * SEL_BLOCK

    @pl.when(all_selected)
    def _():
        sel_ref[...] = jnp.ones(sel_ref.shape, sel_ref.dtype)

    @pl.when(jnp.logical_not(all_selected))
    def _():
        imp = jnp.dot(psum, ov_ref[...], preferred_element_type=F32,
                      precision=lax.Precision.HIGHEST)
        blk = lax.broadcasted_iota(jnp.int32, (1, LANES), 1)
        rel = trow // SEL_BLOCK - blk
        imp = jnp.where(rel >= 0, imp, -1.0)
        forced = (blk == 0) | (rel == 0) | (rel == 1)
        imp = jnp.where(forced, FORCE_SCORE, imp)
        imp_t = jnp.transpose(imp)[:n_sel, :]
        midx = lax.broadcasted_iota(jnp.int32, (n_sel, 1), 0)
        cnt = jnp.zeros((n_sel, tq), F32)
        for mp in range(n_sel):
            row = imp_t[mp:mp + 1, :]
            beats = (row > imp_t) | ((row == imp_t) & (midx > mp))
            cnt = cnt + jnp.where(beats, 1.0, 0.0)
        sel_t = jnp.where(cnt < k_sel, 1.0, 0.0)
        if n_sel < LANES:
            sel_t = jnp.concatenate([sel_t, jnp.zeros((LANES - n_sel, tq), F32)], axis=0)
        sel_ref[...] = jnp.transpose(sel_t).astype(sel_ref.dtype)

    sel = sel_ref[...]

    SEL, WIN = 0, 1
    m_ref[...] = jnp.full(m_ref.shape, NEG_INF, F32)
    acc_ref[...] = jnp.zeros(acc_ref.shape, F32)

    def sel_update(k0, kpos):
        selm = _dot_nt(sel, et_ref[pl.ds(k0, tk), :])
        bias = jnp.where((selm > 0.5) & (kpos <= trow), 0.0, NEG_INF)
        tile_update(SEL, ks_ref, vs_ref, k0, bias)

    def win_update(k0, kpos):
        d = trow - kpos
        bias = jnp.where((d >= 0) & (d < WINDOW), 0.0, NEG_INF)
        tile_update(WIN, kw_ref, vw_ref, k0, bias)

    def key_tile(j):
        k0 = pl.multiple_of(j * tk, tk)
        return k0, k0 + lax.broadcasted_iota(jnp.int32, (1, tk), 1)

    def sel_only(j, carry):
        sel_update(*key_tile(j))
        return carry

    def sel_and_win(j, carry):
        k0, kpos = key_tile(j)
        sel_update(k0, kpos)
        win_update(k0, kpos)
        return carry

    j_win = jnp.maximum(t0 - WINDOW, 0) // tk
    lax.fori_loop(0, j_win, sel_only, 0)
    lax.fori_loop(j_win, (t0 + tq) // tk, sel_and_win, 0)

    def normalized(c, r):
        acc = acc_ref[c, r]
        return acc * (1.0 / pltpu.roll(acc, HEAD_DIM, 1))

    o_heads = [gates[:, r:r + 1] * rows(o_cmp, r)
               + gates[:, R + r:R + r + 1] * normalized(SEL, r)
               + gates[:, 2 * R + r:2 * R + r + 1] * normalized(WIN, r) for r in range(R)]

    for gg in range(NSA_GROUPS):
        @pl.when(g == gg)
        def _(gg=gg):
            src = gg % (LANES // HEAD_DIM) * HEAD_DIM
            for r in range(R):
                lo = r * NSA_KV_COLS + gg * HEAD_DIM
                o_ref[0, :, lo:lo + HEAD_DIM] = o_heads[r][:, src:src + HEAD_DIM].astype(o_ref.dtype)


def _nsa_attn(q, ks, vs, kw, vw, kcmp, vcmp, gates, ov, et):
    b, s, _ = q.shape
    nc = kcmp.shape[1]
    tq = ATT_TQ
    n_sel = s // SEL_BLOCK
    k_sel = min(SEL_TOP, n_sel)
    qspec = pl.BlockSpec((1, tq, NSA_Q_COLS), lambda i, j, g: (i, j, 0))
    kv = pl.BlockSpec((1, s, NSA_KV_COLS), lambda i, j, g: (i, 0, 0))
    cmp_spec = pl.BlockSpec((1, nc, NSA_KV_COLS), lambda i, j, g: (i, 0, 0))
    gspec = pl.BlockSpec((1, tq, LANES), lambda i, j, g: (i, j, g))
    return pl.pallas_call(
        functools.partial(_nsa_attn_kernel, n_sel=n_sel, k_sel=k_sel),
        grid=(b, s // tq, NSA_GROUPS),
        in_specs=[qspec, kv, kv, kv, kv, cmp_spec, cmp_spec, gspec,
                  _const_spec(ov.shape), _const_spec(et.shape)],
        out_specs=qspec,
        out_shape=jax.ShapeDtypeStruct((b, s, NSA_Q_COLS), MXU_DTYPE),
        scratch_shapes=[pltpu.VMEM((2, NSA_HPG, tq, LANES), F32),
                        pltpu.VMEM((2, NSA_HPG, tq, LANES), F32),
                        pltpu.VMEM((tq, LANES), MXU_DTYPE)],
        compiler_params=_cparams(("arbitrary", "arbitrary", "arbitrary")),
        name="nsa_attn",
    )(q, ks, vs, kw, vw, kcmp, vcmp, gates, ov, et)


def _mm_res_ln_kernel(a_ref, w_ref, x_ref, g_ref, b_ref, o_ref):
    m = _dot(a_ref[...].astype(MXU_DTYPE), w_ref[...])
    o_ref[...] = _layer_norm_rows(DN_ALPHA * x_ref[...] + m, g_ref[...], b_ref[...])


def _mm_res_ln(a, w, x2, g, b):
    t, k = a.shape
    tm = ROW_TILE
    return pl.pallas_call(
        _mm_res_ln_kernel,
        grid=(t // tm,),
        in_specs=[pl.BlockSpec((tm, k), lambda i: (i, 0)), _const_spec(w.shape),
                  pl.BlockSpec((tm, D_MODEL), lambda i: (i, 0)),
                  _const_spec((1, D_MODEL)), _const_spec((1, D_MODEL))],
        out_specs=pl.BlockSpec((tm, D_MODEL), lambda i: (i, 0)),
        out_shape=jax.ShapeDtypeStruct((t, D_MODEL), F32),
        compiler_params=_cparams(("arbitrary",)),
        name="mm_res_ln",
    )(a, w, x2, g, b)


def _shift_rows(v, prev, k):
    out = pltpu.roll(v, k, 0)
    rows = lax.broadcasted_iota(jnp.int32, (v.shape[0], 1), 0)
    np_rows = prev.shape[0]
    for i in range(k):
        out = jnp.where(rows == i, prev[np_rows - k + i:np_rows - k + i + 1, :], out)
    return out


def _shortconv_kernel(x_ref, win_ref, cw_ref, cb_ref, wout_ref, g_ref, b_ref, o_ref, carry_ref):
    @pl.when(pl.program_id(1) == 0)
    def _():
        carry_ref[...] = jnp.zeros(carry_ref.shape, F32)

    x = x_ref[0]
    xb = x.astype(MXU_DTYPE)
    d = D_MODEL
    u = _dot(xb, win_ref[:, 0:d])
    bg = _dot(xb, win_ref[:, d:2 * d])
    cg = _dot(xb, win_ref[:, 2 * d:3 * d])
    v = cg * u
    prev = carry_ref[...]
    z = cb_ref[...] + cw_ref[SC_WIDTH - 1:SC_WIDTH, :] * v
    for k in range(1, SC_WIDTH):
        z = z + cw_ref[SC_WIDTH - 1 - k:SC_WIDTH - k, :] * _shift_rows(v, prev, k)
    carry_ref[...] = v[v.shape[0] - SUBLANES:, :]
    m = _dot((bg * z).astype(MXU_DTYPE), wout_ref[...])
    o_ref[0] = _layer_norm_rows(DN_ALPHA * x + m, g_ref[...], b_ref[...])


def _shortconv_layer(x3, win, cw, cb, wout, g, b):
    bsz, s, d = x3.shape
    tm = ROW_TILE
    xspec = pl.BlockSpec((1, tm, d), lambda i, j: (i, j, 0))
    return pl.pallas_call(
        _shortconv_kernel,
        grid=(bsz, s // tm),
        in_specs=[xspec, _const_spec(win.shape), _const_spec(cw.shape), _const_spec(cb.shape),
                  _const_spec(wout.shape), _const_spec(g.shape), _const_spec(b.shape)],
        out_specs=xspec,
        out_shape=jax.ShapeDtypeStruct(x3.shape, F32),
        scratch_shapes=[pltpu.VMEM((SUBLANES, d), F32)],
        compiler_params=_cparams(("arbitrary", "arbitrary")),
        name="shortconv_layer",
    )(x3, win, cw, cb, wout, g, b)


def _round_down(x, m):
    return x // m * m


def _round_up(x, m):
    return -(-x // m) * m


def _block_diag_dot(u, w_ref):
    tile = 2 * LANES
    outs = []
    for c0 in range(0, LRU_WIDTH, tile):
        c1 = min(c0 + tile, LRU_WIDTH)
        lo = _round_down(_round_down(c0, LRU_BW), LANES)
        hi = min(_round_up(_round_up(c1, LRU_BW), LANES), LRU_WIDTH)
        outs.append(_dot(u[:, lo:hi], w_ref[lo:hi, c0:c1]))
    return jnp.concatenate(outs, axis=1)


def _rglru_kernel(x_ref, win_ref, cw_ref, cb_ref, wa_ref, ba_ref, wx_ref, bx_ref, lam_ref,
                  wout_ref, g_ref, b_ref, o_ref, carry_ref, h_ref, a_s, b_s):
    @pl.when(pl.program_id(1) == 0)
    def _():
        carry_ref[...] = jnp.zeros(carry_ref.shape, F32)
        h_ref[...] = jnp.zeros(h_ref.shape, F32)

    x = x_ref[0]
    tm = x.shape[0]
    xb = x.astype(MXU_DTYPE)
    w = LRU_WIDTH
    gate_br = _dot(xb, win_ref[:, 0:w])
    rec = _dot(xb, win_ref[:, w:2 * w])
    prev = carry_ref[...]
    u = cb_ref[...] + cw_ref[LRU_CONV - 1:LRU_CONV, :] * rec
    for k in range(1, LRU_CONV):
        u = u + cw_ref[LRU_CONV - 1 - k:LRU_CONV - k, :] * _shift_rows(rec, prev, k)
    carry_ref[...] = rec[tm - SUBLANES:, :]
    ub = u.astype(MXU_DTYPE)
    r = jax.nn.sigmoid(_block_diag_dot(ub, wa_ref) + ba_ref[...])
    ig = jax.nn.sigmoid(_block_diag_dot(ub, wx_ref) + bx_ref[...])
    log_a = LRU_C * r * jax.nn.log_sigmoid(lam_ref[...])
    a_s[...] = jnp.exp(log_a)
    th = jnp.tanh(log_a)
    b_s[...] = jnp.sqrt(-2.0 * th / (1.0 - th)) * (ig * u)

    rows = lax.broadcasted_iota(jnp.int32, (SUBLANES, 1), 0)

    def group(i, hc):
        r0 = pl.multiple_of(i * SUBLANES, SUBLANES)
        a = a_s[pl.ds(r0, SUBLANES), :]
        bv = b_s[pl.ds(r0, SUBLANES), :]
        for dlt in (1, 2, 4):
            a_sh = jnp.where(rows >= dlt, pltpu.roll(a, dlt, 0), 1.0)
            b_sh = jnp.where(rows >= dlt, pltpu.roll(bv, dlt, 0), 0.0)
            bv = a * b_sh + bv
            a = a * a_sh
        h = a * hc + bv
        b_s[pl.ds(r0, SUBLANES), :] = h
        return h[SUBLANES - 1:SUBLANES, :]

    h_ref[...] = lax.fori_loop(0, tm // SUBLANES, group, h_ref[...])
    y = jax.nn.gelu(gate_br) * b_s[...]
    m = _dot(y.astype(MXU_DTYPE), wout_ref[...])
    o_ref[0] = _layer_norm_rows(DN_ALPHA * x + m, g_ref[...], b_ref[...])


def _rglru_layer(x3, win, cw, cb, wa, ba, wx, bx, lam, wout, g, b):
    bsz, s, d = x3.shape
    tm = LRU_ROW_TILE
    xspec = pl.BlockSpec((1, tm, d), lambda i, j: (i, j, 0))
    consts = [win, cw, cb, wa, ba, wx, bx, lam, wout, g, b]
    return pl.pallas_call(
        _rglru_kernel,
        grid=(bsz, s // tm),
        in_specs=[xspec] + [_const_spec(c.shape) for c in consts],
        out_specs=xspec,
        out_shape=jax.ShapeDtypeStruct(x3.shape, F32),
        scratch_shapes=[pltpu.VMEM((SUBLANES, LRU_WIDTH), F32), pltpu.VMEM((1, LRU_WIDTH), F32),
                        pltpu.VMEM((tm, LRU_WIDTH), F32), pltpu.VMEM((tm, LRU_WIDTH), F32)],
        compiler_params=_cparams(("arbitrary", "arbitrary")),
        name="rglru_layer",
    )(x3, win, cw, cb, wa, ba, wx, bx, lam, wout, g, b)


def _moe_router_kernel(x_ref, rwt_ref, rb_ref, tri_ref, pos_ref, wts_ref, meta_ref):
    tt = x_ref.shape[0]
    x = x_ref[...]
    w = rwt_ref[...]
    xh = x.astype(MXU_DTYPE)
    wh = w.astype(MXU_DTYPE)
    xl = (x - xh.astype(F32)).astype(MXU_DTYPE)
    wl = (w - wh.astype(F32)).astype(MXU_DTYPE)
    logits = _dot_nt(wh, xh) + _dot_nt(wh, xl) + _dot_nt(wl, xh)
    scores = jax.nn.sigmoid(logits)
    biased = scores + rb_ref[...]
    rows = [biased[e:e + 1, :] for e in range(N_EXPERTS)]

    gscore = []
    for gi in range(N_GROUPS):
        a, b, c, d = rows[gi * EXPERTS_PER_GROUP:(gi + 1) * EXPERTS_PER_GROUP]
        hi1, lo1 = jnp.maximum(a, b), jnp.minimum(a, b)
        hi2, lo2 = jnp.maximum(c, d), jnp.minimum(c, d)
        gscore.append(jnp.maximum(hi1, hi2) + jnp.maximum(jnp.minimum(hi1, hi2),
                                                          jnp.maximum(lo1, lo2)))
    best = gscore[0]
    top_g = jnp.zeros((1, tt), jnp.int32)
    for gi in range(1, N_GROUPS):
        better = gscore[gi] > best
        best = jnp.where(better, gscore[gi], best)
        top_g = jnp.where(better, gi, top_g)

    def argmax_in_group(exclude):
        bv = jnp.full((1, tt), -jnp.inf, F32)
        bi = jnp.full((1, tt), -1, jnp.int32)
        for e in range(N_EXPERTS):
            ok = top_g == (e // EXPERTS_PER_GROUP)
            if exclude is not None:
                ok = ok & (exclude != e)
            better = ok & (rows[e] > bv)
            bv = jnp.where(better, rows[e], bv)
            bi = jnp.where(better, e, bi)
        return bi

    e1 = argmax_in_group(None)
    e2 = argmax_in_group(e1)
    eidx = lax.broadcasted_iota(jnp.int32, (N_EXPERTS, 1), 0)
    oh1 = eidx == e1
    oh2 = eidx == e2
    s1 = jnp.sum(jnp.where(oh1, scores, 0.0), axis=0, keepdims=True)
    s2 = jnp.sum(jnp.where(oh2, scores, 0.0), axis=0, keepdims=True)
    wsum = s1 + s2
    wts_ref[0, 0:1, :] = s1 / wsum
    wts_ref[0, 1:2, :] = s2 / wsum

    oh = jnp.where(oh1 | oh2, 1.0, 0.0)
    csum = _dot(oh.astype(MXU_DTYPE), tri_ref[...])
    excl = csum - oh
    cnt = csum[:, tt - 1:tt]
    padded = jnp.ceil(cnt * (1.0 / MOE_ALIGN)) * MOE_ALIGN
    run = jnp.zeros((1, 1), F32)
    off_rows = []
    for e in range(N_EXPERTS):
        off_rows.append(run)
        run = run + padded[e:e + 1, :]
    off = jnp.concatenate(off_rows, axis=0)
    slot = off + excl

    def flat_row_of(onehot):
        p = jnp.sum(jnp.where(onehot, slot, 0.0), axis=0, keepdims=True).astype(jnp.int32)
        return _flat_row(p)

    pos_ref[0, 0:1, :] = flat_row_of(oh1)
    pos_ref[0, 1:2, :] = flat_row_of(oh2)
    lane = lax.broadcasted_iota(jnp.int32, (1, LANES), 1)
    meta = jnp.zeros((1, LANES), F32)
    for e in range(N_EXPERTS):
        meta = jnp.where(lane == e, off[e:e + 1, :], meta)
        meta = jnp.where(lane == N_EXPERTS + e, cnt[e:e + 1, :], meta)
    meta_ref[0] = meta.astype(jnp.int32)


def _moe_router(x2, rwt, rb, tri):
    t = x2.shape[0]
    tt = tri.shape[0]
    nt = t // tt
    return pl.pallas_call(
        _moe_router_kernel,
        grid=(nt,),
        in_specs=[pl.BlockSpec((tt, D_MODEL), lambda i: (i, 0)), _const_spec(rwt.shape),
                  _const_spec(rb.shape), _const_spec(tri.shape)],
        out_specs=[pl.BlockSpec((1, TOP_K, tt), lambda i: (i, 0, 0)),
                   pl.BlockSpec((1, TOP_K, tt), lambda i: (i, 0, 0)),
                   pl.BlockSpec((1, 1, LANES), lambda i: (i, 0, 0))],
        out_shape=[jax.ShapeDtypeStruct((nt, TOP_K, tt), jnp.int32),
                   jax.ShapeDtypeStruct((nt, TOP_K, tt), F32),
                   jax.ShapeDtypeStruct((nt, 1, LANES), jnp.int32)],
        compiler_params=_cparams(("arbitrary",)),
        name="moe_router",
    )(x2, rwt, rb, tri)


def _flat_row(r):
    return (r >> _SUBLANE_SHIFT) * _TILE_ROW + (r & (SUBLANES - 1))


def _row_slice(q):
    return pl.ds(q, _LANE_TILES, stride=SUBLANES)


def _load_rows_flat(ref, r0, n):
    base = pl.multiple_of(r0 * _LANE_TILES, _TILE_ROW)
    cols = [jnp.concatenate([ref[pl.ds(base + i * _TILE_ROW + c * SUBLANES, SUBLANES), :]
                             for i in range(n // SUBLANES)], axis=0) for c in range(_LANE_TILES)]
    return jnp.concatenate(cols, axis=1)


def _store_rows_flat(ref, r0, val):
    base = pl.multiple_of(r0 * _LANE_TILES, _TILE_ROW)
    for c in range(_LANE_TILES):
        for i in range(val.shape[0] // SUBLANES):
            ref[pl.ds(base + i * _TILE_ROW + c * SUBLANES, SUBLANES), :] = (
                val[i * SUBLANES:(i + 1) * SUBLANES, c * LANES:(c + 1) * LANES])


def _moe_ffn_kernel(pos_ref, wts_ref, meta_ref, x_hbm, wg_hbm, wu_hbm, wd_hbm, g_ref, b_ref,
                    o_hbm, xs_ref, xf_ref, w_ref, w_sem, xin_ref, xin_sem, ost_ref, ost_sem,
                    *, layer, tt):
    ch = MOE_CHUNK
    row0 = pl.program_id(0) * tt
    n_in = tt // MOE_IO_ROWS
    n_out = tt // MOE_NORM_ROWS

    def x_copy(c, slot):
        src = x_hbm.at[pl.ds(row0 + c * MOE_IO_ROWS, MOE_IO_ROWS), :]
        return pltpu.make_async_copy(src, xin_ref.at[slot], xin_sem.at[slot])

    def out_copy(i, slot):
        dst = o_hbm.at[pl.ds(row0 + i * MOE_NORM_ROWS, MOE_NORM_ROWS), :]
        return pltpu.make_async_copy(ost_ref.at[slot], dst, ost_sem.at[slot])

    def weight_copy(e, slot, k):
        src = (wg_hbm, wu_hbm, wd_hbm)[k]
        return pltpu.make_async_copy(src.at[layer, e], w_ref.at[slot, k], w_sem.at[slot, k])

    def start_weights(e, slot):
        for k in range(3):
            weight_copy(e, slot, k).start()

    def wait_weights(e, slot):
        for k in range(3):
            weight_copy(e, slot, k).wait()

    x_copy(0, 0).start()
    start_weights(0, 0)

    def token_group(body):
        def group(i, carry):
            base = pl.multiple_of(i * SUBLANES, SUBLANES)
            qbase = pl.multiple_of(i * _TILE_ROW, _TILE_ROW)
            for k in range(SUBLANES):
                body(base + k, qbase + k)
            return carry

        lax.fori_loop(0, tt // SUBLANES, group, 0)

    def stage_in(c, carry):
        slot = c % 2

        @pl.when(c + 1 < n_in)
        def _():
            x_copy(c + 1, 1 - slot).start()

        x_copy(c, slot).wait()

        def to_flat(i, carry_i):
            r = pl.multiple_of(i * SUBLANES, SUBLANES)
            _store_rows_flat(xf_ref, c * MOE_IO_ROWS + r, xin_ref[slot, pl.ds(r, SUBLANES), :])
            return carry_i

        return lax.fori_loop(0, MOE_IO_ROWS // SUBLANES, to_flat, carry)

    lax.fori_loop(0, n_in, stage_in, 0)

    def zero_gap(ei, carry):
        start = meta_ref[ei] + meta_ref[N_EXPERTS + ei]
        stop = jnp.where(ei == N_EXPERTS - 1, start + ch + MOE_ALIGN,
                         meta_ref[jnp.minimum(ei + 1, N_EXPERTS - 1)])

        def zrow(r, c):
            xs_ref[_row_slice(_flat_row(r)), :] = jnp.zeros((_LANE_TILES, LANES), F32)
            return c

        return lax.fori_loop(start, stop, zrow, carry)

    lax.fori_loop(0, N_EXPERTS, zero_gap, 0)

    def scatter(t, q):
        row = xf_ref[_row_slice(q), :]
        xs_ref[_row_slice(pos_ref[t]), :] = row
        xs_ref[_row_slice(pos_ref[tt + t]), :] = row

    token_group(scatter)

    rid = lax.broadcasted_iota(jnp.int32, (ch, 1), 0)

    def expert(e, carry):
        slot = e % 2

        @pl.when(e + 1 < N_EXPERTS)
        def _():
            start_weights(e + 1, 1 - slot)

        wait_weights(e, slot)
        off = meta_ref[e]
        cnt = meta_ref[N_EXPERTS + e]

        def chunk(c, carry_c):
            r0 = pl.multiple_of(off + c * ch, MOE_ALIGN)
            xs = _load_rows_flat(xs_ref, r0, ch)
            xb = xs.astype(MXU_DTYPE)
            hid = jax.nn.silu(_dot(xb, w_ref[slot, 0])) * _dot(xb, w_ref[slot, 1])
            out = _dot(hid.astype(MXU_DTYPE), w_ref[slot, 2])
            _store_rows_flat(xs_ref, r0, jnp.where(rid + c * ch < cnt, out, xs))
            return carry_c

        return lax.fori_loop(0, (cnt + ch - 1) // ch, chunk, carry)

    lax.fori_loop(0, N_EXPERTS, expert, 0)

    def gather(t, q):
        xf_ref[_row_slice(q), :] = (DN_ALPHA * xf_ref[_row_slice(q), :]
                                    + wts_ref[t] * xs_ref[_row_slice(pos_ref[t]), :]
                                    + wts_ref[tt + t] * xs_ref[_row_slice(pos_ref[tt + t]), :])

    token_group(gather)

    def norm(i, carry):
        slot = i % 2

        @pl.when(i >= 2)
        def _():
            out_copy(i - 2, slot).wait()

        r0 = pl.multiple_of(i * MOE_NORM_ROWS, MOE_NORM_ROWS)
        y = _load_rows_flat(xf_ref, r0, MOE_NORM_ROWS)
        ost_ref[slot] = _layer_norm_rows(y, g_ref[...], b_ref[...])
        out_copy(i, slot).start()
        return carry

    lax.fori_loop(0, n_out, norm, 0)
    for i in range(max(n_out - 2, 0), n_out):
        out_copy(i, i % 2).wait()


def _moe_ffn(pos, wts, meta, x2, layer, wg, wu, wd, g, b):
    t = x2.shape[0]
    tt = pos.shape[2]
    nt = t // tt
    smem = lambda n: pl.BlockSpec((n,), lambda i: (i,), memory_space=pltpu.SMEM)
    hbm = pl.BlockSpec(memory_space=pl.ANY)
    xs_rows = TOP_K * tt + N_EXPERTS * MOE_ALIGN + MOE_CHUNK + MOE_ALIGN
    pos, wts, meta = pos.reshape(-1), wts.reshape(-1), meta.reshape(-1)
    assert wg.shape[2:] == wu.shape[2:] == (D_MODEL, D_FF) and wd.shape[2:] == (D_FF, D_MODEL)
    assert D_FF == D_MODEL
    return pl.pallas_call(
        functools.partial(_moe_ffn_kernel, layer=layer, tt=tt),
        grid=(nt,),
        in_specs=[smem(TOP_K * tt), smem(TOP_K * tt), smem(LANES), hbm,
                  hbm, hbm, hbm, _const_spec((1, D_MODEL)), _const_spec((1, D_MODEL))],
        out_specs=hbm,
        out_shape=jax.ShapeDtypeStruct((t, D_MODEL), F32),
        scratch_shapes=[pltpu.VMEM((xs_rows * _LANE_TILES, LANES), F32),
                        pltpu.VMEM((tt * _LANE_TILES, LANES), F32),
                        pltpu.VMEM((2, 3, D_MODEL, D_FF), MXU_DTYPE),
                        pltpu.SemaphoreType.DMA((2, 3)),
                        pltpu.VMEM((2, MOE_IO_ROWS, D_MODEL), F32),
                        pltpu.SemaphoreType.DMA((2,)),
                        pltpu.VMEM((2, MOE_NORM_ROWS, D_MODEL), F32),
                        pltpu.SemaphoreType.DMA((2,))],
        compiler_params=_cparams(("arbitrary",)),
        name="moe_ffn",
    )(pos, wts, meta, x2, wg, wu, wd, g, b)


def _nsa_weight_layout(w_in, b_gate):
    G, R, Dh = NSA_GROUPS, NSA_HPG, HEAD_DIM
    d_in = w_in.shape[0]
    kvw = NSA_KV_COLS
    wq = w_in[:, :NSA_Q_COLS].reshape(d_in, G, R, Dh).transpose(0, 2, 1, 3).reshape(d_in, NSA_Q_COLS)
    kc, vc, ks, vs, kw, vw = (w_in[:, NSA_Q_COLS + i * kvw:NSA_Q_COLS + (i + 1) * kvw] for i in range(6))
    gate_base = NSA_Q_COLS + 6 * kvw

    def gate_layout(v):
        lead = v.shape[:-1]
        v = jnp.swapaxes(v.reshape(lead + (G, R, 3)), -1, -2).reshape(lead + (G, 3 * R))
        v = jnp.pad(v, [(0, 0)] * (len(lead) + 1) + [(0, LANES - 3 * R)])
        return v.reshape(lead + (_NSA_GATE_COLS,))

    wg = gate_layout(w_in[:, gate_base:gate_base + 3 * NSA_HEADS])
    w = jnp.concatenate([wq, ks, kw, kc, vc, vs, vw, wg], axis=1).astype(MXU_DTYPE)
    return w, gate_layout(b_gate)[None, :]


def _nsa_out_layout(w_out):
    G, R, Dh = NSA_GROUPS, NSA_HPG, HEAD_DIM
    d_out = w_out.shape[1]
    w = w_out.reshape(G, R, Dh, d_out).transpose(1, 0, 2, 3).reshape(NSA_Q_COLS, d_out)
    return w.astype(MXU_DTYPE)


def _compress_weight_layout(pe, w1, w2):
    G, Dh, L = NSA_GROUPS, HEAD_DIM, CMP_BLOCK
    eye = jnp.eye(G, dtype=w1.dtype)
    w1r = w1.reshape(L, Dh, Dh)
    w1big = jnp.einsum("ldj,gh->lgdhj", w1r, eye).reshape(L * G * Dh, G * Dh)
    w2big = jnp.einsum("dj,gh->gdhj", w2, eye).reshape(G * Dh, G * Dh)
    pebig = jnp.broadcast_to(pe[:, None, :], (L, G, Dh)).reshape(1, L * G * Dh)
    return pebig, w1big.astype(MXU_DTYPE), w2big.astype(MXU_DTYPE)


def _block_diag(w):
    h, bi, bj = w.shape
    eye = jnp.eye(h, dtype=w.dtype)
    return jnp.einsum("hij,hk->hikj", w, eye).reshape(h * bi, h * bj).astype(MXU_DTYPE)


def _nsa_constants(s):
    nc = s // CMP_STRIDE
    n_cmp = (s - CMP_BLOCK) // CMP_STRIDE + 1
    n_sel = s // SEL_BLOCK
    cmp_start = np.arange(nc) * CMP_STRIDE
    cmp_end = cmp_start + CMP_BLOCK - 1
    sel_start = np.arange(n_sel) * SEL_BLOCK
    ov = ((cmp_start[:, None] <= sel_start[None, :] + SEL_BLOCK - 1)
          & (cmp_end[:, None] >= sel_start[None, :])
          & (np.arange(nc)[:, None] < n_cmp)).astype(np.float32)
    ov = np.pad(ov, ((0, 0), (0, LANES - n_sel)))
    et = (np.arange(s)[:, None] // SEL_BLOCK == np.arange(LANES)[None, :]).astype(np.float32)
    return jnp.asarray(ov), jnp.asarray(et, dtype=MXU_DTYPE)


def _nsa_layer(x2, bsz, s, tables, cmp_tables, w_in, b_gate, pos_k, w1_k, w2_k, pos_v, w1_v, w2_v,
               w_out, g, b):
    w_perm, bg_perm = _nsa_weight_layout(w_in, b_gate)
    q, ks, kw, kc, vc, vs, vw, gates = _nsa_proj(x2, w_perm, bg_perm, *tables)
    nc = s // CMP_STRIDE
    feat = CMP_STRIDE * NSA_KV_COLS
    pek, w1k, w2k = _compress_weight_layout(pos_k, w1_k, w2_k)
    pev, w1v, w2v = _compress_weight_layout(pos_v, w1_v, w2_v)
    kcmp, vcmp = _nsa_compress(kc.reshape(bsz, nc, feat), vc.reshape(bsz, nc, feat),
                               pek, pev, w1k, w1v, w2k, w2v, *cmp_tables)
    ov, et = _nsa_constants(s)
    r3 = lambda a: a.reshape(bsz, s, a.shape[-1])
    o = _nsa_attn(r3(q), r3(ks), r3(vs), r3(kw), r3(vw), kcmp, vcmp, r3(gates), ov, et)
    return _mm_res_ln(o.reshape(bsz * s, NSA_Q_COLS), _nsa_out_layout(w_out), x2, g, b)


def _moe_layer(x2, rwt, rb, tri, layer, wg, wu, wd, g, b):
    pos, wts, meta = _moe_router(x2, rwt, rb, tri)
    return _moe_ffn(pos, wts, meta, x2, layer, wg, wu, wd, g, b)


def kernel(x, positions, nsa_w_in, nsa_b_gate, nsa_cmp_pos_k, nsa_cmp_w1_k, nsa_cmp_w2_k, nsa_cmp_pos_v, nsa_cmp_w1_v, nsa_cmp_w2_v, nsa_w_out, sc_w_in, sc_conv_w, sc_conv_b, sc_w_out, lru_w_in, lru_conv_w, lru_conv_b, lru_wa, lru_ba, lru_wx, lru_bx, lru_lambda, lru_w_out, router_w, router_b, moe_w_gate, moe_w_up, moe_w_down, ln_g, ln_b):
    bsz, s, d = x.shape
    t = bsz * s
    x2 = x.reshape(t, d)
    row = lambda v: v.reshape(1, -1)

    tables = _rope_tables(positions.reshape(t))
    nc = s // CMP_STRIDE
    cmp_idx = np.minimum(np.arange(nc) * CMP_STRIDE + CMP_BLOCK - 1, s - 1)
    cmp_tables = tuple(tb.reshape(bsz, nc, LANES)
                       for tb in _rope_tables(positions[:, cmp_idx].reshape(bsz * nc)))

    tt = min(MOE_TOKENS, t)
    tri = jnp.asarray(np.triu(np.ones((tt, tt), np.float32)), dtype=MXU_DTYPE)
    rwt = router_w.T
    rb = router_b.reshape(N_EXPERTS, 1)
    moe_wg, moe_wu, moe_wd = (w.astype(MXU_DTYPE) for w in (moe_w_gate, moe_w_up, moe_w_down))

    for i in range(DEPTH):
        kind, j = i % N_MIXERS, i // N_MIXERS
        g0, b0 = row(ln_g[i, 0]), row(ln_b[i, 0])
        if kind == 0:
            x2 = _nsa_layer(x2, bsz, s, tables, cmp_tables, nsa_w_in[j], nsa_b_gate[j],
                            nsa_cmp_pos_k[j], nsa_cmp_w1_k[j], nsa_cmp_w2_k[j],
                            nsa_cmp_pos_v[j], nsa_cmp_w1_v[j], nsa_cmp_w2_v[j], nsa_w_out[j], g0, b0)
        elif kind == 1:
            x2 = _shortconv_layer(x2.reshape(bsz, s, d), sc_w_in[j].astype(MXU_DTYPE), sc_conv_w[j],
                                  row(sc_conv_b[j]), sc_w_out[j].astype(MXU_DTYPE), g0, b0).reshape(t, d)
        else:
            x2 = _rglru_layer(x2.reshape(bsz, s, d), lru_w_in[j].astype(MXU_DTYPE), lru_conv_w[j],
                              row(lru_conv_b[j]), _block_diag(lru_wa[j]), row(lru_ba[j]),
                              _block_diag(lru_wx[j]), row(lru_bx[j]), row(lru_lambda[j]),
                              lru_w_out[j].astype(MXU_DTYPE), g0, b0).reshape(t, d)
        x2 = _moe_layer(x2, rwt, rb, tri, i, moe_wg, moe_wu, moe_wd, row(ln_g[i, 1]), row(ln_b[i, 1]))
    return x2.reshape(bsz, s, d)
```

```python
import functools

import numpy as np
import jax
import jax.numpy as jnp
from jax import lax
from jax.experimental import pallas as pl
from jax.experimental.pallas import tpu as pltpu

F32 = jnp.float32
BF16 = jnp.bfloat16
MXU_DTYPE = BF16

D_MODEL = 1024
DEPTH = 4
N_MIXERS = 3
NSA_HEADS = 16
NSA_GROUPS = 4
NSA_HPG = NSA_HEADS // NSA_GROUPS
HEAD_DIM = D_MODEL // NSA_HEADS
CMP_BLOCK = 32
CMP_STRIDE = 16
SEL_BLOCK = 64
SEL_TOP = 16
WINDOW = 512
ROPE_THETA = 10000.0
FORCE_SCORE = 1e6
NSA_Q_COLS = NSA_HEADS * HEAD_DIM
NSA_KV_COLS = NSA_GROUPS * HEAD_DIM
SC_WIDTH = 3
LRU_WIDTH = 1280
LRU_BLOCKS = 16
LRU_BW = LRU_WIDTH // LRU_BLOCKS
LRU_CONV = 4
LRU_C = 8.0
N_EXPERTS = 16
N_GROUPS = 4
EXPERTS_PER_GROUP = N_EXPERTS // N_GROUPS
TOP_K = 2
D_FF = 1024
DN_ALPHA = (2.0 * DEPTH) ** 0.25
LN_EPS = 1e-5
NEG_INF = -1e30
LOG2_E = 1.4426950408889634

LANES = 128
SUBLANES = 8
VMEM_LIMIT_BYTES = 56 * 1024 * 1024
ROW_TILE = 512
OUT_PROJ_ROW_TILE = 1024
LRU_ROW_TILE = 256
ATT_TQ = 512
ATT_TK = 512
MOE_TOKENS = 2048
MOE_CUMSUM_BLOCK = 256
MOE_CHUNK = 144
MOE_NORM_ROWS = 128
MOE_ALIGN = SUBLANES
_SUBLANE_SHIFT = 3
_LANE_TILES = D_MODEL // LANES
_TILE_ROW = SUBLANES * _LANE_TILES


def _cparams(sem):
    return pltpu.CompilerParams(dimension_semantics=sem, vmem_limit_bytes=VMEM_LIMIT_BYTES)


def _dot(a, b):
    return jnp.dot(a, b, preferred_element_type=F32)


def _dot_nt(a, b, precision=None):
    return lax.dot_general(a, b, (((1,), (1,)), ((), ())), preferred_element_type=F32,
                           precision=precision)


def _layer_norm_rows(y, g, b):
    mu = jnp.mean(y, axis=-1, keepdims=True)
    d = y - mu
    var = jnp.mean(d * d, axis=-1, keepdims=True)
    return d * lax.rsqrt(var + LN_EPS) * g + b


def _const_spec(shape):
    nd = len(shape)
    return pl.BlockSpec(shape, lambda *_: (0,) * nd)


def _rope_tables_kernel(pos_ref, inv_ref, cos_ref, sina_ref, sinb_ref):
    ang = pos_ref[...] * inv_ref[...]
    c = jnp.cos(ang)
    s = jnp.sin(ang)
    lane = lax.broadcasted_iota(jnp.int32, ang.shape, 1)
    first = (lane % HEAD_DIM) < (HEAD_DIM // 2)
    cos_ref[...] = c
    sina_ref[...] = jnp.where(first, -s, 0.0)
    sinb_ref[...] = jnp.where(first, 0.0, s)


def _rope_tables(pos_flat):
    n = pos_flat.shape[0]
    tm = min(n, 1024)
    half = HEAD_DIM // 2
    inv = ROPE_THETA ** (-jnp.arange(half, dtype=F32) / half)
    inv_lane = jnp.tile(inv, LANES // half)[None, :]
    posb = jnp.broadcast_to(pos_flat.astype(F32)[:, None], (n, LANES))
    spec = pl.BlockSpec((tm, LANES), lambda i: (i, 0))
    out = jax.ShapeDtypeStruct((n, LANES), F32)
    return pl.pallas_call(
        _rope_tables_kernel,
        grid=(n // tm,),
        in_specs=[spec, _const_spec((1, LANES))],
        out_specs=[spec, spec, spec],
        out_shape=[out, out, out],
        compiler_params=_cparams(("arbitrary",)),
        name="rope_tables",
    )(posb, inv_lane)


def _rope128(v, cos, sina, sinb):
    half = HEAD_DIM // 2
    return (v * cos + pltpu.roll(v, LANES - half, 1) * sina + pltpu.roll(v, half, 1) * sinb)


_NSA_SEG = {
    "q": 0, "ks": 1024, "kw": 1280, "kc": 1536, "vc": 1792, "vs": 2048, "vw": 2304, "gate": 2560,
}
_NSA_GATE_COLS = NSA_GROUPS * LANES
_NSA_PROJ_COLS = 2560 + _NSA_GATE_COLS


def _nsa_proj_kernel(x_ref, w_ref, bg_ref, cos_ref, sina_ref, sinb_ref,
                     q_ref, ks_ref, kw_ref, kc_ref, vc_ref, vs_ref, vw_ref, g_ref, chunk_ref):
    xb = x_ref[...].astype(MXU_DTYPE)
    cos = cos_ref[...]
    sina = sina_ref[...]
    sinb = sinb_ref[...]
    scale = HEAD_DIM ** -0.5 * LOG2_E

    def seg(name, width=256, c=0):
        o = _NSA_SEG[name] + c * 256
        return _dot(xb, w_ref[:, o:o + width])

    for c in range(NSA_Q_COLS // 256):
        acc = seg("q", c=c)
        for h in range(2):
            r = _rope128(acc[:, h * LANES:(h + 1) * LANES], cos, sina, sinb) * scale
            q_ref[:, c * 256 + h * LANES:c * 256 + (h + 1) * LANES] = r.astype(q_ref.dtype)
    for name, ref in (("ks", ks_ref), ("kw", kw_ref)):
        acc = seg(name)
        for h in range(2):
            r = _rope128(acc[:, h * LANES:(h + 1) * LANES], cos, sina, sinb)
            ref[:, h * LANES:(h + 1) * LANES] = r.astype(ref.dtype)
    n_chunks = kc_ref.shape[0]
    for name, ref in (("kc", kc_ref), ("vc", vc_ref)):
        acc = seg(name)
        for h in range(2):
            chunk_ref[h] = acc[:, h * LANES:(h + 1) * LANES]
        for l in range(CMP_STRIDE):
            for h in range(2):
                lo = l * NSA_KV_COLS + h * LANES
                ref[:, lo:lo + LANES] = chunk_ref[h, pl.ds(l, n_chunks, stride=CMP_STRIDE), :]
    vs_ref[...] = seg("vs").astype(vs_ref.dtype)
    vw_ref[...] = seg("vw").astype(vw_ref.dtype)
    g_ref[...] = jax.nn.sigmoid(seg("gate", width=_NSA_GATE_COLS) + bg_ref[...])


def _nsa_proj(x2, w_perm, bg_perm, cos_t, sina_t, sinb_t):
    t = x2.shape[0]
    tm = ROW_TILE
    row = lambda w: pl.BlockSpec((tm, w), lambda i: (i, 0))
    chunk_cols = CMP_STRIDE * NSA_KV_COLS
    chunk = pl.BlockSpec((tm // CMP_STRIDE, chunk_cols), lambda i: (i, 0))
    outs = [
        jax.ShapeDtypeStruct((t, NSA_Q_COLS), MXU_DTYPE),
        jax.ShapeDtypeStruct((t, NSA_KV_COLS), MXU_DTYPE),
        jax.ShapeDtypeStruct((t, NSA_KV_COLS), MXU_DTYPE),
        jax.ShapeDtypeStruct((t // CMP_STRIDE, chunk_cols), F32),
        jax.ShapeDtypeStruct((t // CMP_STRIDE, chunk_cols), F32),
        jax.ShapeDtypeStruct((t, NSA_KV_COLS), MXU_DTYPE),
        jax.ShapeDtypeStruct((t, NSA_KV_COLS), MXU_DTYPE),
        jax.ShapeDtypeStruct((t, _NSA_GATE_COLS), F32),
    ]
    return pl.pallas_call(
        _nsa_proj_kernel,
        grid=(t // tm,),
        in_specs=[row(D_MODEL), _const_spec((D_MODEL, _NSA_PROJ_COLS)),
                  _const_spec((1, _NSA_GATE_COLS)), row(LANES), row(LANES), row(LANES)],
        out_specs=[row(NSA_Q_COLS), row(NSA_KV_COLS), row(NSA_KV_COLS), chunk, chunk,
                   row(NSA_KV_COLS), row(NSA_KV_COLS), row(_NSA_GATE_COLS)],
        out_shape=outs,
        scratch_shapes=[pltpu.VMEM((2, tm, LANES), F32)],
        compiler_params=_cparams(("arbitrary",)),
        name="nsa_proj",
    )(x2, w_perm, bg_perm, cos_t, sina_t, sinb_t)


def _nsa_compress_kernel(kc_ref, vc_ref, pek_ref, pev_ref, w1k_ref, w1v_ref, w2k_ref, w2v_ref,
                         cos_ref, sina_ref, sinb_ref, ko_ref, vo_ref):
    nc = kc_ref.shape[1]
    half_feat = kc_ref.shape[2]
    rows = lax.broadcasted_iota(jnp.int32, (nc, 1), 0)
    valid = rows < nc - 1

    def compress(tok, pe_ref, w1_ref, w2_ref):
        a = (tok + pe_ref[:, :half_feat]).astype(MXU_DTYPE)
        b = (tok + pe_ref[:, half_feat:]).astype(MXU_DTYPE)
        pa = _dot(a, w1_ref[:half_feat, :])
        pb = _dot(b, w1_ref[half_feat:, :])
        pre = pa + pltpu.roll(pb, nc - 1, 0)
        h = jax.nn.gelu(pre)
        return _dot(h.astype(MXU_DTYPE), w2_ref[...])

    kcmp = compress(kc_ref[0], pek_ref, w1k_ref, w2k_ref)
    vcmp = compress(vc_ref[0], pev_ref, w1v_ref, w2v_ref)
    cos = cos_ref[0]
    sina = sina_ref[0]
    sinb = sinb_ref[0]
    for h in range(NSA_KV_COLS // LANES):
        r = _rope128(kcmp[:, h * LANES:(h + 1) * LANES], cos, sina, sinb)
        ko_ref[0, :, h * LANES:(h + 1) * LANES] = jnp.where(valid, r, 0.0).astype(ko_ref.dtype)
    vo_ref[0] = jnp.where(valid, vcmp, 0.0).astype(vo_ref.dtype)


def _nsa_compress(kc3, vc3, pek, pev, w1k, w1v, w2k, w2v, cos_c, sina_c, sinb_c):
    b, nc, feat = kc3.shape
    tok = pl.BlockSpec((1, nc, feat), lambda i: (i, 0, 0))
    tab = pl.BlockSpec((1, nc, LANES), lambda i: (i, 0, 0))
    out = pl.BlockSpec((1, nc, NSA_KV_COLS), lambda i: (i, 0, 0))
    oshape = jax.ShapeDtypeStruct((b, nc, NSA_KV_COLS), MXU_DTYPE)
    return pl.pallas_call(
        _nsa_compress_kernel,
        grid=(b,),
        in_specs=[tok, tok, _const_spec(pek.shape), _const_spec(pev.shape),
                  _const_spec(w1k.shape), _const_spec(w1v.shape),
                  _const_spec(w2k.shape), _const_spec(w2v.shape), tab, tab, tab],
        out_specs=[out, out],
        out_shape=[oshape, oshape],
        compiler_params=_cparams(("arbitrary",)),
        name="nsa_compress",
    )(kc3, vc3, pek, pev, w1k, w1v, w2k, w2v, cos_c, sina_c, sinb_c)


def _nsa_attn_kernel(q_ref, ks_ref, vs_ref, kw_ref, vw_ref, kc_ref, vc_ref, g_ref, ov_ref, et_ref,
                     o_ref, m_ref, acc_ref, sel_ref, *, n_sel, k_sel):
    tq = q_ref.shape[1]
    tk = ATT_TK
    nc = kc_ref.shape[1]
    R = NSA_HPG
    qi = pl.program_id(1)
    g = pl.program_id(2)
    t0 = qi * tq
    trow = t0 + lax.broadcasted_iota(jnp.int32, (tq, 1), 0)
    gmask = lax.broadcasted_iota(jnp.int32, (1, NSA_KV_COLS), 1) // HEAD_DIM == g
    gates = g_ref[0]
    low_half = g < NSA_GROUPS // 2
    in_group = (lax.broadcasted_iota(jnp.int32, (1, LANES), 1) // HEAD_DIM
                == g % (LANES // HEAD_DIM))

    def value_tile(v):
        return jnp.where(in_group, jnp.where(low_half, v[:, :LANES], v[:, LANES:]), 1)

    qs = jnp.concatenate(
        [jnp.where(gmask, q_ref[0, :, r * NSA_KV_COLS:(r + 1) * NSA_KV_COLS], 0)
         for r in range(R)], axis=0)
    rows = lambda a, r: a[r * tq:(r + 1) * tq]

    def tile_update(c, k_ref, v_ref, k0, bias):
        kt = k_ref[0, pl.ds(k0, tk), :]
        vt = value_tile(v_ref[0, pl.ds(k0, tk), :])
        s_all = _dot_nt(qs, kt)
        ps, alphas = [], []
        for r in range(R):
            s = rows(s_all, r) + bias
            m_old = m_ref[c, r]
            m_new = jnp.maximum(m_old, jnp.max(s, axis=-1, keepdims=True))
            m_ref[c, r] = m_new
            alphas.append(jnp.exp2(m_old - m_new))
            m_wide = jnp.concatenate([m_new] * (tk // LANES), axis=1)
            ps.append(jnp.exp2(s - m_wide).astype(MXU_DTYPE))
        pv = _dot(jnp.concatenate(ps, axis=0), vt)
        for r in range(R):
            acc_ref[c, r] = alphas[r] * acc_ref[c, r] + rows(pv, r)

    n_idx = lax.broadcasted_iota(jnp.int32, (1, nc), 1)
    cmp_ok = (n_idx * CMP_STRIDE + CMP_BLOCK - 1) <= trow
    s_all = _dot_nt(qs, kc_ref[0])
    psum = jnp.zeros((tq, nc), F32)
    ps = []
    for r in range(R):
        s = jnp.where(cmp_ok, rows(s_all, r), NEG_INF)
        e = jnp.where(cmp_ok, jnp.exp2(s - jnp.max(s, axis=-1, keepdims=True)), 0.0)
        p = e * (1.0 / jnp.maximum(jnp.sum(e, axis=-1, keepdims=True), 1e-30))
        psum = psum + p
        ps.append(p.astype(MXU_DTYPE))
    o_cmp = _dot(jnp.concatenate(ps, axis=0), value_tile(vc_ref[0]))

    all_selected = t0 + tq <= k_sel * SEL_BLOCK

    @pl.when(all_selected)
    def _():
        sel_ref[...] = jnp.ones(sel_ref.shape, sel_ref.dtype)

    @pl.when(jnp.logical_not(all_selected))
    def _():
        imp = jnp.dot(psum, ov_ref[...], preferred_element_type=F32,
                      precision=lax.Precision.HIGHEST)
        blk = lax.broadcasted_iota(jnp.int32, (1, LANES), 1)
        rel = trow // SEL_BLOCK - blk
        imp = jnp.where(rel >= 0, imp, -1.0)
        forced = (blk == 0) | (rel == 0) | (rel == 1)
        imp = jnp.where(forced, FORCE_SCORE, imp)
        imp_t = jnp.transpose(imp)[:n_sel, :]
        midx = lax.broadcasted_iota(jnp.int32, (n_sel, 1), 0)
        cnt = jnp.zeros((n_sel, tq), F32)
        for mp in range(n_sel):
            row = imp_t[mp:mp + 1, :]
            beats = (row > imp_t) | ((row == imp_t) & (midx > mp))
            cnt = cnt + jnp.where(beats, 1.0, 0.0)
        sel_t = jnp.where(cnt < k_sel, 1.0, 0.0)
        if n_sel < LANES:
            sel_t = jnp.concatenate([sel_t, jnp.zeros((LANES - n_sel, tq), F32)], axis=0)
        sel_ref[...] = jnp.transpose(sel_t).astype(sel_ref.dtype)

    sel = sel_ref[...]

    SEL, WIN = 0, 1
    m_ref[...] = jnp.full(m_ref.shape, NEG_INF, F32)
    acc_ref[...] = jnp.zeros(acc_ref.shape, F32)

    def sel_update(k0, kpos):
        selm = _dot_nt(sel, et_ref[pl.ds(k0, tk), :])
        bias = jnp.where((selm > 0.5) & (kpos <= trow), 0.0, NEG_INF)
        tile_update(SEL, ks_ref, vs_ref, k0, bias)

    def win_update(k0, kpos):
        d = trow - kpos
        bias = jnp.where((d >= 0) & (d < WINDOW), 0.0, NEG_INF)
        tile_update(WIN, kw_ref, vw_ref, k0, bias)

    def key_tile(j):
        k0 = pl.multiple_of(j * tk, tk)
        return k0, k0 + lax.broadcasted_iota(jnp.int32, (1, tk), 1)

    def sel_only(j, carry):
        sel_update(*key_tile(j))
        return carry

    def sel_and_win(j, carry):
        k0, kpos = key_tile(j)
        sel_update(k0, kpos)
        win_update(k0, kpos)
        return carry

    j_win = jnp.maximum(t0 - WINDOW, 0) // tk
    lax.fori_loop(0, j_win, sel_only, 0)
    lax.fori_loop(j_win, (t0 + tq) // tk, sel_and_win, 0)

    def normalized(c, r):
        acc = acc_ref[c, r]
        return acc * (1.0 / pltpu.roll(acc, HEAD_DIM, 1))

    o_heads = [gates[:, r:r + 1] * rows(o_cmp, r)
               + gates[:, R + r:R + r + 1] * normalized(SEL, r)
               + gates[:, 2 * R + r:2 * R + r + 1] * normalized(WIN, r) for r in range(R)]

    for gg in range(NSA_GROUPS):
        @pl.when(g == gg)
        def _(gg=gg):
            src = gg % (LANES // HEAD_DIM) * HEAD_DIM
            for r in range(R):
                lo = r * NSA_KV_COLS + gg * HEAD_DIM
                o_ref[0, :, lo:lo + HEAD_DIM] = o_heads[r][:, src:src + HEAD_DIM].astype(o_ref.dtype)


def _nsa_attn(q, ks, vs, kw, vw, kcmp, vcmp, gates, ov, et):
    b, s, _ = q.shape
    nc = kcmp.shape[1]
    tq = ATT_TQ
    n_sel = s // SEL_BLOCK
    k_sel = min(SEL_TOP, n_sel)
    qspec = pl.BlockSpec((1, tq, NSA_Q_COLS), lambda i, j, g: (i, j, 0))
    kv = pl.BlockSpec((1, s, NSA_KV_COLS), lambda i, j, g: (i, 0, 0))
    cmp_spec = pl.BlockSpec((1, nc, NSA_KV_COLS), lambda i, j, g: (i, 0, 0))
    gspec = pl.BlockSpec((1, tq, LANES), lambda i, j, g: (i, j, g))
    return pl.pallas_call(
        functools.partial(_nsa_attn_kernel, n_sel=n_sel, k_sel=k_sel),
        grid=(b, s // tq, NSA_GROUPS),
        in_specs=[qspec, kv, kv, kv, kv, cmp_spec, cmp_spec, gspec,
                  _const_spec(ov.shape), _const_spec(et.shape)],
        out_specs=qspec,
        out_shape=jax.ShapeDtypeStruct((b, s, NSA_Q_COLS), MXU_DTYPE),
        scratch_shapes=[pltpu.VMEM((2, NSA_HPG, tq, LANES), F32),
                        pltpu.VMEM((2, NSA_HPG, tq, LANES), F32),
                        pltpu.VMEM((tq, LANES), MXU_DTYPE)],
        compiler_params=_cparams(("arbitrary", "arbitrary", "arbitrary")),
        name="nsa_attn",
    )(q, ks, vs, kw, vw, kcmp, vcmp, gates, ov, et)


def _mm_res_ln_kernel(a_ref, w_ref, x_ref, g_ref, b_ref, o_ref):
    m = _dot(a_ref[...].astype(MXU_DTYPE), w_ref[...])
    o_ref[...] = _layer_norm_rows(DN_ALPHA * x_ref[...] + m, g_ref[...], b_ref[...])


def _mm_res_ln(a, w, x2, g, b):
    t, k = a.shape
    tm = min(OUT_PROJ_ROW_TILE, t)
    return pl.pallas_call(
        _mm_res_ln_kernel,
        grid=(t // tm,),
        in_specs=[pl.BlockSpec((tm, k), lambda i: (i, 0)), _const_spec(w.shape),
                  pl.BlockSpec((tm, D_MODEL), lambda i: (i, 0)),
                  _const_spec((1, D_MODEL)), _const_spec((1, D_MODEL))],
        out_specs=pl.BlockSpec((tm, D_MODEL), lambda i: (i, 0)),
        out_shape=jax.ShapeDtypeStruct((t, D_MODEL), F32),
        compiler_params=_cparams(("arbitrary",)),
        name="mm_res_ln",
    )(a, w, x2, g, b)


def _shift_rows(v, prev, k):
    out = pltpu.roll(v, k, 0)
    rows = lax.broadcasted_iota(jnp.int32, (v.shape[0], 1), 0)
    np_rows = prev.shape[0]
    for i in range(k):
        out = jnp.where(rows == i, prev[np_rows - k + i:np_rows - k + i + 1, :], out)
    return out


def _shortconv_kernel(x_ref, win_ref, cw_ref, cb_ref, wout_ref, g_ref, b_ref, o_ref, carry_ref):
    @pl.when(pl.program_id(1) == 0)
    def _():
        carry_ref[...] = jnp.zeros(carry_ref.shape, F32)

    x = x_ref[0]
    xb = x.astype(MXU_DTYPE)
    d = D_MODEL
    u = _dot(xb, win_ref[:, 0:d])
    bg = _dot(xb, win_ref[:, d:2 * d])
    cg = _dot(xb, win_ref[:, 2 * d:3 * d])
    v = cg * u
    prev = carry_ref[...]
    z = cb_ref[...] + cw_ref[SC_WIDTH - 1:SC_WIDTH, :] * v
    for k in range(1, SC_WIDTH):
        z = z + cw_ref[SC_WIDTH - 1 - k:SC_WIDTH - k, :] * _shift_rows(v, prev, k)
    carry_ref[...] = v[v.shape[0] - SUBLANES:, :]
    m = _dot((bg * z).astype(MXU_DTYPE), wout_ref[...])
    o_ref[0] = _layer_norm_rows(DN_ALPHA * x + m, g_ref[...], b_ref[...])


def _shortconv_layer(x3, win, cw, cb, wout, g, b):
    bsz, s, d = x3.shape
    tm = ROW_TILE
    xspec = pl.BlockSpec((1, tm, d), lambda i, j: (i, j, 0))
    return pl.pallas_call(
        _shortconv_kernel,
        grid=(bsz, s // tm),
        in_specs=[xspec, _const_spec(win.shape), _const_spec(cw.shape), _const_spec(cb.shape),
                  _const_spec(wout.shape), _const_spec(g.shape), _const_spec(b.shape)],
        out_specs=xspec,
        out_shape=jax.ShapeDtypeStruct(x3.shape, F32),
        scratch_shapes=[pltpu.VMEM((SUBLANES, d), F32)],
        compiler_params=_cparams(("arbitrary", "arbitrary")),
        name="shortconv_layer",
    )(x3, win, cw, cb, wout, g, b)


def _round_down(x, m):
    return x // m * m


def _round_up(x, m):
    return -(-x // m) * m


def _block_diag_dot(u, w_ref):
    tile = 2 * LANES
    outs = []
    for c0 in range(0, LRU_WIDTH, tile):
        c1 = min(c0 + tile, LRU_WIDTH)
        lo = _round_down(_round_down(c0, LRU_BW), LANES)
        hi = min(_round_up(_round_up(c1, LRU_BW), LANES), LRU_WIDTH)
        outs.append(_dot(u[:, lo:hi], w_ref[lo:hi, c0:c1]))
    return jnp.concatenate(outs, axis=1)


def _rglru_kernel(x_ref, win_ref, cw_ref, cb_ref, wa_ref, ba_ref, wx_ref, bx_ref, lam_ref,
                  wout_ref, g_ref, b_ref, o_ref, carry_ref, h_ref, a_s, b_s):
    @pl.when(pl.program_id(1) == 0)
    def _():
        carry_ref[...] = jnp.zeros(carry_ref.shape, F32)
        h_ref[...] = jnp.zeros(h_ref.shape, F32)

    x = x_ref[0]
    tm = x.shape[0]
    xb = x.astype(MXU_DTYPE)
    w = LRU_WIDTH
    gate_br = _dot(xb, win_ref[:, 0:w])
    rec = _dot(xb, win_ref[:, w:2 * w])
    prev = carry_ref[...]
    u = cb_ref[...] + cw_ref[LRU_CONV - 1:LRU_CONV, :] * rec
    for k in range(1, LRU_CONV):
        u = u + cw_ref[LRU_CONV - 1 - k:LRU_CONV - k, :] * _shift_rows(rec, prev, k)
    carry_ref[...] = rec[tm - SUBLANES:, :]
    ub = u.astype(MXU_DTYPE)
    r = jax.nn.sigmoid(_block_diag_dot(ub, wa_ref) + ba_ref[...])
    ig = jax.nn.sigmoid(_block_diag_dot(ub, wx_ref) + bx_ref[...])
    log_a = LRU_C * r * jax.nn.log_sigmoid(lam_ref[...])
    a_s[...] = jnp.exp(log_a)
    th = jnp.tanh(log_a)
    b_s[...] = jnp.sqrt(-2.0 * th / (1.0 - th)) * (ig * u)

    rows = lax.broadcasted_iota(jnp.int32, (SUBLANES, 1), 0)

    def group(i, hc):
        r0 = pl.multiple_of(i * SUBLANES, SUBLANES)
        a = a_s[pl.ds(r0, SUBLANES), :]
        bv = b_s[pl.ds(r0, SUBLANES), :]
        for dlt in (1, 2, 4):
            a_sh = jnp.where(rows >= dlt, pltpu.roll(a, dlt, 0), 1.0)
            b_sh = jnp.where(rows >= dlt, pltpu.roll(bv, dlt, 0), 0.0)
            bv = a * b_sh + bv
            a = a * a_sh
        h = a * hc + bv
        b_s[pl.ds(r0, SUBLANES), :] = h
        return h[SUBLANES - 1:SUBLANES, :]

    h_ref[...] = lax.fori_loop(0, tm // SUBLANES, group, h_ref[...])
    y = jax.nn.gelu(gate_br) * b_s[...]
    m = _dot(y.astype(MXU_DTYPE), wout_ref[...])
    o_ref[0] = _layer_norm_rows(DN_ALPHA * x + m, g_ref[...], b_ref[...])


def _rglru_layer(x3, win, cw, cb, wa, ba, wx, bx, lam, wout, g, b):
    bsz, s, d = x3.shape
    tm = LRU_ROW_TILE
    xspec = pl.BlockSpec((1, tm, d), lambda i, j: (i, j, 0))
    consts = [win, cw, cb, wa, ba, wx, bx, lam, wout, g, b]
    return pl.pallas_call(
        _rglru_kernel,
        grid=(bsz, s // tm),
        in_specs=[xspec] + [_const_spec(c.shape) for c in consts],
        out_specs=xspec,
        out_shape=jax.ShapeDtypeStruct(x3.shape, F32),
        scratch_shapes=[pltpu.VMEM((SUBLANES, LRU_WIDTH), F32), pltpu.VMEM((1, LRU_WIDTH), F32),
                        pltpu.VMEM((tm, LRU_WIDTH), F32), pltpu.VMEM((tm, LRU_WIDTH), F32)],
        compiler_params=_cparams(("arbitrary", "arbitrary")),
        name="rglru_layer",
    )(x3, win, cw, cb, wa, ba, wx, bx, lam, wout, g, b)


def _moe_router_kernel(x_ref, rwt_ref, rb_ref, tri_ref, pos_ref, wts_ref, meta_ref):
    tt = x_ref.shape[0]
    x = x_ref[...]
    w = rwt_ref[...]
    xh = x.astype(MXU_DTYPE)
    wh = w.astype(MXU_DTYPE)
    xl = (x - xh.astype(F32)).astype(MXU_DTYPE)
    wl = (w - wh.astype(F32)).astype(MXU_DTYPE)
    logits = _dot_nt(wh, xh) + _dot_nt(wh, xl) + _dot_nt(wl, xh)
    scores = jax.nn.sigmoid(logits)
    biased = scores + rb_ref[...]
    rows = [biased[e:e + 1, :] for e in range(N_EXPERTS)]

    gscore = []
    for gi in range(N_GROUPS):
        a, b, c, d = rows[gi * EXPERTS_PER_GROUP:(gi + 1) * EXPERTS_PER_GROUP]
        hi1, lo1 = jnp.maximum(a, b), jnp.minimum(a, b)
        hi2, lo2 = jnp.maximum(c, d), jnp.minimum(c, d)
        gscore.append(jnp.maximum(hi1, hi2) + jnp.maximum(jnp.minimum(hi1, hi2),
                                                          jnp.maximum(lo1, lo2)))
    best = gscore[0]
    top_g = jnp.zeros((1, tt), jnp.int32)
    for gi in range(1, N_GROUPS):
        better = gscore[gi] > best
        best = jnp.where(better, gscore[gi], best)
        top_g = jnp.where(better, gi, top_g)

    def argmax_in_group(exclude):
        bv = jnp.full((1, tt), -jnp.inf, F32)
        bi = jnp.full((1, tt), -1, jnp.int32)
        for e in range(N_EXPERTS):
            ok = top_g == (e // EXPERTS_PER_GROUP)
            if exclude is not None:
                ok = ok & (exclude != e)
            better = ok & (rows[e] > bv)
            bv = jnp.where(better, rows[e], bv)
            bi = jnp.where(better, e, bi)
        return bi

    e1 = argmax_in_group(None)
    e2 = argmax_in_group(e1)
    eidx = lax.broadcasted_iota(jnp.int32, (N_EXPERTS, 1), 0)
    oh1 = eidx == e1
    oh2 = eidx == e2
    s1 = jnp.sum(jnp.where(oh1, scores, 0.0), axis=0, keepdims=True)
    s2 = jnp.sum(jnp.where(oh2, scores, 0.0), axis=0, keepdims=True)
    wsum = s1 + s2
    wts_ref[0, 0:1, :] = s1 / wsum
    wts_ref[0, 1:2, :] = s2 / wsum

    oh = jnp.where(oh1 | oh2, 1.0, 0.0)
    ohb = oh.astype(MXU_DTYPE)
    blk = tri_ref.shape[0]
    local = _dot(jnp.concatenate([ohb[:, i * blk:(i + 1) * blk] for i in range(tt // blk)], axis=0),
                 tri_ref[...])
    carried = jnp.zeros((N_EXPERTS, 1), F32)
    parts = []
    for i in range(tt // blk):
        loc = local[i * N_EXPERTS:(i + 1) * N_EXPERTS, :]
        parts.append(loc + carried)
        carried = carried + loc[:, blk - 1:blk]
    csum = jnp.concatenate(parts, axis=1)
    excl = csum - oh
    cnt = csum[:, tt - 1:tt]
    padded = jnp.ceil(cnt * (1.0 / MOE_ALIGN)) * MOE_ALIGN
    run = jnp.zeros((1, 1), F32)
    off_rows = []
    for e in range(N_EXPERTS):
        off_rows.append(run)
        run = run + padded[e:e + 1, :]
    off = jnp.concatenate(off_rows, axis=0)
    slot = off + excl

    def flat_row_of(onehot):
        p = jnp.sum(jnp.where(onehot, slot, 0.0), axis=0, keepdims=True).astype(jnp.int32)
        return _flat_row(p)

    pos_ref[0, 0:1, :] = flat_row_of(oh1)
    pos_ref[0, 1:2, :] = flat_row_of(oh2)
    lane = lax.broadcasted_iota(jnp.int32, (1, LANES), 1)
    meta = jnp.zeros((1, LANES), F32)
    for e in range(N_EXPERTS):
        meta = jnp.where(lane == e, off[e:e + 1, :], meta)
        meta = jnp.where(lane == N_EXPERTS + e, cnt[e:e + 1, :], meta)
    meta_ref[0] = meta.astype(jnp.int32)


def _moe_router(x2, rwt, rb, tri):
    t = x2.shape[0]
    tt = min(MOE_TOKENS, t)
    nt = t // tt
    return pl.pallas_call(
        _moe_router_kernel,
        grid=(nt,),
        in_specs=[pl.BlockSpec((tt, D_MODEL), lambda i: (i, 0)), _const_spec(rwt.shape),
                  _const_spec(rb.shape), _const_spec(tri.shape)],
        out_specs=[pl.BlockSpec((1, TOP_K, tt), lambda i: (i, 0, 0)),
                   pl.BlockSpec((1, TOP_K, tt), lambda i: (i, 0, 0)),
                   pl.BlockSpec((1, 1, LANES), lambda i: (i, 0, 0))],
        out_shape=[jax.ShapeDtypeStruct((nt, TOP_K, tt), jnp.int32),
                   jax.ShapeDtypeStruct((nt, TOP_K, tt), F32),
                   jax.ShapeDtypeStruct((nt, 1, LANES), jnp.int32)],
        compiler_params=_cparams(("arbitrary",)),
        name="moe_router",
    )(x2, rwt, rb, tri)


def _flat_row(r):
    return (r >> _SUBLANE_SHIFT) * _TILE_ROW + (r & (SUBLANES - 1))


def _row_slice(q):
    return pl.ds(q, _LANE_TILES, stride=SUBLANES)


def _load_rows_flat(ref, r0, n):
    base = pl.multiple_of(r0 * _LANE_TILES, _TILE_ROW)
    cols = [jnp.concatenate([ref[pl.ds(base + i * _TILE_ROW + c * SUBLANES, SUBLANES), :]
                             for i in range(n // SUBLANES)], axis=0) for c in range(_LANE_TILES)]
    return jnp.concatenate(cols, axis=1)


def _store_rows_flat(ref, r0, val):
    base = pl.multiple_of(r0 * _LANE_TILES, _TILE_ROW)
    for c in range(_LANE_TILES):
        for i in range(val.shape[0] // SUBLANES):
            ref[pl.ds(base + i * _TILE_ROW + c * SUBLANES, SUBLANES), :] = (
                val[i * SUBLANES:(i + 1) * SUBLANES, c * LANES:(c + 1) * LANES])


def _moe_ffn_kernel(pos_ref, wts_ref, meta_ref, x_ref, wg_hbm, wu_hbm, wd_hbm, g_ref, b_ref,
                    o_ref, xs_ref, xf_ref, w_ref, w_sem, *, layer):
    tt = x_ref.shape[0]
    ch = MOE_CHUNK

    def weight_copy(e, slot, k):
        src = (wg_hbm, wu_hbm, wd_hbm)[k]
        return pltpu.make_async_copy(src.at[layer, e], w_ref.at[slot, k], w_sem.at[slot, k])

    def start_weights(e, slot):
        for k in range(3):
            weight_copy(e, slot, k).start()

    def wait_weights(e, slot):
        for k in range(3):
            weight_copy(e, slot, k).wait()

    start_weights(0, 0)

    def token_group(body):
        def group(i, carry):
            base = pl.multiple_of(i * SUBLANES, SUBLANES)
            qbase = pl.multiple_of(i * _TILE_ROW, _TILE_ROW)
            for k in range(SUBLANES):
                body(base + k, qbase + k)
            return carry

        lax.fori_loop(0, tt // SUBLANES, group, 0)

    def to_flat(i, carry):
        r0 = pl.multiple_of(i * SUBLANES, SUBLANES)
        _store_rows_flat(xf_ref, r0, x_ref[pl.ds(r0, SUBLANES), :])
        return carry

    lax.fori_loop(0, tt // SUBLANES, to_flat, 0)

    def zero_gap(ei, carry):
        start = meta_ref[ei] + meta_ref[N_EXPERTS + ei]
        stop = jnp.where(ei == N_EXPERTS - 1, start + ch + MOE_ALIGN,
                         meta_ref[jnp.minimum(ei + 1, N_EXPERTS - 1)])

        def zrow(r, c):
            xs_ref[_row_slice(_flat_row(r)), :] = jnp.zeros((_LANE_TILES, LANES), F32)
            return c

        return lax.fori_loop(start, stop, zrow, carry)

    lax.fori_loop(0, N_EXPERTS, zero_gap, 0)

    def scatter(t, q):
        row = xf_ref[_row_slice(q), :]
        xs_ref[_row_slice(pos_ref[t]), :] = row
        xs_ref[_row_slice(pos_ref[tt + t]), :] = row

    token_group(scatter)

    rid = lax.broadcasted_iota(jnp.int32, (ch, 1), 0)

    def expert(e, carry):
        slot = e % 2

        @pl.when(e + 1 < N_EXPERTS)
        def _():
            start_weights(e + 1, 1 - slot)

        wait_weights(e, slot)
        off = meta_ref[e]
        cnt = meta_ref[N_EXPERTS + e]

        def chunk(c, carry_c):
            r0 = pl.multiple_of(off + c * ch, MOE_ALIGN)
            xs = _load_rows_flat(xs_ref, r0, ch)
            xb = xs.astype(MXU_DTYPE)
            hid = jax.nn.silu(_dot(xb, w_ref[slot, 0])) * _dot(xb, w_ref[slot, 1])
            out = _dot(hid.astype(MXU_DTYPE), w_ref[slot, 2])
            _store_rows_flat(xs_ref, r0, jnp.where(rid + c * ch < cnt, out, xs))
            return carry_c

        return lax.fori_loop(0, (cnt + ch - 1) // ch, chunk, carry)

    lax.fori_loop(0, N_EXPERTS, expert, 0)

    def gather(t, q):
        xf_ref[_row_slice(q), :] = (wts_ref[t] * xs_ref[_row_slice(pos_ref[t]), :]
                                    + wts_ref[tt + t] * xs_ref[_row_slice(pos_ref[tt + t]), :])

    token_group(gather)

    def norm(i, carry):
        r0 = pl.multiple_of(i * MOE_NORM_ROWS, MOE_NORM_ROWS)
        y = (DN_ALPHA * x_ref[pl.ds(r0, MOE_NORM_ROWS), :]
             + _load_rows_flat(xf_ref, r0, MOE_NORM_ROWS))
        o_ref[pl.ds(r0, MOE_NORM_ROWS), :] = _layer_norm_rows(y, g_ref[...], b_ref[...])
        return carry

    lax.fori_loop(0, tt // MOE_NORM_ROWS, norm, 0)


def _moe_ffn(pos, wts, meta, x2, layer, wg, wu, wd, g, b):
    t = x2.shape[0]
    tt = pos.shape[2]
    nt = t // tt
    smem = lambda n: pl.BlockSpec((n,), lambda i: (i,), memory_space=pltpu.SMEM)
    hbm = pl.BlockSpec(memory_space=pl.ANY)
    xs_rows = TOP_K * tt + N_EXPERTS * MOE_ALIGN + MOE_CHUNK + MOE_ALIGN
    pos, wts, meta = pos.reshape(-1), wts.reshape(-1), meta.reshape(-1)
    assert wg.shape[2:] == wu.shape[2:] == (D_MODEL, D_FF) and wd.shape[2:] == (D_FF, D_MODEL)
    assert D_FF == D_MODEL
    return pl.pallas_call(
        functools.partial(_moe_ffn_kernel, layer=layer),
        grid=(nt,),
        in_specs=[smem(TOP_K * tt), smem(TOP_K * tt), smem(LANES),
                  pl.BlockSpec((tt, D_MODEL), lambda i: (i, 0), pipeline_mode=pl.Buffered(1)),
                  hbm, hbm, hbm, _const_spec((1, D_MODEL)), _const_spec((1, D_MODEL))],
        out_specs=pl.BlockSpec((tt, D_MODEL), lambda i: (i, 0), pipeline_mode=pl.Buffered(1)),
        out_shape=jax.ShapeDtypeStruct((t, D_MODEL), F32),
        scratch_shapes=[pltpu.VMEM((xs_rows * _LANE_TILES, LANES), F32),
                        pltpu.VMEM((tt * _LANE_TILES, LANES), F32),
                        pltpu.VMEM((2, 3, D_MODEL, D_FF), MXU_DTYPE),
                        pltpu.SemaphoreType.DMA((2, 3))],
        compiler_params=_cparams(("arbitrary",)),
        name="moe_ffn",
    )(pos, wts, meta, x2, wg, wu, wd, g, b)


def _nsa_weight_layout(w_in, b_gate):
    G, R, Dh = NSA_GROUPS, NSA_HPG, HEAD_DIM
    d_in = w_in.shape[0]
    kvw = NSA_KV_COLS
    wq = w_in[:, :NSA_Q_COLS].reshape(d_in, G, R, Dh).transpose(0, 2, 1, 3).reshape(d_in, NSA_Q_COLS)
    kc, vc, ks, vs, kw, vw = (w_in[:, NSA_Q_COLS + i * kvw:NSA_Q_COLS + (i + 1) * kvw] for i in range(6))
    gate_base = NSA_Q_COLS + 6 * kvw

    def gate_layout(v):
        lead = v.shape[:-1]
        v = jnp.swapaxes(v.reshape(lead + (G, R, 3)), -1, -2).reshape(lead + (G, 3 * R))
        v = jnp.pad(v, [(0, 0)] * (len(lead) + 1) + [(0, LANES - 3 * R)])
        return v.reshape(lead + (_NSA_GATE_COLS,))

    wg = gate_layout(w_in[:, gate_base:gate_base + 3 * NSA_HEADS])
    w = jnp.concatenate([wq, ks, kw, kc, vc, vs, vw, wg], axis=1).astype(MXU_DTYPE)
    return w, gate_layout(b_gate)[None, :]


def _nsa_out_layout(w_out):
    G, R, Dh = NSA_GROUPS, NSA_HPG, HEAD_DIM
    d_out = w_out.shape[1]
    w = w_out.reshape(G, R, Dh, d_out).transpose(1, 0, 2, 3).reshape(NSA_Q_COLS, d_out)
    return w.astype(MXU_DTYPE)


def _compress_weight_layout(pe, w1, w2):
    G, Dh, L = NSA_GROUPS, HEAD_DIM, CMP_BLOCK
    eye = jnp.eye(G, dtype=w1.dtype)
    w1r = w1.reshape(L, Dh, Dh)
    w1big = jnp.einsum("ldj,gh->lgdhj", w1r, eye).reshape(L * G * Dh, G * Dh)
    w2big = jnp.einsum("dj,gh->gdhj", w2, eye).reshape(G * Dh, G * Dh)
    pebig = jnp.broadcast_to(pe[:, None, :], (L, G, Dh)).reshape(1, L * G * Dh)
    return pebig, w1big.astype(MXU_DTYPE), w2big.astype(MXU_DTYPE)


def _block_diag(w):
    h, bi, bj = w.shape
    eye = jnp.eye(h, dtype=w.dtype)
    return jnp.einsum("hij,hk->hikj", w, eye).reshape(h * bi, h * bj).astype(MXU_DTYPE)


def _nsa_constants(s):
    nc = s // CMP_STRIDE
    n_cmp = (s - CMP_BLOCK) // CMP_STRIDE + 1
    n_sel = s // SEL_BLOCK
    cmp_start = np.arange(nc) * CMP_STRIDE
    cmp_end = cmp_start + CMP_BLOCK - 1
    sel_start = np.arange(n_sel) * SEL_BLOCK
    ov = ((cmp_start[:, None] <= sel_start[None, :] + SEL_BLOCK - 1)
          & (cmp_end[:, None] >= sel_start[None, :])
          & (np.arange(nc)[:, None] < n_cmp)).astype(np.float32)
    ov = np.pad(ov, ((0, 0), (0, LANES - n_sel)))
    et = (np.arange(s)[:, None] // SEL_BLOCK == np.arange(LANES)[None, :]).astype(np.float32)
    return jnp.asarray(ov), jnp.asarray(et, dtype=MXU_DTYPE)


def _nsa_layer(x2, bsz, s, tables, cmp_tables, w_in, b_gate, pos_k, w1_k, w2_k, pos_v, w1_v, w2_v,
               w_out, g, b):
    w_perm, bg_perm = _nsa_weight_layout(w_in, b_gate)
    q, ks, kw, kc, vc, vs, vw, gates = _nsa_proj(x2, w_perm, bg_perm, *tables)
    nc = s // CMP_STRIDE
    feat = CMP_STRIDE * NSA_KV_COLS
    pek, w1k, w2k = _compress_weight_layout(pos_k, w1_k, w2_k)
    pev, w1v, w2v = _compress_weight_layout(pos_v, w1_v, w2_v)
    kcmp, vcmp = _nsa_compress(kc.reshape(bsz, nc, feat), vc.reshape(bsz, nc, feat),
                               pek, pev, w1k, w1v, w2k, w2v, *cmp_tables)
    ov, et = _nsa_constants(s)
    r3 = lambda a: a.reshape(bsz, s, a.shape[-1])
    o = _nsa_attn(r3(q), r3(ks), r3(vs), r3(kw), r3(vw), kcmp, vcmp, r3(gates), ov, et)
    return _mm_res_ln(o.reshape(bsz * s, NSA_Q_COLS), _nsa_out_layout(w_out), x2, g, b)


def _moe_layer(x2, rwt, rb, tri, layer, wg, wu, wd, g, b):
    pos, wts, meta = _moe_router(x2, rwt, rb, tri)
    return _moe_ffn(pos, wts, meta, x2, layer, wg, wu, wd, g, b)


def kernel(x, positions, nsa_w_in, nsa_b_gate, nsa_cmp_pos_k, nsa_cmp_w1_k, nsa_cmp_w2_k, nsa_cmp_pos_v, nsa_cmp_w1_v, nsa_cmp_w2_v, nsa_w_out, sc_w_in, sc_conv_w, sc_conv_b, sc_w_out, lru_w_in, lru_conv_w, lru_conv_b, lru_wa, lru_ba, lru_wx, lru_bx, lru_lambda, lru_w_out, router_w, router_b, moe_w_gate, moe_w_up, moe_w_down, ln_g, ln_b):
    bsz, s, d = x.shape
    t = bsz * s
    x2 = x.reshape(t, d)
    row = lambda v: v.reshape(1, -1)

    tables = _rope_tables(positions.reshape(t))
    nc = s // CMP_STRIDE
    cmp_idx = np.minimum(np.arange(nc) * CMP_STRIDE + CMP_BLOCK - 1, s - 1)
    cmp_tables = tuple(tb.reshape(bsz, nc, LANES)
                       for tb in _rope_tables(positions[:, cmp_idx].reshape(bsz * nc)))

    blk = min(MOE_CUMSUM_BLOCK, t)
    tri = jnp.asarray(np.triu(np.ones((blk, blk), np.float32)), dtype=MXU_DTYPE)
    rwt = router_w.T
    rb = router_b.reshape(N_EXPERTS, 1)
    moe_wg, moe_wu, moe_wd = (w.astype(MXU_DTYPE) for w in (moe_w_gate, moe_w_up, moe_w_down))

    for i in range(DEPTH):
        kind, j = i % N_MIXERS, i // N_MIXERS
        g0, b0 = row(ln_g[i, 0]), row(ln_b[i, 0])
        if kind == 0:
            x2 = _nsa_layer(x2, bsz, s, tables, cmp_tables, nsa_w_in[j], nsa_b_gate[j],
                            nsa_cmp_pos_k[j], nsa_cmp_w1_k[j], nsa_cmp_w2_k[j],
                            nsa_cmp_pos_v[j], nsa_cmp_w1_v[j], nsa_cmp_w2_v[j], nsa_w_out[j], g0, b0)
        elif kind == 1:
            x2 = _shortconv_layer(x2.reshape(bsz, s, d), sc_w_in[j].astype(MXU_DTYPE), sc_conv_w[j],
                                  row(sc_conv_b[j]), sc_w_out[j].astype(MXU_DTYPE), g0, b0).reshape(t, d)
        else:
            x2 = _rglru_layer(x2.reshape(bsz, s, d), lru_w_in[j].astype(MXU_DTYPE), lru_conv_w[j],
                              row(lru_conv_b[j]), _block_diag(lru_wa[j]), row(lru_ba[j]),
                              _block_diag(lru_wx[j]), row(lru_bx[j]), row(lru_lambda[j]),
                              lru_w_out[j].astype(MXU_DTYPE), g0, b0).reshape(t, d)
        x2 = _moe_layer(x2, rwt, rb, tri, i, moe_wg, moe_wu, moe_wd, row(ln_g[i, 1]), row(ln_b[i, 1]))
    return x2.reshape(bsz, s, d)
```
